```python
import jax, jax.numpy as jnp
from jax import lax
import numpy as np

D_MODEL = 4096
BATCH = 4
SEQ = 2048
DEPTH = 1
DEC_BATCH = 32
DEC_SEQ = 8
PAST_LEN = 8192
PAGE_SIZE = 128

DH_A = 128
H_A = (D_MODEL // 2) // DH_A
W_A = H_A * DH_A
DILATED_BRANCHES = ((128, 1), (512, 4), (2048, 16))
WINDOW_MAX = 2048
ROPE_THETA = 10000.0

H_B = 4
DV_B = (D_MODEL // 2) // H_B
DK_B = DV_B // 2
W_B = H_B * DV_B
GATE_RANK = 16
GATE_TAU = 16.0
GLA_CHUNK = 64

D_MIX = W_A + W_B
IN_SPLIT = (W_A, W_A, W_A, H_B * DK_B, H_B * DK_B, W_B, GATE_RANK, W_B)
IN_COLS = 3 * W_A + 2 * H_B * DK_B + 2 * W_B + GATE_RANK

N_KEYS = 128
N_EXPERTS = N_KEYS * N_KEYS
PEER_HEADS = 8
PEER_DKEY = 256
PEER_TOPK = 16
PEER_BLOCK = 64

EPS = 1e-6
NEG_INF = -1e30

kernel_name = 'hybrid_dilated_gla_peer_step'


def rms_norm(x, gain):
    x32 = x.astype(jnp.float32)
    y = x32 * lax.rsqrt(jnp.mean(x32 * x32, axis=-1, keepdims=True) + EPS)
    return (y * gain.astype(jnp.float32)).astype(x.dtype)


def rope(x, pos):
    half = DH_A // 2
    inv = ROPE_THETA ** (-jnp.arange(half, dtype=jnp.float32) / half)
    ang = pos.astype(jnp.float32)[:, None] * inv[None, :]
    cos = jnp.cos(ang)[None, :, None, :]
    sin = jnp.sin(ang)[None, :, None, :]
    x1, x2 = x[..., :half], x[..., half:]
    return jnp.concatenate([x1 * cos - x2 * sin, x1 * sin + x2 * cos], axis=-1)


def band_attend(q, k, v, w):
    n, L, H, dh = q.shape
    nb = -(-L // w)
    lp = nb * w
    pad = ((0, 0), (0, lp - L), (0, 0), (0, 0))
    qb = jnp.pad(q, pad).reshape(n, nb, w, H, dh)
    kb = jnp.pad(k, pad).reshape(n, nb, w, H, dh)
    vb = jnp.pad(v, pad).reshape(n, nb, w, H, dh)
    kk = jnp.concatenate([jnp.concatenate([jnp.zeros_like(kb[:, :1]), kb[:, :-1]], axis=1), kb], axis=2)
    vv = jnp.concatenate([jnp.concatenate([jnp.zeros_like(vb[:, :1]), vb[:, :-1]], axis=1), vb], axis=2)
    s = jnp.einsum('nbihd,nbjhd->nbhij', qb, kk) * (DH_A ** -0.5)
    i = jnp.arange(w)[:, None]
    j = jnp.arange(2 * w)[None, :]
    dist = i + w - j
    band = (dist >= 0) & (dist <= w)
    has_prev = (jnp.arange(nb) > 0)[:, None, None] | (j >= w)[None]
    mask = band[None] & has_prev
    s = jnp.where(mask[None, :, None, :, :], s, NEG_INF)
    m = jnp.max(s, axis=-1)
    p = jnp.exp(s - m[..., None])
    l = jnp.sum(p, axis=-1)
    o = jnp.einsum('nbhij,nbjhd->nbihd', p, vv) / jnp.transpose(l, (0, 1, 3, 2))[..., None]
    o = o.reshape(n, lp, H, dh)[:, :L]
    m = jnp.transpose(m, (0, 1, 3, 2)).reshape(n, lp, H)[:, :L]
    l = jnp.transpose(l, (0, 1, 3, 2)).reshape(n, lp, H)[:, :L]
    return o, m, l


def strided_band(q, k, v, window, d):
    B, L, H, dh = q.shape
    ls = L // d
    def to_res(x):
        return jnp.transpose(x.reshape(B, ls, d, H, dh), (0, 2, 1, 3, 4)).reshape(B * d, ls, H, dh)
    o, m, l = band_attend(to_res(q), to_res(k), to_res(v), window // d)
    o = jnp.transpose(o.reshape(B, d, ls, H, dh), (0, 2, 1, 3, 4)).reshape(B, L, H, dh)
    m = jnp.transpose(m.reshape(B, d, ls, H), (0, 2, 1, 3)).reshape(B, L, H)
    l = jnp.transpose(l.reshape(B, d, ls, H), (0, 2, 1, 3)).reshape(B, L, H)
    return o, m, l


def merge_branches(outs, ms, ls):
    m_all = jnp.stack(ms)
    m_max = jnp.max(m_all, axis=0)
    wgt = jnp.stack(ls) * jnp.exp(m_all - m_max)
    wgt = wgt / jnp.sum(wgt, axis=0)
    return jnp.sum(wgt[..., None] * jnp.stack(outs), axis=0)


def dilated_attention_prompt(q, k, v):
    outs, ms, ls = [], [], []
    for window, d in DILATED_BRANCHES:
        o, m, l = strided_band(q, k, v, window, d)
        outs.append(o); ms.append(m); ls.append(l)
    return merge_branches(outs, ms, ls)


def dilated_attention_gather(q, k_all, v_all, q_idx):
    outs, ms, ls = [], [], []
    for window, d in DILATED_BRANCHES:
        offs = jnp.arange(window // d + 1) * d
        kidx = q_idx[:, None] - offs[None, :]
        valid = kidx >= 0
        kidx = jnp.maximum(kidx, 0)
        kg = k_all[:, kidx]
        vg = v_all[:, kidx]
        s = jnp.einsum('bshd,bsnhd->bshn', q, kg) * (DH_A ** -0.5)
        s = jnp.where(valid[None, :, None, :], s, NEG_INF)
        m = jnp.max(s, axis=-1)
        p = jnp.exp(s - m[..., None])
        l = jnp.sum(p, axis=-1)
        o = jnp.einsum('bshn,bsnhd->bshd', p, vg) / l[..., None]
        outs.append(o); ms.append(m); ls.append(l)
    return merge_branches(outs, ms, ls)


def gla_chunked(q, k, v, log_a, s0, chunk):
    Bn, L, H, dk = q.shape
    dv = v.shape[-1]
    nc = L // chunk
    def to_chunks(x):
        return jnp.moveaxis(x.reshape(Bn, nc, chunk, H, x.shape[-1]), 1, 0)
    tril = jnp.tril(jnp.ones((chunk, chunk), dtype=bool))
    def step(S, xs):
        qc, kc, vc, gc = xs
        b = jnp.cumsum(gc, axis=1)
        b_last = b[:, -1]
        q_in = qc * jnp.exp(b)
        k_in = kc * jnp.exp(-b)
        att = jnp.where(tril[None, None], jnp.einsum('bihk,bjhk->bhij', q_in, k_in), 0.0)
        o = jnp.einsum('bhij,bjhv->bihv', att, vc) + jnp.einsum('bihk,bhkv->bihv', q_in, S)
        k_dec = kc * jnp.exp(b_last[:, None] - b)
        S = jnp.exp(b_last)[..., None] * S + jnp.einsum('bjhk,bjhv->bhkv', k_dec, vc)
        return S, o
    S, o = lax.scan(step, s0.astype(jnp.float32), (to_chunks(q), to_chunks(k), to_chunks(v), to_chunks(log_a)))
    o = jnp.moveaxis(o, 0, 1).reshape(Bn, L, H, dv)
    return o, S


def peer(h, w_q, keys1, keys2, u_tab, v_tab):
    Bn, L, D = h.shape
    T = Bn * L
    nblk = -(-T // PEER_BLOCK)
    t = jnp.pad(h.reshape(T, D), ((0, nblk * PEER_BLOCK - T), (0, 0)))
    half = PEER_DKEY // 2
    def block(tb):
        q = jnp.einsum('td,de->te', tb, w_q).astype(jnp.float32).reshape(PEER_BLOCK, PEER_HEADS, PEER_DKEY)
        q = q * lax.rsqrt(jnp.mean(q * q, axis=-1, keepdims=True) + EPS)
        s1 = jnp.einsum('thk,hnk->thn', q[..., :half], keys1.astype(jnp.float32))
        s2 = jnp.einsum('thk,hnk->thn', q[..., half:], keys2.astype(jnp.float32))
        v1, i1 = lax.top_k(s1, PEER_TOPK)
        v2, i2 = lax.top_k(s2, PEER_TOPK)
        cand = (v1[..., :, None] + v2[..., None, :]).reshape(PEER_BLOCK, PEER_HEADS, PEER_TOPK * PEER_TOPK)
        cidx = (i1[..., :, None] * N_KEYS + i2[..., None, :]).reshape(PEER_BLOCK, PEER_HEADS, PEER_TOPK * PEER_TOPK)
        top_s, sel = lax.top_k(cand, PEER_TOPK)
        eidx = jnp.take_along_axis(cidx, sel, axis=-1).reshape(PEER_BLOCK, PEER_HEADS * PEER_TOPK)
        g = jax.nn.softmax(top_s, axis=-1).reshape(PEER_BLOCK, PEER_HEADS * PEER_TOPK)
        u = jnp.take(u_tab, eidx, axis=0)
        a = jax.nn.gelu(jnp.einsum('td,tkd->tk', tb, u).astype(jnp.float32))
        vv = jnp.take(v_tab, eidx, axis=0)
        return jnp.einsum('tk,tkd->td', (g * a).astype(tb.dtype), vv)
    y = lax.map(block, t.reshape(nblk, PEER_BLOCK, D))
    return y.reshape(nblk * PEER_BLOCK, D)[:T].reshape(Bn, L, D)


def hybrid_layer(x, pos, attend, s0, chunk, norm_mix, w_in, q_norm, k_norm, w_gate, b_gate,
                 gla_norm, w_out, norm_ffn, w_peer_q, peer_keys1, peer_keys2, peer_u, peer_v):
    Bn, L, _ = x.shape
    f32 = jnp.float32
    h = rms_norm(x, norm_mix)
    z = jnp.einsum('bld,de->ble', h, w_in).astype(f32)
    split_at = np.cumsum(IN_SPLIT)[:-1].tolist()
    qa, ka, va, qb, kb, vb, g_low, r = jnp.split(z, split_at, axis=-1)
    qa = rope(rms_norm(qa.reshape(Bn, L, H_A, DH_A), q_norm), pos)
    ka = rope(rms_norm(ka.reshape(Bn, L, H_A, DH_A), k_norm), pos)
    va = va.reshape(Bn, L, H_A, DH_A)
    o_a = attend(qa, ka, va)
    log_a = jax.nn.log_sigmoid(jnp.einsum('blr,re->ble', g_low, w_gate.astype(f32)) + b_gate.astype(f32)) / GATE_TAU
    o_b, s_new = gla_chunked(qb.reshape(Bn, L, H_B, DK_B) * (DK_B ** -0.5), kb.reshape(Bn, L, H_B, DK_B),
                             vb.reshape(Bn, L, H_B, DV_B), log_a.reshape(Bn, L, H_B, DK_B), s0, chunk)
    o_b = rms_norm(o_b, gla_norm) * jax.nn.silu(r.reshape(Bn, L, H_B, DV_B))
    mix = jnp.concatenate([o_a.reshape(Bn, L, W_A), o_b.reshape(Bn, L, W_B)], axis=-1).astype(x.dtype)
    x = x + jnp.einsum('ble,ed->bld', mix, w_out)
    x = x + peer(rms_norm(x, norm_ffn), w_peer_q, peer_keys1, peer_keys2, peer_u, peer_v)
    return x, ka, va, s_new


def setup_inputs(seed: int = 0) -> dict:
    key = jax.random.key(seed)
    ks = jax.random.split(key, 20)
    f32 = jnp.float32
    nrm = jax.random.normal
    win_buf = min(WINDOW_MAX, PAST_LEN)
    return {
        'x_prompt': nrm(ks[0], (BATCH, SEQ, D_MODEL), f32),
        'x_sample': nrm(ks[1], (DEC_BATCH, DEC_SEQ, D_MODEL), f32),
        'cache_attn_k': nrm(ks[2], (DEPTH, DEC_BATCH, win_buf, H_A, DH_A), f32),
        'cache_attn_v': nrm(ks[3], (DEPTH, DEC_BATCH, win_buf, H_A, DH_A), f32),
        'state_gla': 0.5 * nrm(ks[4], (DEPTH, DEC_BATCH, H_B, DK_B, DV_B), f32),
        'norm_mix': 1.0 + 0.01 * nrm(ks[5], (DEPTH, D_MODEL), f32),
        'w_in': nrm(ks[6], (DEPTH, D_MODEL, IN_COLS), f32) * D_MODEL ** -0.5,
        'q_norm': 1.0 + 0.01 * nrm(ks[7], (DEPTH, DH_A), f32),
        'k_norm': 1.0 + 0.01 * nrm(ks[8], (DEPTH, DH_A), f32),
        'w_gate': nrm(ks[9], (DEPTH, GATE_RANK, H_B * DK_B), f32) * GATE_RANK ** -0.5,
        'b_gate': 0.01 * nrm(ks[10], (DEPTH, H_B * DK_B), f32),
        'gla_norm': 1.0 + 0.01 * nrm(ks[11], (DEPTH, DV_B), f32),
        'w_out': nrm(ks[12], (DEPTH, D_MIX, D_MODEL), f32) * D_MIX ** -0.5,
        'norm_ffn': 1.0 + 0.01 * nrm(ks[13], (DEPTH, D_MODEL), f32),
        'w_peer_q': nrm(ks[14], (DEPTH, D_MODEL, PEER_HEADS * PEER_DKEY), f32) * D_MODEL ** -0.5,
        'peer_keys1': nrm(ks[15], (DEPTH, PEER_HEADS, N_KEYS, PEER_DKEY // 2), f32) * (PEER_DKEY // 2) ** -0.5,
        'peer_keys2': nrm(ks[16], (DEPTH, PEER_HEADS, N_KEYS, PEER_DKEY // 2), f32) * (PEER_DKEY // 2) ** -0.5,
        'peer_u': nrm(ks[17], (DEPTH, N_EXPERTS, D_MODEL), f32) * D_MODEL ** -0.5,
        'peer_v': 0.5 * nrm(ks[18], (DEPTH, N_EXPERTS, D_MODEL), f32),
    }


def reference(x_prompt, x_sample, cache_attn_k, cache_attn_v, state_gla, norm_mix, w_in, q_norm, k_norm,
              w_gate, b_gate, gla_norm, w_out, norm_ffn, w_peer_q, peer_keys1, peer_keys2, peer_u, peer_v):
    f32 = jnp.float32
    seq = x_prompt.shape[1]
    dec_seq = x_sample.shape[1]
    win_buf = cache_attn_k.shape[2]
    keep_p = min(WINDOW_MAX, seq)
    pos_p = jnp.arange(seq, dtype=jnp.int32)
    pos_s = PAST_LEN + jnp.arange(dec_seq, dtype=jnp.int32)
    q_idx_s = win_buf + jnp.arange(dec_seq, dtype=jnp.int32)
    yp, ys = x_prompt, x_sample
    kp_l, vp_l, ks_l, vs_l, sp_l, ss_l = [], [], [], [], [], []
    for l in range(DEPTH):
        lw = (norm_mix[l], w_in[l], q_norm[l], k_norm[l], w_gate[l], b_gate[l], gla_norm[l], w_out[l],
              norm_ffn[l], w_peer_q[l], peer_keys1[l], peer_keys2[l], peer_u[l], peer_v[l])
        s0_p = jnp.zeros((x_prompt.shape[0], H_B, DK_B, DV_B), f32)
        yp, kp, vp, sp = hybrid_layer(yp, pos_p, dilated_attention_prompt, s0_p, GLA_CHUNK, *lw)

        def sample_attend(q, k, v, ck=cache_attn_k[l], cv=cache_attn_v[l]):
            k_all = jnp.concatenate([ck.astype(f32), k], axis=1)
            v_all = jnp.concatenate([cv.astype(f32), v], axis=1)
            return dilated_attention_gather(q, k_all, v_all, q_idx_s)

        ys, ksm, vsm, ssm = hybrid_layer(ys, pos_s, sample_attend, state_gla[l], dec_seq, *lw)
        kp_l.append(kp[:, seq - keep_p:]); vp_l.append(vp[:, seq - keep_p:])
        ks_l.append(ksm); vs_l.append(vsm)
        sp_l.append(sp); ss_l.append(ssm)
    new_k_prompt = jnp.stack(kp_l).astype(cache_attn_k.dtype)
    new_v_prompt = jnp.stack(vp_l).astype(cache_attn_v.dtype)
    new_k_sample = jnp.stack(ks_l).astype(cache_attn_k.dtype)
    new_v_sample = jnp.stack(vs_l).astype(cache_attn_v.dtype)
    new_gla_prompt = jnp.stack(sp_l).astype(state_gla.dtype)
    new_gla_sample = jnp.stack(ss_l).astype(state_gla.dtype)
    return (yp, ys, new_k_prompt, new_v_prompt, new_k_sample, new_v_sample, new_gla_prompt, new_gla_sample)
```

```python
import functools

import numpy as np
import jax
import jax.numpy as jnp
from jax import lax
from jax.experimental import pallas as pl
from jax.experimental.pallas import tpu as pltpu

F32 = jnp.float32
BF16 = jnp.bfloat16

DH_A = 128
DILATED_BRANCHES = ((128, 1), (512, 4), (2048, 16))
ROPE_THETA = 10000.0
PAST_LEN = 8192
H_B = 4
GATE_RANK = 16
GATE_TAU = 16.0
GLA_CHUNK = 64
N_KEYS = 128
PEER_HEADS = 8
PEER_DKEY = 256
PEER_TOPK = 16
EPS = 1e-6
NEG_INF = -1e30

LANES = 128
VMEM_LIMIT = 56 * 1024 * 1024


def _cparams(*sem):
    return pltpu.CompilerParams(dimension_semantics=sem, vmem_limit_bytes=VMEM_LIMIT)


def _dot(a, b):
    return jnp.dot(a, b, preferred_element_type=F32)


def _dot_nt(a, b):
    return lax.dot_general(a, b, (((1,), (1,)), ((), ())), preferred_element_type=F32)


def _dot_tn(a, b):
    return lax.dot_general(a, b, (((0,), (0,)), ((), ())), preferred_element_type=F32)


def _rmsnorm_body(x_ref, g_ref, o_ref):
    x = x_ref[...]
    ms = jnp.mean(x * x, axis=-1, keepdims=True)
    o_ref[...] = (x * lax.rsqrt(ms + EPS) * g_ref[...]).astype(o_ref.dtype)


def _rmsnorm(x, gain, tt):
    t, d = x.shape
    return pl.pallas_call(
        _rmsnorm_body,
        grid=(t // tt,),
        in_specs=[pl.BlockSpec((tt, d), lambda i: (i, 0)), pl.BlockSpec((1, d), lambda i: (0, 0))],
        out_specs=pl.BlockSpec((tt, d), lambda i: (i, 0)),
        out_shape=jax.ShapeDtypeStruct((t, d), BF16),
        compiler_params=_cparams("parallel"),
        name="rmsnorm",
    )(x, gain.reshape(1, d))


def _mm_body(a_ref, w_ref, o_ref):
    o_ref[...] = _dot(a_ref[...], w_ref[...])


def _matmul(a, w, tm, tn):
    t, k = a.shape
    n = w.shape[1]
    return pl.pallas_call(
        _mm_body,
        grid=(t // tm, n // tn),
        in_specs=[pl.BlockSpec((tm, k), lambda i, j: (i, 0)), pl.BlockSpec((k, tn), lambda i, j: (0, j))],
        out_specs=pl.BlockSpec((tm, tn), lambda i, j: (i, j)),
        out_shape=jax.ShapeDtypeStruct((t, n), F32),
        compiler_params=_cparams("parallel", "arbitrary"),
        name="matmul",
    )(a, w)


def _qk_post_body(z_ref, g_ref, cos_ref, sin_ref, o_ref, *, heads):
    cos = cos_ref[...]
    sin = sin_ref[...]
    for h in range(heads):
        sl = slice(h * DH_A, (h + 1) * DH_A)
        x = z_ref[:, sl]
        ms = jnp.mean(x * x, axis=-1, keepdims=True)
        y = x * lax.rsqrt(ms + EPS) * g_ref[:, sl]
        o_ref[:, sl] = y * cos + pltpu.roll(y, DH_A // 2, axis=1) * sin


def _qk_post(z, gains, cos2, sin2, tt, heads):
    t = z.shape[0]
    n = gains.shape[1]
    bw = heads * DH_A
    return pl.pallas_call(
        functools.partial(_qk_post_body, heads=heads),
        grid=(t // tt, n // bw),
        in_specs=[pl.BlockSpec((tt, bw), lambda i, j: (i, j)),
                  pl.BlockSpec((1, bw), lambda i, j: (0, j)),
                  pl.BlockSpec((tt, DH_A), lambda i, j: (i, 0)),
                  pl.BlockSpec((tt, DH_A), lambda i, j: (i, 0))],
        out_specs=pl.BlockSpec((tt, bw), lambda i, j: (i, j)),
        out_shape=jax.ShapeDtypeStruct((t, n), F32),
        compiler_params=_cparams("parallel", "arbitrary"),
        name="qk_post",
    )(z, gains, cos2, sin2)


def _attn_prompt_body(q_ref, k_ref, v_ref, o_ref, m_scr, l_scr, acc_scr, *, seq):
    blk = DH_A
    scale = DH_A ** -0.5
    m_scr[...] = jnp.full(m_scr.shape, NEG_INF, F32)
    l_scr[...] = jnp.zeros(l_scr.shape, F32)
    acc_scr[...] = jnp.zeros(acc_scr.shape, F32)
    qi = lax.broadcasted_iota(jnp.int32, (blk, 2 * blk), 0)
    kc = lax.broadcasted_iota(jnp.int32, (blk, 2 * blk), 1)
    band = (kc >= qi) & (kc <= qi + blk)
    cur_half = kc >= blk

    for window, d in DILATED_BRANCHES:
        assert window // d == blk
        nb = seq // d // blk

        def block(idx, carry, d=d, nb=nb):
            r = idx // nb
            j = idx % nb
            start = r + j * (blk * d)
            prev = jnp.maximum(start - blk * d, r)
            rows = pl.ds(start, blk, stride=d)
            prows = pl.ds(prev, blk, stride=d)
            q = (q_ref[rows, :] * scale).astype(BF16)
            kk = jnp.concatenate([k_ref[prows, :], k_ref[rows, :]], axis=0).astype(BF16)
            vv = jnp.concatenate([v_ref[prows, :], v_ref[rows, :]], axis=0).astype(BF16)
            s = _dot_nt(q, kk)
            mask = band & (cur_half | (j > 0))
            s = jnp.where(mask, s, NEG_INF)
            m_old = m_scr[rows, :]
            l_old = l_scr[rows, :]
            m_new = jnp.maximum(m_old, jnp.max(s, axis=-1, keepdims=True))
            alpha = jnp.exp(m_old - m_new)
            p = jnp.exp(s - jnp.concatenate([m_new, m_new], axis=1))
            l_scr[rows, :] = alpha * l_old + jnp.sum(p, axis=-1, keepdims=True)
            acc_scr[rows, :] = alpha * acc_scr[rows, :] + _dot(p.astype(BF16), vv)
            m_scr[rows, :] = m_new
            return carry

        lax.fori_loop(0, d * nb, block, 0)

    o_ref[...] = (acc_scr[...] / l_scr[...]).astype(o_ref.dtype)


def _attn_prompt(qk, z, n_seq, seq, heads):
    return pl.pallas_call(
        functools.partial(_attn_prompt_body, seq=seq),
        grid=(n_seq, heads),
        in_specs=[pl.BlockSpec((seq, DH_A), lambda b, h: (b, h)),
                  pl.BlockSpec((seq, DH_A), lambda b, h: (b, heads + h)),
                  pl.BlockSpec((seq, DH_A), lambda b, h: (b, 2 * heads + h))],
        out_specs=pl.BlockSpec((seq, DH_A), lambda b, h: (b, h)),
        out_shape=jax.ShapeDtypeStruct((n_seq * seq, heads * DH_A), BF16),
        scratch_shapes=[pltpu.VMEM((seq, DH_A), F32)] * 3,
        compiler_params=_cparams("parallel", "arbitrary"),
        name="attn_prompt",
    )(qk, qk, z)


def _sample_counts(win_buf, dec_seq, heads, n_cols):
    n = np.arange(n_cols)[None, :]
    s = np.arange(dec_seq)[:, None]
    dist = win_buf + s - n
    cnt = np.zeros((dec_seq, n_cols), np.float32)
    for window, d in DILATED_BRANCHES:
        cnt += ((dist >= 0) & (dist <= window) & (dist % d == 0) & (n < win_buf + dec_seq))
    return np.tile(cnt, (heads, 1))


def _attn_sample_body(q_ref, kc_ref, vc_ref, kn_ref, vn_ref, cnt_ref, o_ref, qbd_scr, s_scr, p_scr, pn_scr, acc_scr,
                      *, heads, dec_seq, n_chunks, ck):
    c = pl.program_id(1)
    rows = heads * dec_seq
    width = heads * DH_A
    win = n_chunks * ck

    @pl.when(c == 0)
    def _():
        q = jnp.concatenate([q_ref[...]] * heads, axis=0)
        rh = lax.broadcasted_iota(jnp.int32, (rows, width), 0) // dec_seq
        ch = lax.broadcasted_iota(jnp.int32, (rows, width), 1) // DH_A
        qbd_scr[...] = jnp.where(rh == ch, q * (DH_A ** -0.5), 0.0).astype(BF16)

    @pl.when(c < n_chunks)
    def _():
        s_scr[c] = _dot_nt(qbd_scr[...], kc_ref[0].astype(BF16))

    @pl.when(c == n_chunks - 1)
    def _():
        kn = jnp.concatenate([kn_ref[...], jnp.zeros((LANES - dec_seq, width), F32)], axis=0).astype(BF16)
        cnts = [cnt_ref[:, i * ck:(i + 1) * ck] for i in range(n_chunks)] + [cnt_ref[:, win:]]
        parts = [s_scr[i] for i in range(n_chunks)] + [_dot_nt(qbd_scr[...], kn)]
        parts = [jnp.where(cn > 0, s, NEG_INF) for cn, s in zip(cnts, parts)]
        m = functools.reduce(jnp.maximum, [jnp.max(s, axis=-1, keepdims=True) for s in parts])
        ps = [cn * jnp.exp(s - m) for cn, s in zip(cnts, parts)]
        inv_l = 1.0 / functools.reduce(jnp.add, [jnp.sum(p, axis=-1, keepdims=True) for p in ps])
        for i in range(n_chunks):
            p_scr[i] = (ps[i] * inv_l).astype(BF16)
        pn_scr[...] = (ps[n_chunks] * inv_l).astype(BF16)
        acc_scr[...] = jnp.zeros(acc_scr.shape, F32)

    @pl.when(c >= n_chunks)
    def _():
        acc_scr[...] += _dot(p_scr[c - n_chunks], vc_ref[0].astype(BF16))

    @pl.when(c == 2 * n_chunks - 1)
    def _():
        vn = jnp.concatenate([vn_ref[...], jnp.zeros((LANES - dec_seq, width), F32)], axis=0).astype(BF16)
        acc = acc_scr[...] + _dot(pn_scr[...], vn)
        for h in range(heads):
            o_ref[:, h * DH_A:(h + 1) * DH_A] = acc[h * dec_seq:(h + 1) * dec_seq, h * DH_A:(h + 1) * DH_A]


def _attn_sample(qk, z, cache_k, cache_v, row0, n_seq, dec_seq, heads):
    win_buf = cache_k.shape[1]
    width = heads * DH_A
    ck = 512
    n_chunks = win_buf // ck
    n_cols = win_buf + LANES
    cnt = jnp.asarray(_sample_counts(win_buf, dec_seq, heads, n_cols))
    rb = row0 // dec_seq
    rows = heads * dec_seq
    return pl.pallas_call(
        functools.partial(_attn_sample_body, heads=heads, dec_seq=dec_seq, n_chunks=n_chunks, ck=ck),
        grid=(n_seq, 2 * n_chunks),
        in_specs=[pl.BlockSpec((dec_seq, width), lambda b, c: (rb + b, 0)),
                  pl.BlockSpec((1, ck, width), lambda b, c: (b, jnp.minimum(c, n_chunks - 1), 0)),
                  pl.BlockSpec((1, ck, width), lambda b, c: (b, jnp.maximum(c - n_chunks, 0), 0)),
                  pl.BlockSpec((dec_seq, width), lambda b, c: (rb + b, 1)),
                  pl.BlockSpec((dec_seq, width), lambda b, c: (rb + b, 2)),
                  pl.BlockSpec((rows, n_cols), lambda b, c: (0, 0))],
        out_specs=pl.BlockSpec((dec_seq, width), lambda b, c: (b, 0)),
        out_shape=jax.ShapeDtypeStruct((n_seq * dec_seq, width), F32),
        scratch_shapes=[pltpu.VMEM((rows, width), BF16), pltpu.VMEM((n_chunks, rows, ck), F32),
                        pltpu.VMEM((n_chunks, rows, ck), BF16), pltpu.VMEM((rows, LANES), BF16),
                        pltpu.VMEM((rows, width), F32)],
        compiler_params=_cparams("parallel", "arbitrary"),
        name="attn_sample",
    )(qk, cache_k, cache_v, qk, z, cnt)


def _gla_body(q_ref, k_ref, v_ref, r_ref, gl_ref, wg_ref, bg_ref, gn_ref, s0_ref, o_ref, sn_ref, st_scr,
              *, chunk, dk):
    c = pl.program_id(2)

    @pl.when(c == 0)
    def _():
        st_scr[...] = s0_ref[0, 0].T

    cp = max(chunk, LANES // 2)

    def pad(a):
        return a if cp == chunk else jnp.concatenate([a, jnp.zeros((cp - chunk, a.shape[1]), a.dtype)], axis=0)

    pre = _dot(gl_ref[...].astype(BF16), wg_ref[...]) + bg_ref[...]
    log_a = pad(jax.nn.log_sigmoid(pre) / GATE_TAU)
    ti = lax.broadcasted_iota(jnp.int32, (cp, cp), 0)
    tj = lax.broadcasted_iota(jnp.int32, (cp, cp), 1)
    tril = ti >= tj
    b = jnp.dot(tril.astype(F32), log_a, preferred_element_type=F32, precision=lax.Precision.HIGHEST)
    b_last = b[cp - 1:cp, :]
    q = pad(q_ref[...]) * (dk ** -0.5)
    k = pad(k_ref[...])
    v = pad(v_ref[...]).astype(BF16)
    q_in = (q * jnp.exp(b)).astype(BF16)
    k_in = (k * jnp.exp(-b)).astype(BF16)
    att = jnp.where(tril, _dot_nt(q_in, k_in), 0.0)
    st = st_scr[...]
    o = (_dot(att.astype(BF16), v) + _dot_nt(q_in, st.astype(BF16)))[:chunk]
    k_dec = (k * jnp.exp(b_last - b)).astype(BF16)
    st_new = st * jnp.exp(b_last) + _dot_tn(v, k_dec)
    st_scr[...] = st_new
    ms = jnp.mean(o * o, axis=-1, keepdims=True)
    o_ref[...] = (o * lax.rsqrt(ms + EPS) * gn_ref[...]) * jax.nn.silu(r_ref[...])

    @pl.when(c == pl.num_programs(2) - 1)
    def _():
        sn_ref[0, 0] = st_new.T


def _gla(z, zg, w_gate, b_gate, gla_norm, s0, *, row0, chunk, n_chunks, col_q, col_k, col_v, col_r):
    n_seq, heads, dk, dv = s0.shape
    rb = row0 // chunk

    def rows(b, h, c):
        return rb + b * n_chunks + c

    return pl.pallas_call(
        functools.partial(_gla_body, chunk=chunk, dk=dk),
        grid=(n_seq, heads, n_chunks),
        in_specs=[pl.BlockSpec((chunk, dk), lambda b, h, c: (rows(b, h, c), col_q // dk + h)),
                  pl.BlockSpec((chunk, dk), lambda b, h, c: (rows(b, h, c), col_k // dk + h)),
                  pl.BlockSpec((chunk, dv), lambda b, h, c: (rows(b, h, c), col_v // dv + h)),
                  pl.BlockSpec((chunk, dv), lambda b, h, c: (rows(b, h, c), col_r // dv + h)),
                  pl.BlockSpec((chunk, LANES), lambda b, h, c: (rows(b, h, c), 0)),
                  pl.BlockSpec((LANES, dk), lambda b, h, c: (0, h)),
                  pl.BlockSpec((1, dk), lambda b, h, c: (0, h)),
                  pl.BlockSpec((1, dv), lambda b, h, c: (0, 0)),
                  pl.BlockSpec((1, 1, dk, dv), lambda b, h, c: (b, h, 0, 0))],
        out_specs=[pl.BlockSpec((chunk, dv), lambda b, h, c: (b * n_chunks + c, h)),
                   pl.BlockSpec((1, 1, dk, dv), lambda b, h, c: (b, h, 0, 0))],
        out_shape=[jax.ShapeDtypeStruct((n_seq * n_chunks * chunk, heads * dv), F32),
                   jax.ShapeDtypeStruct((n_seq, heads, dk, dv), F32)],
        scratch_shapes=[pltpu.VMEM((dv, dk), F32)],
        compiler_params=_cparams("parallel", "parallel", "arbitrary"),
        name="gla",
    )(z, z, z, z, zg, w_gate, b_gate, gla_norm, s0)


def _mm_out_body(a1_ref, a2_ref, w1_ref, w2_ref, x_ref, o_ref):
    o_ref[...] = x_ref[...] + _dot(a1_ref[...], w1_ref[...]) + _dot(a2_ref[...], w2_ref[...])


def _mm_out(a1, a2, w1, w2, x, tm, tn):
    t, k1 = a1.shape
    k2 = a2.shape[1]
    n = w1.shape[1]
    return pl.pallas_call(
        _mm_out_body,
        grid=(t // tm, n // tn),
        in_specs=[pl.BlockSpec((tm, k1), lambda i, j: (i, 0)), pl.BlockSpec((tm, k2), lambda i, j: (i, 0)),
                  pl.BlockSpec((k1, tn), lambda i, j: (0, j)), pl.BlockSpec((k2, tn), lambda i, j: (0, j)),
                  pl.BlockSpec((tm, tn), lambda i, j: (i, j))],
        out_specs=pl.BlockSpec((tm, tn), lambda i, j: (i, j)),
        out_shape=jax.ShapeDtypeStruct((t, n), F32),
        compiler_params=_cparams("parallel", "arbitrary"),
        name="mm_out",
    )(a1, a2, w1, w2, x)


def _topk_rows(s, k, on_pick):
    n = s.shape[-1]
    lane = lax.broadcasted_iota(jnp.int32, s.shape, 1).astype(F32)
    for i in range(k):
        m = jnp.max(s, axis=-1, keepdims=True)
        idx = jnp.min(jnp.where(s == m, lane, float(n)), axis=-1, keepdims=True)
        on_pick(i, m, idx)
        s = jnp.where(lane == idx, -jnp.inf, s)


def _route_body(q_ref, k1_ref, k2_ref, ii_ref, jj_ref, g_ref):
    tt = q_ref.shape[0]
    half = PEER_DKEY // 2
    nc = PEER_TOPK * PEER_TOPK
    lane_c = lax.broadcasted_iota(jnp.int32, (tt, nc), 1)
    lane_cf = lane_c.astype(F32)
    lane_o = lax.broadcasted_iota(jnp.int32, (tt, PEER_HEADS * PEER_TOPK), 1)
    out_i = jnp.zeros((tt, PEER_HEADS * PEER_TOPK), F32)
    out_j = jnp.zeros((tt, PEER_HEADS * PEER_TOPK), F32)
    out_g = jnp.zeros((tt, PEER_HEADS * PEER_TOPK), F32)
    for h in range(PEER_HEADS):
        qh = q_ref[:, h * PEER_DKEY:(h + 1) * PEER_DKEY]
        qn = qh * lax.rsqrt(jnp.mean(qh * qh, axis=-1, keepdims=True) + EPS)
        s1 = _dot_nt(qn[:, :half].astype(BF16), k1_ref[h])
        s2 = _dot_nt(qn[:, half:].astype(BF16), k2_ref[h])
        st = dict(v1=jnp.zeros((tt, nc), F32), i1=jnp.zeros((tt, nc), F32),
                  v2=jnp.zeros((tt, nc), F32), i2=jnp.zeros((tt, nc), F32))

        def pick1(a, m, idx, st=st):
            sel = (lane_c // PEER_TOPK) == a
            st["v1"] = jnp.where(sel, m, st["v1"])
            st["i1"] = jnp.where(sel, idx, st["i1"])

        def pick2(b, m, idx, st=st):
            sel = (lane_c % PEER_TOPK) == b
            st["v2"] = jnp.where(sel, m, st["v2"])
            st["i2"] = jnp.where(sel, idx, st["i2"])

        _topk_rows(s1, PEER_TOPK, pick1)
        _topk_rows(s2, PEER_TOPK, pick2)
        cand = st["v1"] + st["v2"]
        acc = dict(i=out_i, j=out_j, e=out_g, top=None, z=None)

        def pick(kk, m, pos, acc=acc, st=st, h=h):
            sel = lane_cf == pos
            ei = jnp.sum(jnp.where(sel, st["i1"], 0.0), axis=-1, keepdims=True)
            ej = jnp.sum(jnp.where(sel, st["i2"], 0.0), axis=-1, keepdims=True)
            if kk == 0:
                acc["top"] = m
            e = jnp.exp(m - acc["top"])
            acc["z"] = e if kk == 0 else acc["z"] + e
            dst = lane_o == h * PEER_TOPK + kk
            acc["i"] = jnp.where(dst, ei, acc["i"])
            acc["j"] = jnp.where(dst, ej, acc["j"])
            acc["e"] = jnp.where(dst, e, acc["e"])

        _topk_rows(cand, PEER_TOPK, pick)
        seg = (lane_o // PEER_TOPK) == h
        out_i, out_j = acc["i"], acc["j"]
        out_g = jnp.where(seg, acc["e"] / acc["z"], acc["e"])
    ii_ref[...] = out_i.astype(jnp.int32)
    jj_ref[...] = out_j.astype(jnp.int32)
    g_ref[...] = out_g


def _route(q, keys1, keys2, tt):
    t, n = q.shape
    no = PEER_HEADS * PEER_TOPK
    kspec = pl.BlockSpec(keys1.shape, lambda i: (0, 0, 0))
    ospec = pl.BlockSpec((tt, no), lambda i: (i, 0))
    return pl.pallas_call(
        _route_body,
        grid=(t // tt,),
        in_specs=[pl.BlockSpec((tt, n), lambda i: (i, 0)), kspec, kspec],
        out_specs=[ospec, ospec, ospec],
        out_shape=[jax.ShapeDtypeStruct((t, no), jnp.int32), jax.ShapeDtypeStruct((t, no), jnp.int32),
                   jax.ShapeDtypeStruct((t, no), F32)],
        compiler_params=_cparams("parallel"),
        name="peer_route",
    )(q, keys1, keys2)


def _expand_body(ii_ref, jj_ref, g_ref, o_ref):
    te = ii_ref.shape[0]
    sub = lax.broadcasted_iota(jnp.int32, (N_KEYS, ii_ref.shape[1]), 0)

    def tok(t, carry):
        ii = ii_ref[pl.ds(t, 1), :]
        jj = jj_ref[pl.ds(t, 1), :]
        g = g_ref[pl.ds(t, 1), :]
        a = jnp.where(sub == ii, 1.0, 0.0).astype(BF16)
        b = jnp.where(sub == jj, g, 0.0).astype(BF16)
        o_ref[t] = _dot_nt(a, b).astype(o_ref.dtype)
        return carry

    lax.fori_loop(0, te, tok, 0)


def _expand(ii, jj, g, te):
    t, no = ii.shape
    spec = pl.BlockSpec((te, no), lambda i: (i, 0))
    return pl.pallas_call(
        _expand_body,
        grid=(t // te,),
        in_specs=[spec, spec, spec],
        out_specs=pl.BlockSpec((te, N_KEYS, N_KEYS), lambda i: (i, 0, 0)),
        out_shape=jax.ShapeDtypeStruct((t, N_KEYS, N_KEYS), BF16),
        compiler_params=_cparams("parallel"),
        name="peer_expand",
    )(ii, jj, g)


def _peer_a_body(h_ref, u_ref, g_ref, o_ref):
    a = _dot_nt(h_ref[...], u_ref[...])
    o_ref[...] = (g_ref[...].astype(F32) * jax.nn.gelu(a)).astype(o_ref.dtype)


def _peer_a(h, u, gates, tm, tn):
    t, k = h.shape
    e = u.shape[0]
    return pl.pallas_call(
        _peer_a_body,
        grid=(t // tm, e // tn),
        in_specs=[pl.BlockSpec((tm, k), lambda i, j: (i, 0)), pl.BlockSpec((tn, k), lambda i, j: (j, 0)),
                  pl.BlockSpec((tm, tn), lambda i, j: (i, j))],
        out_specs=pl.BlockSpec((tm, tn), lambda i, j: (i, j)),
        out_shape=jax.ShapeDtypeStruct((t, e), BF16),
        compiler_params=_cparams("parallel", "arbitrary"),
        name="peer_act",
    )(h, u, gates)


def _peer_v_body(c_ref, v_ref, x_ref, o_ref, acc_scr):
    kk = pl.program_id(2)

    @pl.when(kk == 0)
    def _():
        acc_scr[...] = x_ref[...]

    acc_scr[...] += _dot(c_ref[...], v_ref[...])

    @pl.when(kk == pl.num_programs(2) - 1)
    def _():
        o_ref[...] = acc_scr[...]


def _peer_v(c, v, x, tm, tn, tk):
    t, e = c.shape
    n = v.shape[1]
    return pl.pallas_call(
        _peer_v_body,
        grid=(t // tm, n // tn, e // tk),
        in_specs=[pl.BlockSpec((tm, tk), lambda i, j, k: (i, k)), pl.BlockSpec((tk, tn), lambda i, j, k: (k, j)),
                  pl.BlockSpec((tm, tn), lambda i, j, k: (i, j))],
        out_specs=pl.BlockSpec((tm, tn), lambda i, j, k: (i, j)),
        out_shape=jax.ShapeDtypeStruct((t, n), F32),
        scratch_shapes=[pltpu.VMEM((tm, tn), F32)],
        compiler_params=_cparams("parallel", "parallel", "arbitrary"),
        name="peer_mix",
    )(c, v, x)


def _token_tile(t, cap):
    best = 16
    for c in range(16, cap + 1, 16):
        if t % c == 0:
            best = c
    return best


def _rope_tables(pos):
    half = DH_A // 2
    inv = ROPE_THETA ** (-jnp.arange(half, dtype=F32) / half)
    ang = pos.astype(F32)[:, None] * inv[None, :]
    cos, sin = jnp.cos(ang), jnp.sin(ang)
    return jnp.concatenate([cos, cos], axis=1), jnp.concatenate([-sin, sin], axis=1)


def kernel(x_prompt, x_sample, cache_attn_k, cache_attn_v, state_gla, norm_mix, w_in, q_norm, k_norm, w_gate, b_gate,
           gla_norm, w_out, norm_ffn, w_peer_q, peer_keys1, peer_keys2, peer_u, peer_v):
    n_p, seq, d_model = x_prompt.shape
    n_s, dec_seq, _ = x_sample.shape
    depth = w_in.shape[0]
    heads_a = cache_attn_k.shape[3]
    w_a = heads_a * DH_A
    _, _, heads_b, dk, dv = state_gla.shape
    w_b = heads_b * dv
    assert seq == DILATED_BRANCHES[-1][0] and heads_b == H_B and cache_attn_k.shape[2] == seq
    t_p = n_p * seq
    t_all = t_p + n_s * dec_seq
    tm = _token_tile(t_all, 1056)
    tt = _token_tile(t_all, 264)

    x = jnp.concatenate([x_prompt.reshape(t_p, d_model), x_sample.reshape(-1, d_model)], axis=0)
    pos = jnp.concatenate([jnp.tile(jnp.arange(seq, dtype=jnp.int32), n_p),
                           jnp.tile(PAST_LEN + jnp.arange(dec_seq, dtype=jnp.int32), n_s)])
    cos2, sin2 = _rope_tables(pos)

    c_qb = 3 * w_a
    c_kb = c_qb + heads_b * dk
    c_vb = c_kb + heads_b * dk
    c_g = c_vb + w_b
    c_r = c_g + GATE_RANK

    outs = dict(kp=[], vp=[], ks=[], vs=[], sp=[], ss=[])
    for l in range(depth):
        w_main = jnp.concatenate([w_in[l][:, :c_g], w_in[l][:, c_r:]], axis=1).astype(BF16)
        w_glow = jnp.pad(w_in[l][:, c_g:c_r], ((0, 0), (0, LANES - GATE_RANK))).astype(BF16)
        w_gate_p = jnp.pad(w_gate[l], ((0, LANES - GATE_RANK), (0, 0))).astype(BF16)
        qk_gain = jnp.concatenate([jnp.tile(q_norm[l], heads_a), jnp.tile(k_norm[l], heads_a)]).reshape(1, 2 * w_a)

        h = _rmsnorm(x, norm_mix[l], tt)
        z = _matmul(h, w_main, tm, 1024)
        zg = _matmul(h, w_glow, tm, LANES)
        qk = _qk_post(z, qk_gain, cos2, sin2, tt, 4)

        o_a_p = _attn_prompt(qk, z, n_p, seq, heads_a)
        o_a_s = _attn_sample(qk, z, cache_attn_k[l].reshape(n_s, seq, w_a), cache_attn_v[l].reshape(n_s, seq, w_a),
                             t_p, n_s, dec_seq, heads_a)
        gla_args = (z, zg, w_gate_p, b_gate[l].reshape(1, -1), gla_norm[l].reshape(1, -1))
        gla_cols = dict(col_q=c_qb, col_k=c_kb, col_v=c_vb, col_r=c_g)
        o_b_p, s_p = _gla(*gla_args, jnp.zeros((n_p, heads_b, dk, dv), F32), row0=0, chunk=GLA_CHUNK,
                          n_chunks=seq // GLA_CHUNK, **gla_cols)
        o_b_s, s_s = _gla(*gla_args, state_gla[l], row0=t_p, chunk=dec_seq, n_chunks=1, **gla_cols)

        o_a = jnp.concatenate([o_a_p, o_a_s.astype(BF16)], axis=0)
        o_b = jnp.concatenate([o_b_p, o_b_s], axis=0).astype(BF16)
        w_o = w_out[l].astype(BF16)
        x1 = _mm_out(o_a, o_b, w_o[:w_a], w_o[w_a:], x, tm, 512)

        h2 = _rmsnorm(x1, norm_ffn[l], tt)
        q = _matmul(h2, w_peer_q[l].astype(BF16), tm, 1024)
        ii, jj, g = _route(q, peer_keys1[l].astype(BF16), peer_keys2[l].astype(BF16), tt)
        gates = _expand(ii, jj, g, _token_tile(t_all, 128)).reshape(t_all, N_KEYS * N_KEYS)
        c_act = _peer_a(h2, peer_u[l].astype(BF16), gates, tm, 512)
        x = _peer_v(c_act, peer_v[l].astype(BF16), x1, tm, 1024, 2048)

        outs["kp"].append(qk[:t_p, w_a:].reshape(n_p, seq, heads_a, DH_A))
        outs["vp"].append(z[:t_p, 2 * w_a:3 * w_a].reshape(n_p, seq, heads_a, DH_A))
        outs["ks"].append(qk[t_p:, w_a:].reshape(n_s, dec_seq, heads_a, DH_A))
        outs["vs"].append(z[t_p:, 2 * w_a:3 * w_a].reshape(n_s, dec_seq, heads_a, DH_A))
        outs["sp"].append(s_p)
        outs["ss"].append(s_s)

    y_prompt = x[:t_p].reshape(n_p, seq, d_model)
    y_sample = x[t_p:].reshape(n_s, dec_seq, d_model)
    return (y_prompt, y_sample, jnp.stack(outs["kp"]), jnp.stack(outs["vp"]), jnp.stack(outs["ks"]),
            jnp.stack(outs["vs"]), jnp.stack(outs["sp"]), jnp.stack(outs["ss"]))
```

```python
import functools

import numpy as np
import jax
import jax.numpy as jnp
from jax import lax
from jax.experimental import pallas as pl
from jax.experimental.pallas import tpu as pltpu

F32 = jnp.float32
BF16 = jnp.bfloat16

DH_A = 128
DILATED_BRANCHES = ((128, 1), (512, 4), (2048, 16))
ROPE_THETA = 10000.0
PAST_LEN = 8192
H_B = 4
GATE_RANK = 16
GATE_TAU = 16.0
GLA_CHUNK = 64
N_KEYS = 128
PEER_HEADS = 8
PEER_DKEY = 256
PEER_TOPK = 16
EPS = 1e-6
NEG_INF = -1e30

LANES = 128
VMEM_LIMIT = 56 * 1024 * 1024


def _cparams(*sem):
    return pltpu.CompilerParams(dimension_semantics=sem, vmem_limit_bytes=VMEM_LIMIT)


def _dot(a, b):
    return jnp.dot(a, b, preferred_element_type=F32)


def _dot_nt(a, b):
    return lax.dot_general(a, b, (((1,), (1,)), ((), ())), preferred_element_type=F32)


def _dot_tn(a, b):
    return lax.dot_general(a, b, (((0,), (0,)), ((), ())), preferred_element_type=F32)


def _rmsnorm_rows(x, gain):
    ms = jnp.mean(x * x, axis=-1, keepdims=True)
    return (x * lax.rsqrt(ms + EPS) * gain).astype(BF16)


def _rmsnorm_body(x_ref, g_ref, o_ref):
    o_ref[...] = _rmsnorm_rows(x_ref[...], g_ref[...])


def _rmsnorm(x, gain, tt):
    t, d = x.shape
    return pl.pallas_call(
        _rmsnorm_body,
        grid=(t // tt,),
        in_specs=[pl.BlockSpec((tt, d), lambda i: (i, 0)), pl.BlockSpec((1, d), lambda i: (0, 0))],
        out_specs=pl.BlockSpec((tt, d), lambda i: (i, 0)),
        out_shape=jax.ShapeDtypeStruct((t, d), BF16),
        compiler_params=_cparams("parallel"),
        name="rmsnorm",
    )(x, gain.reshape(1, d))


def _rmsnorm_pair_body(xp_ref, xs_ref, g_ref, o_ref, *, n_p):
    i = pl.program_id(0)

    @pl.when(i < n_p)
    def _():
        o_ref[...] = _rmsnorm_rows(xp_ref[...], g_ref[...])

    @pl.when(i >= n_p)
    def _():
        o_ref[...] = _rmsnorm_rows(xs_ref[...], g_ref[...])


def _rmsnorm_pair(xp, xs, gain, tt):
    t_p, d = xp.shape
    t_s = xs.shape[0]
    assert t_p % tt == 0 and t_s % tt == 0
    n_p = t_p // tt
    return pl.pallas_call(
        functools.partial(_rmsnorm_pair_body, n_p=n_p),
        grid=((t_p + t_s) // tt,),
        in_specs=[pl.BlockSpec((tt, d), lambda i: (jnp.minimum(i, n_p - 1), 0)),
                  pl.BlockSpec((tt, d), lambda i: (jnp.maximum(i - n_p, 0), 0)),
                  pl.BlockSpec((1, d), lambda i: (0, 0))],
        out_specs=pl.BlockSpec((tt, d), lambda i: (i, 0)),
        out_shape=jax.ShapeDtypeStruct((t_p + t_s, d), BF16),
        compiler_params=_cparams("arbitrary"),
        name="rmsnorm_in",
    )(xp, xs, gain.reshape(1, d))


def _mm_body(a_ref, w_ref, o_ref):
    o_ref[...] = _dot(a_ref[...], w_ref[...])


def _matmul(a, w, tm, tn):
    t, k = a.shape
    n = w.shape[1]
    return pl.pallas_call(
        _mm_body,
        grid=(t // tm, n // tn),
        in_specs=[pl.BlockSpec((tm, k), lambda i, j: (i, 0)), pl.BlockSpec((k, tn), lambda i, j: (0, j))],
        out_specs=pl.BlockSpec((tm, tn), lambda i, j: (i, j)),
        out_shape=jax.ShapeDtypeStruct((t, n), F32),
        compiler_params=_cparams("parallel", "arbitrary"),
        name="matmul",
    )(a, w)


def _qk_post_body(z_ref, g_ref, cos_ref, sin_ref, o_ref, *, heads):
    cos = cos_ref[...]
    sin = sin_ref[...]
    for h in range(heads):
        sl = slice(h * DH_A, (h + 1) * DH_A)
        x = z_ref[:, sl]
        ms = jnp.mean(x * x, axis=-1, keepdims=True)
        y = x * lax.rsqrt(ms + EPS) * g_ref[:, sl]
        o_ref[:, sl] = y * cos + pltpu.roll(y, DH_A // 2, axis=1) * sin


def _qk_post(z, gains, cos2, sin2, tt, heads):
    t = z.shape[0]
    n = gains.shape[1]
    bw = heads * DH_A
    return pl.pallas_call(
        functools.partial(_qk_post_body, heads=heads),
        grid=(t // tt, n // bw),
        in_specs=[pl.BlockSpec((tt, bw), lambda i, j: (i, j)),
                  pl.BlockSpec((1, bw), lambda i, j: (0, j)),
                  pl.BlockSpec((tt, DH_A), lambda i, j: (i, 0)),
                  pl.BlockSpec((tt, DH_A), lambda i, j: (i, 0))],
        out_specs=pl.BlockSpec((tt, bw), lambda i, j: (i, j)),
        out_shape=jax.ShapeDtypeStruct((t, n), F32),
        compiler_params=_cparams("parallel", "arbitrary"),
        name="qk_post",
    )(z, gains, cos2, sin2)


def _attn_prompt_body(q_ref, k_ref, v_ref, o_ref, *scr, seq):
    blk = DH_A
    scale = DH_A ** -0.5
    n_br = len(DILATED_BRANCHES)
    m_scr, l_scr, acc_scr = scr[:n_br], scr[n_br:2 * n_br], scr[2 * n_br:]
    qi = lax.broadcasted_iota(jnp.int32, (blk, 2 * blk), 0)
    kc = lax.broadcasted_iota(jnp.int32, (blk, 2 * blk), 1)
    band = (kc >= qi) & (kc <= qi + blk)
    cur_half = kc >= blk
    qi1 = lax.broadcasted_iota(jnp.int32, (blk, blk), 0)
    causal = lax.broadcasted_iota(jnp.int32, (blk, blk), 1) <= qi1
    floor_tile = 2.0 * NEG_INF - qi1.astype(F32)

    for bi, (window, d) in enumerate(DILATED_BRANCHES):
        assert window // d == blk
        nb = seq // d // blk

        def block(idx, carry, bi=bi, d=d, nb=nb):
            if nb == 1:
                start = idx
                rows = pl.ds(start, blk, stride=d)
                kk = k_ref[rows, :].astype(BF16)
                vv = v_ref[rows, :].astype(BF16)
                mask = causal
            else:
                r = idx // nb
                j = idx % nb
                start = r + j * (blk * d)
                rows = pl.ds(start, blk, stride=d)
                prows = pl.ds(jnp.maximum(start - blk * d, r), blk, stride=d)
                kk = jnp.concatenate([k_ref[prows, :], k_ref[rows, :]], axis=0).astype(BF16)
                vv = jnp.concatenate([v_ref[prows, :], v_ref[rows, :]], axis=0).astype(BF16)
                mask = band & (cur_half | (j > 0))
            q = (q_ref[rows, :] * scale).astype(BF16)
            s = jnp.where(mask, _dot_nt(q, kk), NEG_INF)
            m = jnp.max(s, axis=-1, keepdims=True)
            p = jnp.exp(s - m)
            m_scr[bi][rows, :] = jnp.maximum(m, floor_tile)
            l_scr[bi][rows, :] = jnp.maximum(jnp.sum(p, axis=-1, keepdims=True), floor_tile)
            acc_scr[bi][rows, :] = _dot(p.astype(BF16), vv)
            return carry

        lax.fori_loop(0, d * nb, block, 0, unroll=2)

    def merge(c, carry):
        rows = pl.ds(pl.multiple_of(c * blk, blk), blk)
        ms = [m[rows, :] for m in m_scr]
        m_max = functools.reduce(jnp.maximum, ms)
        ws = [jnp.exp(m - m_max) for m in ms]
        num = functools.reduce(jnp.add, [w * a[rows, :] for w, a in zip(ws, acc_scr)])
        den = functools.reduce(jnp.add, [w * l[rows, :] for w, l in zip(ws, l_scr)])
        o_ref[rows, :] = (num / den).astype(o_ref.dtype)
        return carry

    lax.fori_loop(0, seq // blk, merge, 0)


def _attn_prompt(qk, z, n_seq, seq, heads):
    return pl.pallas_call(
        functools.partial(_attn_prompt_body, seq=seq),
        grid=(n_seq, heads),
        in_specs=[pl.BlockSpec((seq, DH_A), lambda b, h: (b, h)),
                  pl.BlockSpec((seq, DH_A), lambda b, h: (b, heads + h)),
                  pl.BlockSpec((seq, DH_A), lambda b, h: (b, 2 * heads + h))],
        out_specs=pl.BlockSpec((seq, DH_A), lambda b, h: (b, h)),
        out_shape=jax.ShapeDtypeStruct((n_seq * seq, heads * DH_A), BF16),
        scratch_shapes=[pltpu.VMEM((seq, DH_A), F32)] * (3 * len(DILATED_BRANCHES)),
        compiler_params=_cparams("parallel", "arbitrary"),
        name="attn_prompt",
    )(qk, qk, z)


def _sample_weights(win_buf, dec_seq, heads, ck):
    n = np.arange(win_buf + dec_seq)[None, :]
    s = np.arange(dec_seq)[:, None]
    dist = win_buf + s - n
    cnt = np.zeros((dec_seq, win_buf + dec_seq), np.float32)
    for window, d in DILATED_BRANCHES:
        cnt += (dist >= 0) & (dist <= window) & (dist % d == 0)
    eye = np.eye(heads, dtype=np.float32)
    main = np.einsum("sn,hg->nhgs", cnt[:, :win_buf], eye).reshape(win_buf // ck, ck * heads, heads * dec_seq)
    new = np.einsum("sn,hg->hngs", cnt[:, win_buf:], eye).reshape(heads * dec_seq, heads * dec_seq)
    return main, new


def _attn_sample_body(q_ref, kc_ref, vc_ref, kn_ref, vn_ref, w_ref, wn_ref, o_ref, q_scr, m_scr, l_scr, acc_scr,
                      *, heads, dec_seq, n_chunks):
    c = pl.program_id(1)

    def by_head(ref):
        return jnp.concatenate([ref[:, h * DH_A:(h + 1) * DH_A] for h in range(heads)], axis=0)

    @pl.when(c == 0)
    def _():
        q_scr[...] = (by_head(q_ref) * (DH_A ** -0.5)).astype(BF16)
        m_scr[...] = jnp.full(m_scr.shape, NEG_INF, F32)
        l_scr[...] = jnp.zeros(l_scr.shape, F32)
        acc_scr[...] = jnp.zeros(acc_scr.shape, F32)

    def step(kb, vb, w):
        s = jnp.where(w > 0, _dot_nt(kb, q_scr[...]), NEG_INF)
        m_old = m_scr[...]
        m_new = jnp.maximum(m_old, jnp.max(s, axis=0, keepdims=True))
        alpha = jnp.exp(m_old - m_new)
        p = w * jnp.exp(s - m_new)
        l_scr[...] = alpha * l_scr[...] + jnp.sum(p, axis=0, keepdims=True)
        acc_scr[...] = alpha * acc_scr[...] + _dot_tn(vb, p.astype(BF16))
        m_scr[...] = m_new

    @pl.when(c < n_chunks)
    def _():
        step(kc_ref[0].astype(BF16), vc_ref[0].astype(BF16), w_ref[c].astype(F32))

    @pl.when(c == n_chunks)
    def _():
        step(by_head(kn_ref).astype(BF16), by_head(vn_ref).astype(BF16), wn_ref[...].astype(F32))
        out = (acc_scr[...] / l_scr[...]).T
        for h in range(heads):
            o_ref[:, h * DH_A:(h + 1) * DH_A] = out[h * dec_seq:(h + 1) * dec_seq, :]


def _attn_sample(qk, z, cache_k, cache_v, row0, n_seq, dec_seq, heads):
    assert heads * dec_seq == LANES
    win_buf = cache_k.shape[1] // heads
    width = heads * DH_A
    ck = 256
    n_chunks = win_buf // ck
    w_main, w_new = _sample_weights(win_buf, dec_seq, heads, ck)
    rb = row0 // dec_seq

    def new_spec(col):
        return pl.BlockSpec((dec_seq, width), lambda b, c: (rb + b, col))

    cache_spec = pl.BlockSpec((1, ck * heads, DH_A), lambda b, c: (b, jnp.minimum(c, n_chunks - 1), 0))
    return pl.pallas_call(
        functools.partial(_attn_sample_body, heads=heads, dec_seq=dec_seq, n_chunks=n_chunks),
        grid=(n_seq, n_chunks + 1),
        in_specs=[new_spec(0), cache_spec, cache_spec, new_spec(1), new_spec(2),
                  pl.BlockSpec(w_main.shape, lambda b, c: (0, 0, 0)),
                  pl.BlockSpec(w_new.shape, lambda b, c: (0, 0))],
        out_specs=pl.BlockSpec((dec_seq, width), lambda b, c: (b, 0)),
        out_shape=jax.ShapeDtypeStruct((n_seq * dec_seq, width), F32),
        scratch_shapes=[pltpu.VMEM((LANES, DH_A), BF16), pltpu.VMEM((1, LANES), F32), pltpu.VMEM((1, LANES), F32),
                        pltpu.VMEM((DH_A, LANES), F32)],
        compiler_params=_cparams("parallel", "arbitrary"),
        name="attn_sample",
    )(qk, cache_k, cache_v, qk, z, jnp.asarray(w_main, BF16), jnp.asarray(w_new, BF16))


def _gla_body(q_ref, k_ref, v_ref, r_ref, gl_ref, wg_ref, bg_ref, gn_ref, s0_ref, o_ref, sn_ref, st_scr,
              *, chunk, dk):
    c = pl.program_id(2)

    @pl.when(c == 0)
    def _():
        st_scr[...] = s0_ref[0, 0].T

    cp = max(chunk, LANES // 2)

    def pad(a):
        return a if cp == chunk else jnp.concatenate([a, jnp.zeros((cp - chunk, a.shape[1]), a.dtype)], axis=0)

    pre = _dot(gl_ref[...].astype(BF16), wg_ref[...]) + bg_ref[...]
    log_a = pad(jax.nn.log_sigmoid(pre) / GATE_TAU)
    ti = lax.broadcasted_iota(jnp.int32, (cp, cp), 0)
    tj = lax.broadcasted_iota(jnp.int32, (cp, cp), 1)
    tril = ti >= tj
    b = jnp.dot(tril.astype(F32), log_a, preferred_element_type=F32, precision=lax.Precision.HIGHEST)
    b_last = b[cp - 1:cp, :]
    q = pad(q_ref[...]) * (dk ** -0.5)
    k = pad(k_ref[...])
    v = pad(v_ref[...]).astype(BF16)
    q_in = (q * jnp.exp(b)).astype(BF16)
    k_in = (k * jnp.exp(-b)).astype(BF16)
    att = jnp.where(tril, _dot_nt(q_in, k_in), 0.0)
    st = st_scr[...]
    o = (_dot(att.astype(BF16), v) + _dot_nt(q_in, st.astype(BF16)))[:chunk]
    k_dec = (k * jnp.exp(b_last - b)).astype(BF16)
    st_new = st * jnp.exp(b_last) + _dot_tn(v, k_dec)
    st_scr[...] = st_new
    ms = jnp.mean(o * o, axis=-1, keepdims=True)
    o_ref[...] = ((o * lax.rsqrt(ms + EPS) * gn_ref[...]) * jax.nn.silu(r_ref[...])).astype(o_ref.dtype)

    @pl.when(c == pl.num_programs(2) - 1)
    def _():
        sn_ref[0, 0] = st_new.T


def _gla(z, zr, zg, w_gate, b_gate, gla_norm, s0, *, row0, chunk, n_chunks, col_q, col_k, col_v, out_dtype):
    n_seq, heads, dk, dv = s0.shape
    rb = row0 // chunk

    def rows(b, h, c):
        return rb + b * n_chunks + c

    return pl.pallas_call(
        functools.partial(_gla_body, chunk=chunk, dk=dk),
        grid=(n_seq, heads, n_chunks),
        in_specs=[pl.BlockSpec((chunk, dk), lambda b, h, c: (rows(b, h, c), col_q // dk + h)),
                  pl.BlockSpec((chunk, dk), lambda b, h, c: (rows(b, h, c), col_k // dk + h)),
                  pl.BlockSpec((chunk, dv), lambda b, h, c: (rows(b, h, c), col_v // dv + h)),
                  pl.BlockSpec((chunk, dv), lambda b, h, c: (rows(b, h, c), h)),
                  pl.BlockSpec((chunk, LANES), lambda b, h, c: (rows(b, h, c), 0)),
                  pl.BlockSpec((LANES, dk), lambda b, h, c: (0, h)),
                  pl.BlockSpec((1, dk), lambda b, h, c: (0, h)),
                  pl.BlockSpec((1, dv), lambda b, h, c: (0, 0)),
                  pl.BlockSpec((1, 1, dk, dv), lambda b, h, c: (b, h, 0, 0))],
        out_specs=[pl.BlockSpec((chunk, dv), lambda b, h, c: (b * n_chunks + c, h)),
                   pl.BlockSpec((1, 1, dk, dv), lambda b, h, c: (b, h, 0, 0))],
        out_shape=[jax.ShapeDtypeStruct((n_seq * n_chunks * chunk, heads * dv), out_dtype),
                   jax.ShapeDtypeStruct((n_seq, heads, dk, dv), F32)],
        scratch_shapes=[pltpu.VMEM((dv, dk), F32)],
        compiler_params=_cparams("parallel", "parallel", "arbitrary"),
        name="gla",
    )(z, z, z, zr, zg, w_gate, b_gate, gla_norm, s0)


def _mm_out_body(a1p_ref, a1s_ref, a2p_ref, a2s_ref, w1_ref, w2_ref, xp_ref, xs_ref, o_ref, *, n_full, rem):
    i = pl.program_id(0)

    def go(a1, a2, x):
        o_ref[...] = x + _dot(a1, w1_ref[...]) + _dot(a2, w2_ref[...])

    @pl.when(i < n_full)
    def _():
        go(a1p_ref[...], a2p_ref[...], xp_ref[...])

    @pl.when(i == n_full)
    def _():
        def cat(p_ref, s_ref):
            return jnp.concatenate([p_ref[:rem], s_ref[...].astype(p_ref.dtype)], axis=0)

        go(cat(a1p_ref, a1s_ref), cat(a2p_ref, a2s_ref), cat(xp_ref, xs_ref))


def _mm_out(a1p, a1s, a2p, a2s, w1, w2, xp, xs, tm, tn):
    t_p, k1 = a1p.shape
    t_s = a1s.shape[0]
    k2 = a2p.shape[1]
    n = w1.shape[1]
    n_full, rem = divmod(t_p, tm)
    assert rem > 0 and rem + t_s == tm

    def pspec(k):
        return pl.BlockSpec((tm, k), lambda i, j: (i, 0))

    def sspec(k):
        return pl.BlockSpec((t_s, k), lambda i, j: (0, 0))

    return pl.pallas_call(
        functools.partial(_mm_out_body, n_full=n_full, rem=rem),
        grid=(n_full + 1, n // tn),
        in_specs=[pspec(k1), sspec(k1), pspec(k2), sspec(k2),
                  pl.BlockSpec((k1, tn), lambda i, j: (0, j)), pl.BlockSpec((k2, tn), lambda i, j: (0, j)),
                  pl.BlockSpec((tm, tn), lambda i, j: (i, j)), pl.BlockSpec((t_s, tn), lambda i, j: (0, j))],
        out_specs=pl.BlockSpec((tm, tn), lambda i, j: (i, j)),
        out_shape=jax.ShapeDtypeStruct((t_p + t_s, n), F32),
        compiler_params=_cparams("parallel", "arbitrary"),
        name="mm_out",
    )(a1p, a1s, a2p, a2s, w1, w2, xp, xs)


def _rows_iota(shape):
    return lax.broadcasted_iota(jnp.int32, shape, 0).astype(F32)


def _topk_cols(s, ranks, k):
    big = float(2 ** 20)
    for _ in range(k):
        m = jnp.max(s, axis=0, keepdims=True)
        r = jnp.min(jnp.where(s == m, ranks, big), axis=0, keepdims=True)
        sel = ranks == r
        yield m, r, sel
        s = jnp.where(sel, -jnp.inf, s)


def _stack_rows(rows):
    n = len(rows)
    rid = lax.broadcasted_iota(jnp.int32, (n, rows[0].shape[1]), 0)
    out = jnp.broadcast_to(rows[0], (n, rows[0].shape[1]))
    for i in range(1, n):
        out = jnp.where(rid == i, rows[i], out)
    return out


def _route_body(q_ref, k1_ref, k2_ref, ii_ref, jj_ref, g_ref):
    tt = q_ref.shape[0]
    half = PEER_DKEY // 2
    topk = PEER_TOPK
    key_rank = _rows_iota((N_KEYS, tt))
    r8 = _rows_iota((8, tt))
    r16 = _rows_iota((topk, tt))
    gates, experts = [], []
    for h in range(PEER_HEADS):
        qh = q_ref[:, h * PEER_DKEY:(h + 1) * PEER_DKEY]
        qn = (qh * lax.rsqrt(jnp.mean(qh * qh, axis=-1, keepdims=True) + EPS)).astype(BF16)
        s1 = _dot_nt(k1_ref[h], qn[:, :half])
        s2 = _dot_nt(k2_ref[h], qn[:, half:])
        v1, i1 = zip(*[(m, r) for m, r, _ in _topk_cols(s1, key_rank, topk)])
        v2, i2 = zip(*[(m, r) for m, r, _ in _topk_cols(s2, key_rank, topk)])
        v1t, i1t, v2t, i2t = (_stack_rows(list(x)) for x in (v1, i1, v2, i2))
        cand = [v1[0] + v2t]
        code = [i1[0] * N_KEYS + i2t]
        flat = [r16]
        for a in range(1, 8):
            ok = r8 < float(topk // (a + 1))
            cand.append(jnp.where(ok, v1[a] + v2t[:8], -jnp.inf))
            code.append(i1[a] * N_KEYS + i2t[:8])
            flat.append(r8 + float(a * topk))
        cand.append(v1t[8:] + v2[0])
        code.append(i1t[8:] * N_KEYS + i2[0])
        flat.append((r8 + 8.0) * float(topk))
        cand, code, flat = (jnp.concatenate(x, axis=0) for x in (cand, code, flat))
        top, sel_codes = [], []
        for m, _, sel in _topk_cols(cand, flat, topk):
            top.append(m)
            sel_codes.append(jnp.max(jnp.where(sel, code, -1.0), axis=0, keepdims=True))
        ex = [jnp.exp(m - top[0]) for m in top]
        inv_z = 1.0 / functools.reduce(jnp.add, ex)
        gates.append(_stack_rows([e * inv_z for e in ex]))
        experts.append(_stack_rows(sel_codes))
    g_ref[...] = jnp.concatenate(gates, axis=0).T
    e = jnp.concatenate(experts, axis=0).T.astype(jnp.int32)
    ii_ref[...] = lax.shift_right_logical(e, N_KEYS.bit_length() - 1)
    jj_ref[...] = lax.bitwise_and(e, N_KEYS - 1)


def _route(q, keys1, keys2, tt):
    t, n = q.shape
    no = PEER_HEADS * PEER_TOPK
    kspec = pl.BlockSpec(keys1.shape, lambda i: (0, 0, 0))
    ospec = pl.BlockSpec((tt, no), lambda i: (i, 0))
    return pl.pallas_call(
        _route_body,
        grid=(t // tt,),
        in_specs=[pl.BlockSpec((tt, n), lambda i: (i, 0)), kspec, kspec],
        out_specs=[ospec, ospec, ospec],
        out_shape=[jax.ShapeDtypeStruct((t, no), jnp.int32), jax.ShapeDtypeStruct((t, no), jnp.int32),
                   jax.ShapeDtypeStruct((t, no), F32)],
        compiler_params=_cparams("parallel"),
        name="peer_route",
    )(q, keys1, keys2)


EXPAND_GROUP = 16
EXPAND_PITCH = N_KEYS + 8


def _expand_body(ii_ref, jj_ref, g_ref, o_ref, scr):
    te = ii_ref.shape[0]
    sub = lax.broadcasted_iota(jnp.int32, (N_KEYS, ii_ref.shape[1]), 0)

    def group(gi, carry):
        base = pl.multiple_of(gi * EXPAND_GROUP, EXPAND_GROUP)
        for u in range(EXPAND_GROUP):
            ii = ii_ref[pl.ds(base + u, 1), :]
            jj = jj_ref[pl.ds(base + u, 1), :]
            g = g_ref[pl.ds(base + u, 1), :]
            a = jnp.where(sub == ii, 1.0, 0.0).astype(BF16)
            b = jnp.where(sub == jj, g, 0.0).astype(BF16)
            scr[u * EXPAND_PITCH:u * EXPAND_PITCH + N_KEYS, :] = _dot_nt(a, b)
        for i in range(N_KEYS):
            rows = scr[pl.ds(i, EXPAND_GROUP, stride=EXPAND_PITCH), :]
            o_ref[pl.ds(base, EXPAND_GROUP), i * N_KEYS:(i + 1) * N_KEYS] = rows.astype(o_ref.dtype)
        return carry

    lax.fori_loop(0, te // EXPAND_GROUP, group, 0)


def _expand(ii, jj, g, te):
    t, no = ii.shape
    spec = pl.BlockSpec((te, no), lambda i: (i, 0))
    return pl.pallas_call(
        _expand_body,
        grid=(t // te,),
        in_specs=[spec, spec, spec],
        out_specs=pl.BlockSpec((te, N_KEYS * N_KEYS), lambda i: (i, 0)),
        out_shape=jax.ShapeDtypeStruct((t, N_KEYS * N_KEYS), BF16),
        scratch_shapes=[pltpu.VMEM((EXPAND_GROUP * EXPAND_PITCH, N_KEYS), F32)],
        compiler_params=_cparams("parallel"),
        name="peer_expand",
    )(ii, jj, g)


def _peer_a_body(h_ref, u_ref, g_ref, o_ref):
    a = _dot_nt(h_ref[...], u_ref[...])
    o_ref[...] = (g_ref[...].astype(F32) * jax.nn.gelu(a)).astype(o_ref.dtype)


def _peer_a(h, u, gates, tm, tn):
    t, k = h.shape
    e = u.shape[0]
    return pl.pallas_call(
        _peer_a_body,
        grid=(t // tm, e // tn),
        in_specs=[pl.BlockSpec((tm, k), lambda i, j: (i, 0)), pl.BlockSpec((tn, k), lambda i, j: (j, 0)),
                  pl.BlockSpec((tm, tn), lambda i, j: (i, j))],
        out_specs=pl.BlockSpec((tm, tn), lambda i, j: (i, j)),
        out_shape=jax.ShapeDtypeStruct((t, e), BF16),
        compiler_params=_cparams("parallel", "arbitrary"),
        name="peer_act",
    )(h, u, gates)


def _peer_v_body(c_ref, v_ref, x_ref, yp_ref, ys_ref, acc_scr, *, n_full, rem):
    i = pl.program_id(0)
    kk = pl.program_id(2)
    last = pl.num_programs(2) - 1

    @pl.when(kk == 0)
    def _():
        acc_scr[...] = x_ref[...]

    acc_scr[...] += _dot(c_ref[...], v_ref[...])

    @pl.when(kk == last)
    def _():
        yp_ref[...] = acc_scr[...]

    @pl.when((kk == last) & (i == n_full))
    def _():
        ys_ref[...] = acc_scr[rem:, :]


def _peer_v(c, v, x, t_p, tm, tn, tk):
    t, e = c.shape
    n = v.shape[1]
    t_s = t - t_p
    n_full, rem = divmod(t_p, tm)
    assert rem > 0 and rem + t_s == tm
    return pl.pallas_call(
        functools.partial(_peer_v_body, n_full=n_full, rem=rem),
        grid=(t // tm, n // tn, e // tk),
        in_specs=[pl.BlockSpec((tm, tk), lambda i, j, k: (i, k)), pl.BlockSpec((tk, tn), lambda i, j, k: (k, j)),
                  pl.BlockSpec((tm, tn), lambda i, j, k: (i, j))],
        out_specs=[pl.BlockSpec((tm, tn), lambda i, j, k: (i, j)),
                   pl.BlockSpec((t_s, tn), lambda i, j, k: (0, jnp.where(i == n_full, j, 0)))],
        out_shape=[jax.ShapeDtypeStruct((t_p, n), F32), jax.ShapeDtypeStruct((t_s, n), F32)],
        scratch_shapes=[pltpu.VMEM((tm, tn), F32)],
        compiler_params=_cparams("arbitrary", "arbitrary", "arbitrary"),
        name="peer_mix",
    )(c, v, x)


def _token_tile(t, cap):
    best = 16
    for c in range(16, cap + 1, 16):
        if t % c == 0:
            best = c
    return best


def _rope_tables(pos):
    half = DH_A // 2
    inv = ROPE_THETA ** (-jnp.arange(half, dtype=F32) / half)
    ang = pos.astype(F32)[:, None] * inv[None, :]
    cos, sin = jnp.cos(ang), jnp.sin(ang)
    return jnp.concatenate([cos, cos], axis=1), jnp.concatenate([-sin, sin], axis=1)


def kernel(x_prompt, x_sample, cache_attn_k, cache_attn_v, state_gla, norm_mix, w_in, q_norm, k_norm, w_gate, b_gate,
           gla_norm, w_out, norm_ffn, w_peer_q, peer_keys1, peer_keys2, peer_u, peer_v):
    n_p, seq, d_model = x_prompt.shape
    n_s, dec_seq, _ = x_sample.shape
    depth = w_in.shape[0]
    heads_a = cache_attn_k.shape[3]
    w_a = heads_a * DH_A
    _, _, heads_b, dk, dv = state_gla.shape
    w_b = heads_b * dv
    assert seq == DILATED_BRANCHES[-1][0] and heads_b == H_B and cache_attn_k.shape[2] == seq
    t_p = n_p * seq
    t_s = n_s * dec_seq
    t_all = t_p + t_s
    tm = _token_tile(t_all, 1056)
    tt = _token_tile(t_s, 256)

    xp = x_prompt.reshape(t_p, d_model)
    xs = x_sample.reshape(t_s, d_model)
    pos = jnp.concatenate([jnp.tile(jnp.arange(seq, dtype=jnp.int32), n_p),
                           jnp.tile(PAST_LEN + jnp.arange(dec_seq, dtype=jnp.int32), n_s)])
    cos2, sin2 = _rope_tables(pos)

    c_qb = 3 * w_a
    c_kb = c_qb + heads_b * dk
    c_vb = c_kb + heads_b * dk
    c_g = c_vb + w_b
    c_r = c_g + GATE_RANK

    outs = dict(kp=[], vp=[], ks=[], vs=[], sp=[], ss=[])
    for l in range(depth):
        w_main = w_in[l][:, :c_g].astype(BF16)
        w_r = w_in[l][:, c_r:].astype(BF16)
        w_glow = jnp.pad(w_in[l][:, c_g:c_r].astype(BF16), ((0, 0), (0, LANES - GATE_RANK)))
        w_gate_p = jnp.pad(w_gate[l].astype(BF16), ((0, LANES - GATE_RANK), (0, 0)))
        qk_gain = jnp.concatenate([jnp.tile(q_norm[l], heads_a), jnp.tile(k_norm[l], heads_a)]).reshape(1, 2 * w_a)

        if l == 0:
            h = _rmsnorm_pair(xp, xs, norm_mix[l], tt)
        else:
            h = _rmsnorm(x, norm_mix[l], tt)
            xp, xs = x[:t_p], x[t_p:]
        z = _matmul(h, w_main, tm, 1024)
        zr = _matmul(h, w_r, tm, 1024)
        zg = _matmul(h, w_glow, tm, LANES)
        qk = _qk_post(z, qk_gain, cos2, sin2, tt, 4)

        o_a_p = _attn_prompt(qk, z, n_p, seq, heads_a)
        o_a_s = _attn_sample(qk, z, cache_attn_k[l].reshape(n_s, seq * heads_a, DH_A),
                             cache_attn_v[l].reshape(n_s, seq * heads_a, DH_A), t_p, n_s, dec_seq, heads_a)
        gla_args = (z, zr, zg, w_gate_p, b_gate[l].reshape(1, -1), gla_norm[l].reshape(1, -1))
        gla_cols = dict(col_q=c_qb, col_k=c_kb, col_v=c_vb)
        o_b_p, s_p = _gla(*gla_args, jnp.zeros((n_p, heads_b, dk, dv), F32), row0=0, chunk=GLA_CHUNK,
                          n_chunks=seq // GLA_CHUNK, out_dtype=BF16, **gla_cols)
        o_b_s, s_s = _gla(*gla_args, state_gla[l], row0=t_p, chunk=dec_seq, n_chunks=1, out_dtype=F32, **gla_cols)

        w_o1 = w_out[l][:w_a].astype(BF16)
        w_o2 = w_out[l][w_a:].astype(BF16)
        x1 = _mm_out(o_a_p, o_a_s, o_b_p, o_b_s, w_o1, w_o2, xp, xs, tm, 512)

        h2 = _rmsnorm(x1, norm_ffn[l], tt)
        q = _matmul(h2, w_peer_q[l].astype(BF16), tm, 1024)
        ii, jj, g = _route(q, peer_keys1[l].astype(BF16), peer_keys2[l].astype(BF16), LANES)
        gates = _expand(ii, jj, g, _token_tile(t_all, 128))
        c_act = _peer_a(h2, peer_u[l].astype(BF16), gates, tm, 512)
        yp, ys = _peer_v(c_act, peer_v[l].astype(BF16), x1, t_p, tm, 1024, 2048)
        if l + 1 < depth:
            x = jnp.concatenate([yp, ys], axis=0)

        outs["kp"].append(qk[:t_p, w_a:].reshape(n_p, seq, heads_a, DH_A))
        outs["vp"].append(z[:t_p, 2 * w_a:3 * w_a].reshape(n_p, seq, heads_a, DH_A))
        outs["ks"].append(qk[t_p:, w_a:].reshape(n_s, dec_seq, heads_a, DH_A))
        outs["vs"].append(z[t_p:, 2 * w_a:3 * w_a].reshape(n_s, dec_seq, heads_a, DH_A))
        outs["sp"].append(s_p)
        outs["ss"].append(s_s)

    y_prompt = yp.reshape(n_p, seq, d_model)
    y_sample = ys.reshape(n_s, dec_seq, d_model)
    return (y_prompt, y_sample, jnp.stack(outs["kp"]), jnp.stack(outs["vp"]), jnp.stack(outs["ks"]),
            jnp.stack(outs["vs"]), jnp.stack(outs["sp"]), jnp.stack(outs["ss"]))
```

```python
import functools

import numpy as np
import jax
import jax.numpy as jnp
from jax import lax
from jax.experimental import pallas as pl
from jax.experimental.pallas import tpu as pltpu

F32 = jnp.float32
BF16 = jnp.bfloat16

DH_A = 128
DILATED_BRANCHES = ((128, 1), (512, 4), (2048, 16))
ROPE_THETA = 10000.0
PAST_LEN = 8192
H_B = 4
GATE_RANK = 16
GATE_TAU = 16.0
GLA_CHUNK = 64
N_KEYS = 128
PEER_HEADS = 8
PEER_DKEY = 256
PEER_TOPK = 16
EPS = 1e-6
NEG_INF = -1e30

LANES = 128
VMEM_LIMIT = 56 * 1024 * 1024


def _cparams(*sem):
    return pltpu.CompilerParams(dimension_semantics=sem, vmem_limit_bytes=VMEM_LIMIT)


def _dot(a, b):
    return jnp.dot(a, b, preferred_element_type=F32)


def _dot_nt(a, b):
    return lax.dot_general(a, b, (((1,), (1,)), ((), ())), preferred_element_type=F32)


def _dot_tn(a, b):
    return lax.dot_general(a, b, (((0,), (0,)), ((), ())), preferred_element_type=F32)


def _rmsnorm_rows(x, gain):
    ms = jnp.mean(x * x, axis=-1, keepdims=True)
    return (x * lax.rsqrt(ms + EPS) * gain).astype(BF16)


def _rmsnorm_body(x_ref, g_ref, o_ref):
    o_ref[...] = _rmsnorm_rows(x_ref[...], g_ref[...])


def _rmsnorm(x, gain, tt):
    t, d = x.shape
    return pl.pallas_call(
        _rmsnorm_body,
        grid=(t // tt,),
        in_specs=[pl.BlockSpec((tt, d), lambda i: (i, 0)), pl.BlockSpec((1, d), lambda i: (0, 0))],
        out_specs=pl.BlockSpec((tt, d), lambda i: (i, 0)),
        out_shape=jax.ShapeDtypeStruct((t, d), BF16),
        compiler_params=_cparams("parallel"),
        name="rmsnorm",
    )(x, gain.reshape(1, d))


def _rmsnorm_pair_body(xp_ref, xs_ref, g_ref, o_ref, *, n_p):
    i = pl.program_id(0)

    @pl.when(i < n_p)
    def _():
        o_ref[...] = _rmsnorm_rows(xp_ref[...], g_ref[...])

    @pl.when(i >= n_p)
    def _():
        o_ref[...] = _rmsnorm_rows(xs_ref[...], g_ref[...])


def _rmsnorm_pair(xp, xs, gain, tt):
    t_p, d = xp.shape
    t_s = xs.shape[0]
    assert t_p % tt == 0 and t_s % tt == 0
    n_p = t_p // tt
    return pl.pallas_call(
        functools.partial(_rmsnorm_pair_body, n_p=n_p),
        grid=((t_p + t_s) // tt,),
        in_specs=[pl.BlockSpec((tt, d), lambda i: (jnp.minimum(i, n_p - 1), 0)),
                  pl.BlockSpec((tt, d), lambda i: (jnp.maximum(i - n_p, 0), 0)),
                  pl.BlockSpec((1, d), lambda i: (0, 0))],
        out_specs=pl.BlockSpec((tt, d), lambda i: (i, 0)),
        out_shape=jax.ShapeDtypeStruct((t_p + t_s, d), BF16),
        compiler_params=_cparams("arbitrary"),
        name="rmsnorm_in",
    )(xp, xs, gain.reshape(1, d))


def _mm_body(a_ref, w_ref, o_ref):
    o_ref[...] = _dot(a_ref[...], w_ref[...].astype(BF16))


def _matmul(a, w, tm, tn, n_cols=None):
    t, k = a.shape
    n = w.shape[1] if n_cols is None else n_cols
    return pl.pallas_call(
        _mm_body,
        grid=(t // tm, n // tn),
        in_specs=[pl.BlockSpec((tm, k), lambda i, j: (i, 0)), pl.BlockSpec((k, tn), lambda i, j: (0, j))],
        out_specs=pl.BlockSpec((tm, tn), lambda i, j: (i, j)),
        out_shape=jax.ShapeDtypeStruct((t, n), F32),
        compiler_params=_cparams("parallel", "arbitrary"),
        name="matmul",
    )(a, w)


def _qk_post_body(z_ref, g_ref, cos_ref, sin_ref, o_ref, *, heads):
    cos = cos_ref[...]
    sin = sin_ref[...]
    for h in range(heads):
        sl = slice(h * DH_A, (h + 1) * DH_A)
        x = z_ref[:, sl]
        ms = jnp.mean(x * x, axis=-1, keepdims=True)
        y = x * lax.rsqrt(ms + EPS) * g_ref[:, sl]
        o_ref[:, sl] = y * cos + pltpu.roll(y, DH_A // 2, axis=1) * sin


def _qk_post(z, gains, cos2, sin2, tt, heads):
    t = z.shape[0]
    n = gains.shape[1]
    bw = heads * DH_A
    return pl.pallas_call(
        functools.partial(_qk_post_body, heads=heads),
        grid=(t // tt, n // bw),
        in_specs=[pl.BlockSpec((tt, bw), lambda i, j: (i, j)),
                  pl.BlockSpec((1, bw), lambda i, j: (0, j)),
                  pl.BlockSpec((tt, DH_A), lambda i, j: (i, 0)),
                  pl.BlockSpec((tt, DH_A), lambda i, j: (i, 0))],
        out_specs=pl.BlockSpec((tt, bw), lambda i, j: (i, j)),
        out_shape=jax.ShapeDtypeStruct((t, n), F32),
        compiler_params=_cparams("parallel", "arbitrary"),
        name="qk_post",
    )(z, gains, cos2, sin2)


ATTN_GROUP = 4


def _attn_prompt_body(q_ref, k_ref, v_ref, o_ref, *scr, seq):
    blk = DH_A
    scale = DH_A ** -0.5
    n_br = len(DILATED_BRANCHES)
    m_scr, l_scr, acc_scr = scr[:n_br], scr[n_br:2 * n_br], scr[2 * n_br:]
    qi = lax.broadcasted_iota(jnp.int32, (blk, 2 * blk), 0)
    kc = lax.broadcasted_iota(jnp.int32, (blk, 2 * blk), 1)
    band = (kc >= qi) & (kc <= qi + blk)
    cur_half = kc >= blk
    qi1 = lax.broadcasted_iota(jnp.int32, (blk, blk), 0)
    causal = lax.broadcasted_iota(jnp.int32, (blk, blk), 1) <= qi1
    floor_tile = 2.0 * NEG_INF - qi1.astype(F32)

    for bi, (window, d) in enumerate(DILATED_BRANCHES):
        assert window // d == blk
        nb = seq // d // blk

        def blocks(it, carry, bi=bi, d=d, nb=nb):
            rows, vvs, ss = [], [], []
            for u in range(ATTN_GROUP):
                idx = it * ATTN_GROUP + u
                if nb == 1:
                    rw = pl.ds(idx, blk, stride=d)
                    kk = k_ref[rw, :].astype(BF16)
                    vv = v_ref[rw, :].astype(BF16)
                    mask = causal
                else:
                    r = idx // nb
                    j = idx % nb
                    start = r + j * (blk * d)
                    rw = pl.ds(start, blk, stride=d)
                    prows = pl.ds(jnp.maximum(start - blk * d, r), blk, stride=d)
                    kk = jnp.concatenate([k_ref[prows, :], k_ref[rw, :]], axis=0).astype(BF16)
                    vv = jnp.concatenate([v_ref[prows, :], v_ref[rw, :]], axis=0).astype(BF16)
                    mask = band & (cur_half | (j > 0))
                q = (q_ref[rw, :] * scale).astype(BF16)
                rows.append(rw)
                vvs.append(vv)
                ss.append(jnp.where(mask, _dot_nt(q, kk), NEG_INF))
            ms = [jnp.max(s, axis=-1, keepdims=True) for s in ss]
            ps = [jnp.exp(s - m) for s, m in zip(ss, ms)]
            for rw, m, p, vv in zip(rows, ms, ps, vvs):
                m_scr[bi][rw, :] = jnp.maximum(m, floor_tile)
                l_scr[bi][rw, :] = jnp.maximum(jnp.sum(p, axis=-1, keepdims=True), floor_tile)
                acc_scr[bi][rw, :] = _dot(p.astype(BF16), vv)
            return carry

        assert (d * nb) % ATTN_GROUP == 0
        lax.fori_loop(0, d * nb // ATTN_GROUP, blocks, 0)

    def merge(c, carry):
        rows = pl.ds(pl.multiple_of(c * blk, blk), blk)
        ms = [m[rows, :] for m in m_scr]
        m_max = functools.reduce(jnp.maximum, ms)
        ws = [jnp.exp(m - m_max) for m in ms]
        num = functools.reduce(jnp.add, [w * a[rows, :] for w, a in zip(ws, acc_scr)])
        den = functools.reduce(jnp.add, [w * l[rows, :] for w, l in zip(ws, l_scr)])
        o_ref[rows, :] = (num / den).astype(o_ref.dtype)
        return carry

    lax.fori_loop(0, seq // blk, merge, 0)


def _attn_prompt(qk, z, n_seq, seq, heads):
    return pl.pallas_call(
        functools.partial(_attn_prompt_body, seq=seq),
        grid=(n_seq, heads),
        in_specs=[pl.BlockSpec((seq, DH_A), lambda b, h: (b, h)),
                  pl.BlockSpec((seq, DH_A), lambda b, h: (b, heads + h)),
                  pl.BlockSpec((seq, DH_A), lambda b, h: (b, 2 * heads + h))],
        out_specs=pl.BlockSpec((seq, DH_A), lambda b, h: (b, h)),
        out_shape=jax.ShapeDtypeStruct((n_seq * seq, heads * DH_A), BF16),
        scratch_shapes=[pltpu.VMEM((seq, DH_A), F32)] * (3 * len(DILATED_BRANCHES)),
        compiler_params=_cparams("parallel", "arbitrary"),
        name="attn_prompt",
    )(qk, qk, z)


def _sample_weights(win_buf, dec_seq, heads, ck):
    n = np.arange(win_buf + dec_seq)[None, :]
    s = np.arange(dec_seq)[:, None]
    dist = win_buf + s - n
    cnt = np.zeros((dec_seq, win_buf + dec_seq), np.float32)
    for window, d in DILATED_BRANCHES:
        cnt += (dist >= 0) & (dist <= window) & (dist % d == 0)
    eye = np.eye(heads, dtype=np.float32)
    main = np.einsum("sn,hg->nhgs", cnt[:, :win_buf], eye).reshape(win_buf // ck, ck * heads, heads * dec_seq)
    new = np.einsum("sn,hg->hngs", cnt[:, win_buf:], eye).reshape(heads * dec_seq, heads * dec_seq)
    return main, new


def _attn_sample_body(q_ref, kc_ref, vc_ref, kn_ref, vn_ref, w_ref, wn_ref, o_ref, q_scr, m_scr, l_scr, acc_scr,
                      *, heads, dec_seq, n_chunks):
    c = pl.program_id(1)

    def by_head(ref):
        return jnp.concatenate([ref[:, h * DH_A:(h + 1) * DH_A] for h in range(heads)], axis=0)

    @pl.when(c == 0)
    def _():
        q_scr[...] = (by_head(q_ref) * (DH_A ** -0.5)).astype(BF16)
        m_scr[...] = jnp.full(m_scr.shape, NEG_INF, F32)
        l_scr[...] = jnp.zeros(l_scr.shape, F32)
        acc_scr[...] = jnp.zeros(acc_scr.shape, F32)

    def step(kb, vb, w):
        s = jnp.where(w > 0, _dot_nt(kb, q_scr[...]), NEG_INF)
        m_old = m_scr[...]
        m_new = jnp.maximum(m_old, jnp.max(s, axis=0, keepdims=True))
        alpha = jnp.exp(m_old - m_new)
        p = w * jnp.exp(s - m_new)
        l_scr[...] = alpha * l_scr[...] + jnp.sum(p, axis=0, keepdims=True)
        acc_scr[...] = alpha * acc_scr[...] + _dot_tn(vb, p.astype(BF16))
        m_scr[...] = m_new

    @pl.when(c < n_chunks)
    def _():
        step(kc_ref[0].astype(BF16), vc_ref[0].astype(BF16), w_ref[c].astype(F32))

    @pl.when(c == n_chunks)
    def _():
        step(by_head(kn_ref).astype(BF16), by_head(vn_ref).astype(BF16), wn_ref[...].astype(F32))
        out = (acc_scr[...] / l_scr[...]).T
        for h in range(heads):
            o_ref[:, h * DH_A:(h + 1) * DH_A] = out[h * dec_seq:(h + 1) * dec_seq, :]


def _attn_sample(qk, z, cache_k, cache_v, row0, n_seq, dec_seq, heads):
    assert heads * dec_seq == LANES
    win_buf = cache_k.shape[1] // heads
    width = heads * DH_A
    ck = 256
    n_chunks = win_buf // ck
    w_main, w_new = _sample_weights(win_buf, dec_seq, heads, ck)
    rb = row0 // dec_seq

    def new_spec(col):
        return pl.BlockSpec((dec_seq, width), lambda b, c: (rb + b, col))

    cache_spec = pl.BlockSpec((1, ck * heads, DH_A), lambda b, c: (b, jnp.minimum(c, n_chunks - 1), 0))
    return pl.pallas_call(
        functools.partial(_attn_sample_body, heads=heads, dec_seq=dec_seq, n_chunks=n_chunks),
        grid=(n_seq, n_chunks + 1),
        in_specs=[new_spec(0), cache_spec, cache_spec, new_spec(1), new_spec(2),
                  pl.BlockSpec(w_main.shape, lambda b, c: (0, 0, 0)),
                  pl.BlockSpec(w_new.shape, lambda b, c: (0, 0))],
        out_specs=pl.BlockSpec((dec_seq, width), lambda b, c: (b, 0)),
        out_shape=jax.ShapeDtypeStruct((n_seq * dec_seq, width), F32),
        scratch_shapes=[pltpu.VMEM((LANES, DH_A), BF16), pltpu.VMEM((1, LANES), F32), pltpu.VMEM((1, LANES), F32),
                        pltpu.VMEM((DH_A, LANES), F32)],
        compiler_params=_cparams("parallel", "arbitrary"),
        name="attn_sample",
    )(qk, cache_k, cache_v, qk, z, jnp.asarray(w_main, BF16), jnp.asarray(w_new, BF16))


def _gla_body(q_ref, k_ref, v_ref, r_ref, gl_ref, wg_ref, bg_ref, gn_ref, s0_ref, o_ref, sn_ref, st_scr,
              *, chunk, heads, dk, dv):
    c = pl.program_id(1)
    hs = range(heads)

    @pl.when(c == 0)
    def _():
        for h in hs:
            st_scr[h] = s0_ref[0, h].T

    cp = max(chunk, LANES // 2)

    def pad(a):
        return a if cp == chunk else jnp.concatenate([a, jnp.zeros((cp - chunk, a.shape[1]), a.dtype)], axis=0)

    def cols(ref, h, w):
        return ref[:, h * w:(h + 1) * w]

    ti = lax.broadcasted_iota(jnp.int32, (cp, cp), 0)
    tj = lax.broadcasted_iota(jnp.int32, (cp, cp), 1)
    tril = ti >= tj
    trilf = tril.astype(F32)
    glow = gl_ref[...].astype(BF16)
    log_a = [pad(jax.nn.log_sigmoid(_dot(glow, cols(wg_ref, h, dk)) + cols(bg_ref, h, dk)) / GATE_TAU) for h in hs]
    b = [jnp.dot(trilf, la, preferred_element_type=F32, precision=lax.Precision.HIGHEST) for la in log_a]
    b_last = [x[cp - 1:cp, :] for x in b]
    k = [pad(cols(k_ref, h, dk)) for h in hs]
    v = [pad(cols(v_ref, h, dv)).astype(BF16) for h in hs]
    q_in = [(pad(cols(q_ref, h, dk)) * (dk ** -0.5) * jnp.exp(b[h])).astype(BF16) for h in hs]
    k_in = [(k[h] * jnp.exp(-b[h])).astype(BF16) for h in hs]
    k_dec = [(k[h] * jnp.exp(b_last[h] - b[h])).astype(BF16) for h in hs]
    att = [jnp.where(tril, _dot_nt(q_in[h], k_in[h]), 0.0).astype(BF16) for h in hs]
    st = [st_scr[h] for h in hs]
    o = [(_dot(att[h], v[h]) + _dot_nt(q_in[h], st[h].astype(BF16)))[:chunk] for h in hs]
    st_new = [st[h] * jnp.exp(b_last[h]) + _dot_tn(v[h], k_dec[h]) for h in hs]
    for h in hs:
        st_scr[h] = st_new[h]
        ms = jnp.mean(o[h] * o[h], axis=-1, keepdims=True)
        gated = (o[h] * lax.rsqrt(ms + EPS) * gn_ref[...]) * jax.nn.silu(cols(r_ref, h, dv))
        o_ref[:, h * dv:(h + 1) * dv] = gated.astype(o_ref.dtype)

    @pl.when(c == pl.num_programs(1) - 1)
    def _():
        for h in hs:
            sn_ref[0, h] = st_new[h].T


def _gla(z, zr, zg, w_gate, b_gate, gla_norm, s0, *, row0, chunk, n_chunks, col_q, col_k, col_v, out_dtype):
    n_seq, heads, dk, dv = s0.shape
    rb = row0 // chunk
    wk, wv = heads * dk, heads * dv
    assert col_q % wk == 0 and col_k % wk == 0 and col_v % wv == 0 and zr.shape[1] == wv

    def rows(b, c):
        return rb + b * n_chunks + c

    return pl.pallas_call(
        functools.partial(_gla_body, chunk=chunk, heads=heads, dk=dk, dv=dv),
        grid=(n_seq, n_chunks),
        in_specs=[pl.BlockSpec((chunk, wk), lambda b, c: (rows(b, c), col_q // wk)),
                  pl.BlockSpec((chunk, wk), lambda b, c: (rows(b, c), col_k // wk)),
                  pl.BlockSpec((chunk, wv), lambda b, c: (rows(b, c), col_v // wv)),
                  pl.BlockSpec((chunk, wv), lambda b, c: (rows(b, c), 0)),
                  pl.BlockSpec((chunk, LANES), lambda b, c: (rows(b, c), 0)),
                  pl.BlockSpec((LANES, wk), lambda b, c: (0, 0)),
                  pl.BlockSpec((1, wk), lambda b, c: (0, 0)),
                  pl.BlockSpec((1, dv), lambda b, c: (0, 0)),
                  pl.BlockSpec((1, heads, dk, dv), lambda b, c: (b, 0, 0, 0))],
        out_specs=[pl.BlockSpec((chunk, wv), lambda b, c: (b * n_chunks + c, 0)),
                   pl.BlockSpec((1, heads, dk, dv), lambda b, c: (b, 0, 0, 0))],
        out_shape=[jax.ShapeDtypeStruct((n_seq * n_chunks * chunk, wv), out_dtype),
                   jax.ShapeDtypeStruct((n_seq, heads, dk, dv), F32)],
        scratch_shapes=[pltpu.VMEM((heads, dv, dk), F32)],
        compiler_params=_cparams("parallel", "arbitrary"),
        name="gla",
    )(z, z, z, zr, zg, w_gate, b_gate, gla_norm, s0)


def _mm_out_body(a1p_ref, a1s_ref, a2p_ref, a2s_ref, w1_ref, w2_ref, xp_ref, xs_ref, o_ref, *, n_full, rem):
    i = pl.program_id(0)

    def go(a1, a2, x):
        o_ref[...] = x + _dot(a1, w1_ref[...]) + _dot(a2, w2_ref[...])

    @pl.when(i < n_full)
    def _():
        go(a1p_ref[...], a2p_ref[...], xp_ref[...])

    @pl.when(i == n_full)
    def _():
        def cat(p_ref, s_ref):
            return jnp.concatenate([p_ref[:rem], s_ref[...].astype(p_ref.dtype)], axis=0)

        go(cat(a1p_ref, a1s_ref), cat(a2p_ref, a2s_ref), cat(xp_ref, xs_ref))


def _mm_out(a1p, a1s, a2p, a2s, w1, w2, xp, xs, tm, tn):
    t_p, k1 = a1p.shape
    t_s = a1s.shape[0]
    k2 = a2p.shape[1]
    n = w1.shape[1]
    n_full, rem = divmod(t_p, tm)
    assert rem > 0 and rem + t_s == tm

    def pspec(k):
        return pl.BlockSpec((tm, k), lambda i, j: (i, 0))

    def sspec(k):
        return pl.BlockSpec((t_s, k), lambda i, j: (0, 0))

    return pl.pallas_call(
        functools.partial(_mm_out_body, n_full=n_full, rem=rem),
        grid=(n_full + 1, n // tn),
        in_specs=[pspec(k1), sspec(k1), pspec(k2), sspec(k2),
                  pl.BlockSpec((k1, tn), lambda i, j: (0, j)), pl.BlockSpec((k2, tn), lambda i, j: (0, j)),
                  pl.BlockSpec((tm, tn), lambda i, j: (i, j)), pl.BlockSpec((t_s, tn), lambda i, j: (0, j))],
        out_specs=pl.BlockSpec((tm, tn), lambda i, j: (i, j)),
        out_shape=jax.ShapeDtypeStruct((t_p + t_s, n), F32),
        compiler_params=_cparams("parallel", "arbitrary"),
        name="mm_out",
    )(a1p, a1s, a2p, a2s, w1, w2, xp, xs)


def _rows_iota(shape):
    return lax.broadcasted_iota(jnp.int32, shape, 0).astype(F32)


def _topk_cols(s, ranks, k):
    big = float(2 ** 20)
    for _ in range(k):
        m = jnp.max(s, axis=0, keepdims=True)
        r = jnp.min(jnp.where(s == m, ranks, big), axis=0, keepdims=True)
        sel = ranks == r
        yield m, r, sel
        s = jnp.where(sel, -jnp.inf, s)


def _stack_rows(rows):
    n = len(rows)
    rid = lax.broadcasted_iota(jnp.int32, (n, rows[0].shape[1]), 0)
    out = jnp.broadcast_to(rows[0], (n, rows[0].shape[1]))
    for i in range(1, n):
        out = jnp.where(rid == i, rows[i], out)
    return out


def _route_body(q_ref, k1_ref, k2_ref, ii_ref, jj_ref, g_ref):
    tt = q_ref.shape[0]
    half = PEER_DKEY // 2
    topk = PEER_TOPK
    key_rank = _rows_iota((N_KEYS, tt))
    r8 = _rows_iota((8, tt))
    r16 = _rows_iota((topk, tt))
    gates, experts = [], []
    for h in range(PEER_HEADS):
        qh = q_ref[:, h * PEER_DKEY:(h + 1) * PEER_DKEY]
        qn = (qh * lax.rsqrt(jnp.mean(qh * qh, axis=-1, keepdims=True) + EPS)).astype(BF16)
        s1 = _dot_nt(k1_ref[h], qn[:, :half])
        s2 = _dot_nt(k2_ref[h], qn[:, half:])
        v1, i1 = zip(*[(m, r) for m, r, _ in _topk_cols(s1, key_rank, topk)])
        v2, i2 = zip(*[(m, r) for m, r, _ in _topk_cols(s2, key_rank, topk)])
        v1t, i1t, v2t, i2t = (_stack_rows(list(x)) for x in (v1, i1, v2, i2))
        cand = [v1[0] + v2t]
        code = [i1[0] * N_KEYS + i2t]
        flat = [r16]
        for a in range(1, 8):
            ok = r8 < float(topk // (a + 1))
            cand.append(jnp.where(ok, v1[a] + v2t[:8], -jnp.inf))
            code.append(i1[a] * N_KEYS + i2t[:8])
            flat.append(r8 + float(a * topk))
        cand.append(v1t[8:] + v2[0])
        code.append(i1t[8:] * N_KEYS + i2[0])
        flat.append((r8 + 8.0) * float(topk))
        cand, code, flat = (jnp.concatenate(x, axis=0) for x in (cand, code, flat))
        top, sel_codes = [], []
        for m, _, sel in _topk_cols(cand, flat, topk):
            top.append(m)
            sel_codes.append(jnp.max(jnp.where(sel, code, -1.0), axis=0, keepdims=True))
        ex = [jnp.exp(m - top[0]) for m in top]
        inv_z = 1.0 / functools.reduce(jnp.add, ex)
        gates.append(_stack_rows([e * inv_z for e in ex]))
        experts.append(_stack_rows(sel_codes))
    g_ref[...] = jnp.concatenate(gates, axis=0).T
    e = jnp.concatenate(experts, axis=0).T.astype(jnp.int32)
    ii_ref[...] = lax.shift_right_logical(e, N_KEYS.bit_length() - 1)
    jj_ref[...] = lax.bitwise_and(e, N_KEYS - 1)


def _route(q, keys1, keys2, tt):
    t, n = q.shape
    no = PEER_HEADS * PEER_TOPK
    kspec = pl.BlockSpec(keys1.shape, lambda i: (0, 0, 0))
    ospec = pl.BlockSpec((tt, no), lambda i: (i, 0))
    return pl.pallas_call(
        _route_body,
        grid=(t // tt,),
        in_specs=[pl.BlockSpec((tt, n), lambda i: (i, 0)), kspec, kspec],
        out_specs=[ospec, ospec, ospec],
        out_shape=[jax.ShapeDtypeStruct((t, no), jnp.int32), jax.ShapeDtypeStruct((t, no), jnp.int32),
                   jax.ShapeDtypeStruct((t, no), F32)],
        compiler_params=_cparams("parallel"),
        name="peer_route",
    )(q, keys1, keys2)


EXPAND_GROUP = 16
EXPAND_PITCH = N_KEYS + 8


def _expand_body(ii_ref, jj_ref, g_ref, o_ref, scr):
    te = ii_ref.shape[0]
    sub = lax.broadcasted_iota(jnp.int32, (N_KEYS, ii_ref.shape[1]), 0)

    def group(gi, carry):
        base = pl.multiple_of(gi * EXPAND_GROUP, EXPAND_GROUP)
        for u in range(EXPAND_GROUP):
            ii = ii_ref[pl.ds(base + u, 1), :]
            jj = jj_ref[pl.ds(base + u, 1), :]
            g = g_ref[pl.ds(base + u, 1), :]
            a = jnp.where(sub == ii, 1.0, 0.0).astype(BF16)
            b = jnp.where(sub == jj, g, 0.0).astype(BF16)
            scr[u * EXPAND_PITCH:u * EXPAND_PITCH + N_KEYS, :] = _dot_nt(a, b)
        for i in range(N_KEYS):
            rows = scr[pl.ds(i, EXPAND_GROUP, stride=EXPAND_PITCH), :]
            o_ref[pl.ds(base, EXPAND_GROUP), i * N_KEYS:(i + 1) * N_KEYS] = rows.astype(o_ref.dtype)
        return carry

    lax.fori_loop(0, te // EXPAND_GROUP, group, 0)


def _expand(ii, jj, g, te):
    t, no = ii.shape
    spec = pl.BlockSpec((te, no), lambda i: (i, 0))
    return pl.pallas_call(
        _expand_body,
        grid=(t // te,),
        in_specs=[spec, spec, spec],
        out_specs=pl.BlockSpec((te, N_KEYS * N_KEYS), lambda i: (i, 0)),
        out_shape=jax.ShapeDtypeStruct((t, N_KEYS * N_KEYS), BF16),
        scratch_shapes=[pltpu.VMEM((EXPAND_GROUP * EXPAND_PITCH, N_KEYS), F32)],
        compiler_params=_cparams("parallel"),
        name="peer_expand",
    )(ii, jj, g)


def _peer_a_body(h_ref, u_ref, g_ref, o_ref):
    a = _dot_nt(h_ref[...], u_ref[...].astype(BF16))
    o_ref[...] = (g_ref[...].astype(F32) * jax.nn.gelu(a)).astype(o_ref.dtype)


def _peer_a(h, u, gates, tm, tn):
    t, k = h.shape
    e = u.shape[0]
    return pl.pallas_call(
        _peer_a_body,
        grid=(t // tm, e // tn),
        in_specs=[pl.BlockSpec((tm, k), lambda i, j: (i, 0)), pl.BlockSpec((tn, k), lambda i, j: (j, 0)),
                  pl.BlockSpec((tm, tn), lambda i, j: (i, j))],
        out_specs=pl.BlockSpec((tm, tn), lambda i, j: (i, j)),
        out_shape=jax.ShapeDtypeStruct((t, e), BF16),
        compiler_params=_cparams("parallel", "arbitrary"),
        name="peer_act",
    )(h, u, gates)


def _peer_v_body(c_ref, v_ref, x_ref, yp_ref, ys_ref, acc_scr, *, n_full, rem):
    i = pl.program_id(0)
    kk = pl.program_id(2)
    last = pl.num_programs(2) - 1

    @pl.when(kk == 0)
    def _():
        acc_scr[...] = x_ref[...]

    acc_scr[...] += _dot(c_ref[...], v_ref[...].astype(BF16))

    @pl.when(kk == last)
    def _():
        yp_ref[...] = acc_scr[...]

    @pl.when((kk == last) & (i == n_full))
    def _():
        ys_ref[...] = acc_scr[rem:, :]


def _peer_v(c, v, x, t_p, tm, tn, tk):
    t, e = c.shape
    n = v.shape[1]
    t_s = t - t_p
    n_full, rem = divmod(t_p, tm)
    assert rem > 0 and rem + t_s == tm
    return pl.pallas_call(
        functools.partial(_peer_v_body, n_full=n_full, rem=rem),
        grid=(t // tm, n // tn, e // tk),
        in_specs=[pl.BlockSpec((tm, tk), lambda i, j, k: (i, k)), pl.BlockSpec((tk, tn), lambda i, j, k: (k, j)),
                  pl.BlockSpec((tm, tn), lambda i, j, k: (i, j))],
        out_specs=[pl.BlockSpec((tm, tn), lambda i, j, k: (i, j)),
                   pl.BlockSpec((t_s, tn), lambda i, j, k: (0, jnp.where(i == n_full, j, 0)))],
        out_shape=[jax.ShapeDtypeStruct((t_p, n), F32), jax.ShapeDtypeStruct((t_s, n), F32)],
        scratch_shapes=[pltpu.VMEM((tm, tn), F32)],
        compiler_params=_cparams("arbitrary", "arbitrary", "arbitrary"),
        name="peer_mix",
    )(c, v, x)


def _token_tile(t, cap):
    best = 16
    for c in range(16, cap + 1, 16):
        if t % c == 0:
            best = c
    return best


def _rope_tables(pos):
    half = DH_A // 2
    inv = ROPE_THETA ** (-jnp.arange(half, dtype=F32) / half)
    ang = pos.astype(F32)[:, None] * inv[None, :]
    cos, sin = jnp.cos(ang), jnp.sin(ang)
    return jnp.concatenate([cos, cos], axis=1), jnp.concatenate([-sin, sin], axis=1)


def kernel(x_prompt, x_sample, cache_attn_k, cache_attn_v, state_gla, norm_mix, w_in, q_norm, k_norm, w_gate, b_gate,
           gla_norm, w_out, norm_ffn, w_peer_q, peer_keys1, peer_keys2, peer_u, peer_v):
    n_p, seq, d_model = x_prompt.shape
    n_s, dec_seq, _ = x_sample.shape
    depth = w_in.shape[0]
    heads_a = cache_attn_k.shape[3]
    w_a = heads_a * DH_A
    _, _, heads_b, dk, dv = state_gla.shape
    w_b = heads_b * dv
    assert seq == DILATED_BRANCHES[-1][0] and heads_b == H_B and cache_attn_k.shape[2] == seq
    t_p = n_p * seq
    t_s = n_s * dec_seq
    t_all = t_p + t_s
    tm = _token_tile(t_all, 1056)
    tt = _token_tile(t_s, 256)

    xp = x_prompt.reshape(t_p, d_model)
    xs = x_sample.reshape(t_s, d_model)
    pos = jnp.concatenate([jnp.tile(jnp.arange(seq, dtype=jnp.int32), n_p),
                           jnp.tile(PAST_LEN + jnp.arange(dec_seq, dtype=jnp.int32), n_s)])
    cos2, sin2 = _rope_tables(pos)

    c_qb = 3 * w_a
    c_kb = c_qb + heads_b * dk
    c_vb = c_kb + heads_b * dk
    c_g = c_vb + w_b
    c_r = c_g + GATE_RANK

    outs = dict(kp=[], vp=[], ks=[], vs=[], sp=[], ss=[])
    for l in range(depth):
        w_r = w_in[l][:, c_r:].astype(BF16)
        w_glow = jnp.pad(w_in[l][:, c_g:c_r].astype(BF16), ((0, 0), (0, LANES - GATE_RANK)))
        w_gate_p = jnp.pad(w_gate[l].astype(BF16), ((0, LANES - GATE_RANK), (0, 0)))
        qk_gain = jnp.concatenate([jnp.tile(q_norm[l], heads_a), jnp.tile(k_norm[l], heads_a)]).reshape(1, 2 * w_a)

        if l == 0:
            h = _rmsnorm_pair(xp, xs, norm_mix[l], tt)
        else:
            h = _rmsnorm(x, norm_mix[l], tt)
            xp, xs = x[:t_p], x[t_p:]
        z = _matmul(h, w_in[l], tm, 512, n_cols=c_g)
        zr = _matmul(h, w_r, tm, 1024)
        zg = _matmul(h, w_glow, tm, LANES)
        qk = _qk_post(z, qk_gain, cos2, sin2, tt, heads_a)

        o_a_p = _attn_prompt(qk, z, n_p, seq, heads_a)
        o_a_s = _attn_sample(qk, z, cache_attn_k[l].reshape(n_s, seq * heads_a, DH_A),
                             cache_attn_v[l].reshape(n_s, seq * heads_a, DH_A), t_p, n_s, dec_seq, heads_a)
        gla_args = (z, zr, zg, w_gate_p, b_gate[l].reshape(1, -1), gla_norm[l].reshape(1, -1))
        gla_cols = dict(col_q=c_qb, col_k=c_kb, col_v=c_vb)
        o_b_p, s_p = _gla(*gla_args, jnp.zeros((n_p, heads_b, dk, dv), F32), row0=0, chunk=GLA_CHUNK,
                          n_chunks=seq // GLA_CHUNK, out_dtype=BF16, **gla_cols)
        o_b_s, s_s = _gla(*gla_args, state_gla[l], row0=t_p, chunk=dec_seq, n_chunks=1, out_dtype=F32, **gla_cols)

        w_o1 = w_out[l][:w_a].astype(BF16)
        w_o2 = w_out[l][w_a:].astype(BF16)
        x1 = _mm_out(o_a_p, o_a_s, o_b_p, o_b_s, w_o1, w_o2, xp, xs, tm, 512)

        h2 = _rmsnorm(x1, norm_ffn[l], tt)
        q = _matmul(h2, w_peer_q[l], tm, 512)
        ii, jj, g = _route(q, peer_keys1[l].astype(BF16), peer_keys2[l].astype(BF16), LANES)
        gates = _expand(ii, jj, g, _token_tile(t_all, 128))
        c_act = _peer_a(h2, peer_u[l], gates, tm, 512)
        yp, ys = _peer_v(c_act, peer_v[l], x1, t_p, tm, 1024, 1024)
        if l + 1 < depth:
            x = jnp.concatenate([yp, ys], axis=0)

        outs["kp"].append(qk[:t_p, w_a:].reshape(n_p, seq, heads_a, DH_A))
        outs["vp"].append(z[:t_p, 2 * w_a:3 * w_a].reshape(n_p, seq, heads_a, DH_A))
        outs["ks"].append(qk[t_p:, w_a:].reshape(n_s, dec_seq, heads_a, DH_A))
        outs["vs"].append(z[t_p:, 2 * w_a:3 * w_a].reshape(n_s, dec_seq, heads_a, DH_A))
        outs["sp"].append(s_p)
        outs["ss"].append(s_s)

    y_prompt = yp.reshape(n_p, seq, d_model)
    y_sample = ys.reshape(n_s, dec_seq, d_model)
    return (y_prompt, y_sample, jnp.stack(outs["kp"]), jnp.stack(outs["vp"]), jnp.stack(outs["ks"]),
            jnp.stack(outs["vs"]), jnp.stack(outs["sp"]), jnp.stack(outs["ss"]))
```

```python
import functools

import numpy as np
import jax
import jax.numpy as jnp
from jax import lax
from jax.experimental import pallas as pl
from jax.experimental.pallas import tpu as pltpu

F32 = jnp.float32
BF16 = jnp.bfloat16

DH_A = 128
DILATED_BRANCHES = ((128, 1), (512, 4), (2048, 16))
ROPE_THETA = 10000.0
PAST_LEN = 8192
H_B = 4
GATE_RANK = 16
GATE_TAU = 16.0
GLA_CHUNK = 64
N_KEYS = 128
PEER_HEADS = 8
PEER_DKEY = 256
PEER_TOPK = 16
EPS = 1e-6
NEG_INF = -1e30

LANES = 128
VMEM_LIMIT = 56 * 1024 * 1024


def _cparams(*sem):
    return pltpu.CompilerParams(dimension_semantics=sem, vmem_limit_bytes=VMEM_LIMIT)


def _dot(a, b):
    return jnp.dot(a, b, preferred_element_type=F32)


def _dot_nt(a, b):
    return lax.dot_general(a, b, (((1,), (1,)), ((), ())), preferred_element_type=F32)


def _dot_tn(a, b):
    return lax.dot_general(a, b, (((0,), (0,)), ((), ())), preferred_element_type=F32)


def _rmsnorm_rows(x, gain):
    ms = jnp.mean(x * x, axis=-1, keepdims=True)
    return (x * lax.rsqrt(ms + EPS) * gain).astype(BF16)


def _rmsnorm_body(x_ref, g_ref, o_ref):
    o_ref[...] = _rmsnorm_rows(x_ref[...], g_ref[...])


def _rmsnorm(x, gain, tt):
    t, d = x.shape
    return pl.pallas_call(
        _rmsnorm_body,
        grid=(t // tt,),
        in_specs=[pl.BlockSpec((tt, d), lambda i: (i, 0)), pl.BlockSpec((1, d), lambda i: (0, 0))],
        out_specs=pl.BlockSpec((tt, d), lambda i: (i, 0)),
        out_shape=jax.ShapeDtypeStruct((t, d), BF16),
        compiler_params=_cparams("parallel"),
        name="rmsnorm",
    )(x, gain.reshape(1, d))


def _rmsnorm_pair_body(xp_ref, xs_ref, g_ref, o_ref, *, n_p):
    i = pl.program_id(0)

    @pl.when(i < n_p)
    def _():
        o_ref[...] = _rmsnorm_rows(xp_ref[...], g_ref[...])

    @pl.when(i >= n_p)
    def _():
        o_ref[...] = _rmsnorm_rows(xs_ref[...], g_ref[...])


def _rmsnorm_pair(xp, xs, gain, tt):
    t_p, d = xp.shape
    t_s = xs.shape[0]
    assert t_p % tt == 0 and t_s % tt == 0
    n_p = t_p // tt
    return pl.pallas_call(
        functools.partial(_rmsnorm_pair_body, n_p=n_p),
        grid=((t_p + t_s) // tt,),
        in_specs=[pl.BlockSpec((tt, d), lambda i: (jnp.minimum(i, n_p - 1), 0)),
                  pl.BlockSpec((tt, d), lambda i: (jnp.maximum(i - n_p, 0), 0)),
                  pl.BlockSpec((1, d), lambda i: (0, 0))],
        out_specs=pl.BlockSpec((tt, d), lambda i: (i, 0)),
        out_shape=jax.ShapeDtypeStruct((t_p + t_s, d), BF16),
        compiler_params=_cparams("arbitrary"),
        name="rmsnorm_in",
    )(xp, xs, gain.reshape(1, d))


def _mm_body(a_ref, w_ref, o_ref):
    o_ref[...] = _dot(a_ref[...], w_ref[...].astype(BF16))


def _matmul(a, w, tm, tn, n_cols=None):
    t, k = a.shape
    n = w.shape[1] if n_cols is None else n_cols
    return pl.pallas_call(
        _mm_body,
        grid=(t // tm, n // tn),
        in_specs=[pl.BlockSpec((tm, k), lambda i, j: (i, 0)), pl.BlockSpec((k, tn), lambda i, j: (0, j))],
        out_specs=pl.BlockSpec((tm, tn), lambda i, j: (i, j)),
        out_shape=jax.ShapeDtypeStruct((t, n), F32),
        compiler_params=_cparams("parallel", "arbitrary"),
        name="matmul",
    )(a, w)


def _qk_post_body(z_ref, g_ref, cos_ref, sin_ref, o_ref, *, heads):
    cos = cos_ref[...]
    sin = sin_ref[...]
    for h in range(heads):
        sl = slice(h * DH_A, (h + 1) * DH_A)
        x = z_ref[:, sl]
        ms = jnp.mean(x * x, axis=-1, keepdims=True)
        y = x * lax.rsqrt(ms + EPS) * g_ref[:, sl]
        o_ref[:, sl] = y * cos + pltpu.roll(y, DH_A // 2, axis=1) * sin


def _qk_post(z, gains, cos2, sin2, tt, heads):
    t = z.shape[0]
    n = gains.shape[1]
    bw = heads * DH_A
    return pl.pallas_call(
        functools.partial(_qk_post_body, heads=heads),
        grid=(t // tt, n // bw),
        in_specs=[pl.BlockSpec((tt, bw), lambda i, j: (i, j)),
                  pl.BlockSpec((1, bw), lambda i, j: (0, j)),
                  pl.BlockSpec((tt, DH_A), lambda i, j: (i, 0)),
                  pl.BlockSpec((tt, DH_A), lambda i, j: (i, 0))],
        out_specs=pl.BlockSpec((tt, bw), lambda i, j: (i, j)),
        out_shape=jax.ShapeDtypeStruct((t, n), F32),
        compiler_params=_cparams("parallel", "arbitrary"),
        name="qk_post",
    )(z, gains, cos2, sin2)


ATTN_GROUP = 4


def _attn_prompt_body(q_ref, k_ref, v_ref, o_ref, *scr, seq):
    blk = DH_A
    scale = DH_A ** -0.5
    n_br = len(DILATED_BRANCHES)
    m_scr, l_scr, acc_scr = scr[:n_br], scr[n_br:2 * n_br], scr[2 * n_br:]
    qi = lax.broadcasted_iota(jnp.int32, (blk, 2 * blk), 0)
    kc = lax.broadcasted_iota(jnp.int32, (blk, 2 * blk), 1)
    band = (kc >= qi) & (kc <= qi + blk)
    cur_half = kc >= blk
    qi1 = lax.broadcasted_iota(jnp.int32, (blk, blk), 0)
    causal = lax.broadcasted_iota(jnp.int32, (blk, blk), 1) <= qi1
    floor_tile = 2.0 * NEG_INF - qi1.astype(F32)

    for bi, (window, d) in enumerate(DILATED_BRANCHES):
        assert window // d == blk
        nb = seq // d // blk

        def blocks(it, carry, bi=bi, d=d, nb=nb):
            rows, vvs, ss = [], [], []
            for u in range(ATTN_GROUP):
                idx = it * ATTN_GROUP + u
                if nb == 1:
                    rw = pl.ds(idx, blk, stride=d)
                    kk = k_ref[rw, :].astype(BF16)
                    vv = v_ref[rw, :].astype(BF16)
                    mask = causal
                else:
                    r = idx // nb
                    j = idx % nb
                    start = r + j * (blk * d)
                    rw = pl.ds(start, blk, stride=d)
                    prows = pl.ds(jnp.maximum(start - blk * d, r), blk, stride=d)
                    kk = jnp.concatenate([k_ref[prows, :], k_ref[rw, :]], axis=0).astype(BF16)
                    vv = jnp.concatenate([v_ref[prows, :], v_ref[rw, :]], axis=0).astype(BF16)
                    mask = band & (cur_half | (j > 0))
                q = (q_ref[rw, :] * scale).astype(BF16)
                rows.append(rw)
                vvs.append(vv)
                ss.append(jnp.where(mask, _dot_nt(q, kk), NEG_INF))
            ms = [jnp.max(s, axis=-1, keepdims=True) for s in ss]
            ps = [jnp.exp(s - m) for s, m in zip(ss, ms)]
            for rw, m, p, vv in zip(rows, ms, ps, vvs):
                m_scr[bi][rw, :] = jnp.maximum(m, floor_tile)
                l_scr[bi][rw, :] = jnp.maximum(jnp.sum(p, axis=-1, keepdims=True), floor_tile)
                acc_scr[bi][rw, :] = _dot(p.astype(BF16), vv)
            return carry

        assert (d * nb) % ATTN_GROUP == 0
        lax.fori_loop(0, d * nb // ATTN_GROUP, blocks, 0)

    def merge(c, carry):
        rows = pl.ds(pl.multiple_of(c * blk, blk), blk)
        ms = [m[rows, :] for m in m_scr]
        m_max = functools.reduce(jnp.maximum, ms)
        ws = [jnp.exp(m - m_max) for m in ms]
        num = functools.reduce(jnp.add, [w * a[rows, :] for w, a in zip(ws, acc_scr)])
        den = functools.reduce(jnp.add, [w * l[rows, :] for w, l in zip(ws, l_scr)])
        o_ref[rows, :] = (num / den).astype(o_ref.dtype)
        return carry

    lax.fori_loop(0, seq // blk, merge, 0)


def _attn_prompt(qk, z, n_seq, seq, heads):
    return pl.pallas_call(
        functools.partial(_attn_prompt_body, seq=seq),
        grid=(n_seq, heads),
        in_specs=[pl.BlockSpec((seq, DH_A), lambda b, h: (b, h)),
                  pl.BlockSpec((seq, DH_A), lambda b, h: (b, heads + h)),
                  pl.BlockSpec((seq, DH_A), lambda b, h: (b, 2 * heads + h))],
        out_specs=pl.BlockSpec((seq, DH_A), lambda b, h: (b, h)),
        out_shape=jax.ShapeDtypeStruct((n_seq * seq, heads * DH_A), BF16),
        scratch_shapes=[pltpu.VMEM((seq, DH_A), F32)] * (3 * len(DILATED_BRANCHES)),
        compiler_params=_cparams("parallel", "arbitrary"),
        name="attn_prompt",
    )(qk, qk, z)


SAMPLE_CHUNK_ROWS = 4096


def _sample_plan(win_buf, dec_seq, heads):
    dil = DILATED_BRANCHES[-1][1]
    dense_from = win_buf - DILATED_BRANCHES[-2][0]
    assert win_buf % dil == 0 and dense_from % dil == 0 and dec_seq * 2 == dil and dec_seq * heads == LANES
    pos_sparse = SAMPLE_CHUNK_ROWS // (dec_seq * heads) * dil
    pos_dense = SAMPLE_CHUNK_ROWS // heads
    assert dense_from % pos_sparse == 0 and (win_buf - dense_from) % pos_dense == 0 and dense_from % pos_dense == 0
    return dil, dense_from, dense_from // pos_sparse, (win_buf - dense_from) // pos_dense


def _sample_weights(win_buf, dec_seq, heads):
    dil, dense_from, n_sparse, n_dense = _sample_plan(win_buf, dec_seq, heads)
    n = np.arange(win_buf + dec_seq)[None, :]
    s = np.arange(dec_seq)[:, None]
    dist = win_buf + s - n
    cnt = np.zeros((dec_seq, win_buf + dec_seq), np.float32)
    for window, d in DILATED_BRANCHES:
        cnt += (dist >= 0) & (dist <= window) & (dist % d == 0)
    assert not cnt[:, :dense_from].reshape(dec_seq, -1, dil)[:, :, dec_seq:].any()
    eye = np.eye(heads, dtype=np.float32)
    sparse = cnt[:, :dense_from].reshape(dec_seq, -1, dil)[:, :, :dec_seq].reshape(dec_seq, -1)
    sparse = np.einsum("sn,hg->nhgs", sparse, eye).reshape(n_sparse, SAMPLE_CHUNK_ROWS, LANES)
    dense = np.einsum("sn,hg->nhgs", cnt[:, dense_from:win_buf], eye).reshape(n_dense, SAMPLE_CHUNK_ROWS, LANES)
    new = np.einsum("sn,hg->hngs", cnt[:, win_buf:], eye).reshape(LANES, LANES)
    return np.concatenate([sparse, dense], axis=0), new


def _attn_sample_body(q_ref, ks_ref, kd_ref, vs_ref, vd_ref, kn_ref, vn_ref, w_ref, wn_ref, o_ref,
                      q_scr, m_scr, l_scr, acc_scr, *, heads, dec_seq, n_sparse, n_dense):
    c = pl.program_id(1)
    n_chunks = n_sparse + n_dense

    def by_head(ref):
        return jnp.concatenate([ref[:, h * DH_A:(h + 1) * DH_A] for h in range(heads)], axis=0)

    @pl.when(c == 0)
    def _():
        q_scr[...] = (by_head(q_ref) * (DH_A ** -0.5)).astype(BF16)
        m_scr[...] = jnp.full(m_scr.shape, NEG_INF, F32)
        l_scr[...] = jnp.zeros(l_scr.shape, F32)
        acc_scr[...] = jnp.zeros(acc_scr.shape, F32)

    def step(kb, vb, w):
        s = jnp.where(w > 0, _dot_nt(kb, q_scr[...]), NEG_INF)
        m_old = m_scr[...]
        m_new = jnp.maximum(m_old, jnp.max(s, axis=0, keepdims=True))
        alpha = jnp.exp(m_old - m_new)
        p = w * jnp.exp(s - m_new)
        l_scr[...] = alpha * l_scr[...] + jnp.sum(p, axis=0, keepdims=True)
        acc_scr[...] = alpha * acc_scr[...] + _dot_tn(vb, p.astype(BF16))
        m_scr[...] = m_new

    @pl.when(c < n_sparse)
    def _():
        rows = (SAMPLE_CHUNK_ROWS, DH_A)
        step(ks_ref[0].reshape(rows).astype(BF16), vs_ref[0].reshape(rows).astype(BF16), w_ref[c].astype(F32))

    @pl.when((c >= n_sparse) & (c < n_chunks))
    def _():
        step(kd_ref[0].astype(BF16), vd_ref[0].astype(BF16), w_ref[c].astype(F32))

    @pl.when(c == n_chunks)
    def _():
        step(by_head(kn_ref).astype(BF16), by_head(vn_ref).astype(BF16), wn_ref[...].astype(F32))
        out = (acc_scr[...] / l_scr[...]).T
        for h in range(heads):
            o_ref[:, h * DH_A:(h + 1) * DH_A] = out[h * dec_seq:(h + 1) * dec_seq, :]


def _attn_sample(qk, z, cache_k, cache_v, row0, dec_seq):
    n_seq, win_buf, heads, _ = cache_k.shape
    width = heads * DH_A
    dil, dense_from, n_sparse, n_dense = _sample_plan(win_buf, dec_seq, heads)
    n_chunks = n_sparse + n_dense
    w_main, w_new = _sample_weights(win_buf, dec_seq, heads)
    rb = row0 // dec_seq
    groups = SAMPLE_CHUNK_ROWS // LANES
    dense0 = dense_from * heads // SAMPLE_CHUNK_ROWS

    def new_spec(col):
        return pl.BlockSpec((dec_seq, width), lambda b, c: (rb + b, col))

    sparse_view = lambda x: x.reshape(n_seq, win_buf // dil, dil * heads, DH_A)
    dense_view = lambda x: x.reshape(n_seq, win_buf * heads, DH_A)
    sparse_spec = pl.BlockSpec((1, groups, LANES, DH_A), lambda b, c: (b, jnp.minimum(c, n_sparse - 1), 0, 0))
    dense_spec = pl.BlockSpec((1, SAMPLE_CHUNK_ROWS, DH_A),
                              lambda b, c: (b, dense0 + jnp.clip(c - n_sparse, 0, n_dense - 1), 0))
    return pl.pallas_call(
        functools.partial(_attn_sample_body, heads=heads, dec_seq=dec_seq, n_sparse=n_sparse, n_dense=n_dense),
        grid=(n_seq, n_chunks + 1),
        in_specs=[new_spec(0), sparse_spec, dense_spec, sparse_spec, dense_spec, new_spec(1), new_spec(2),
                  pl.BlockSpec(w_main.shape, lambda b, c: (0, 0, 0)),
                  pl.BlockSpec(w_new.shape, lambda b, c: (0, 0))],
        out_specs=pl.BlockSpec((dec_seq, width), lambda b, c: (b, 0)),
        out_shape=jax.ShapeDtypeStruct((n_seq * dec_seq, width), F32),
        scratch_shapes=[pltpu.VMEM((LANES, DH_A), BF16), pltpu.VMEM((1, LANES), F32), pltpu.VMEM((1, LANES), F32),
                        pltpu.VMEM((DH_A, LANES), F32)],
        compiler_params=_cparams("parallel", "arbitrary"),
        name="attn_sample",
    )(qk, sparse_view(cache_k), dense_view(cache_k), sparse_view(cache_v), dense_view(cache_v), qk, z,
      jnp.asarray(w_main, BF16), jnp.asarray(w_new, BF16))


def _gla_body(q_ref, k_ref, v_ref, r_ref, gl_ref, wg_ref, bg_ref, gn_ref, s0_ref, o_ref, sn_ref, st_scr,
              *, chunk, heads, dk, dv):
    c = pl.program_id(1)
    hs = range(heads)

    @pl.when(c == 0)
    def _():
        for h in hs:
            st_scr[h] = s0_ref[0, h].T

    cp = max(chunk, LANES // 2)

    def pad(a):
        return a if cp == chunk else jnp.concatenate([a, jnp.zeros((cp - chunk, a.shape[1]), a.dtype)], axis=0)

    def cols(ref, h, w):
        return ref[:, h * w:(h + 1) * w]

    ti = lax.broadcasted_iota(jnp.int32, (cp, cp), 0)
    tj = lax.broadcasted_iota(jnp.int32, (cp, cp), 1)
    tril = ti >= tj
    trilf = tril.astype(F32)
    glow = gl_ref[...].astype(BF16)
    log_a = [pad(jax.nn.log_sigmoid(_dot(glow, cols(wg_ref, h, dk)) + cols(bg_ref, h, dk)) / GATE_TAU) for h in hs]
    b = [jnp.dot(trilf, la, preferred_element_type=F32, precision=lax.Precision.HIGHEST) for la in log_a]
    b_last = [x[cp - 1:cp, :] for x in b]
    k = [pad(cols(k_ref, h, dk)) for h in hs]
    v = [pad(cols(v_ref, h, dv)).astype(BF16) for h in hs]
    q_in = [(pad(cols(q_ref, h, dk)) * (dk ** -0.5) * jnp.exp(b[h])).astype(BF16) for h in hs]
    k_in = [(k[h] * jnp.exp(-b[h])).astype(BF16) for h in hs]
    k_dec = [(k[h] * jnp.exp(b_last[h] - b[h])).astype(BF16) for h in hs]
    att = [jnp.where(tril, _dot_nt(q_in[h], k_in[h]), 0.0).astype(BF16) for h in hs]
    st = [st_scr[h] for h in hs]
    o = [(_dot(att[h], v[h]) + _dot_nt(q_in[h], st[h].astype(BF16)))[:chunk] for h in hs]
    st_new = [st[h] * jnp.exp(b_last[h]) + _dot_tn(v[h], k_dec[h]) for h in hs]
    for h in hs:
        st_scr[h] = st_new[h]
        ms = jnp.mean(o[h] * o[h], axis=-1, keepdims=True)
        gated = (o[h] * lax.rsqrt(ms + EPS) * gn_ref[...]) * jax.nn.silu(cols(r_ref, h, dv))
        o_ref[:, h * dv:(h + 1) * dv] = gated.astype(o_ref.dtype)

    @pl.when(c == pl.num_programs(1) - 1)
    def _():
        for h in hs:
            sn_ref[0, h] = st_new[h].T


def _gla(z, zr, zg, w_gate, b_gate, gla_norm, s0, *, row0, chunk, n_chunks, col_q, col_k, col_v, out_dtype):
    n_seq, heads, dk, dv = s0.shape
    rb = row0 // chunk
    wk, wv = heads * dk, heads * dv
    assert col_q % wk == 0 and col_k % wk == 0 and col_v % wv == 0 and zr.shape[1] == wv

    def rows(b, c):
        return rb + b * n_chunks + c

    return pl.pallas_call(
        functools.partial(_gla_body, chunk=chunk, heads=heads, dk=dk, dv=dv),
        grid=(n_seq, n_chunks),
        in_specs=[pl.BlockSpec((chunk, wk), lambda b, c: (rows(b, c), col_q // wk)),
                  pl.BlockSpec((chunk, wk), lambda b, c: (rows(b, c), col_k // wk)),
                  pl.BlockSpec((chunk, wv), lambda b, c: (rows(b, c), col_v // wv)),
                  pl.BlockSpec((chunk, wv), lambda b, c: (rows(b, c), 0)),
                  pl.BlockSpec((chunk, LANES), lambda b, c: (rows(b, c), 0)),
                  pl.BlockSpec((LANES, wk), lambda b, c: (0, 0)),
                  pl.BlockSpec((1, wk), lambda b, c: (0, 0)),
                  pl.BlockSpec((1, dv), lambda b, c: (0, 0)),
                  pl.BlockSpec((1, heads, dk, dv), lambda b, c: (b, 0, 0, 0))],
        out_specs=[pl.BlockSpec((chunk, wv), lambda b, c: (b * n_chunks + c, 0)),
                   pl.BlockSpec((1, heads, dk, dv), lambda b, c: (b, 0, 0, 0))],
        out_shape=[jax.ShapeDtypeStruct((n_seq * n_chunks * chunk, wv), out_dtype),
                   jax.ShapeDtypeStruct((n_seq, heads, dk, dv), F32)],
        scratch_shapes=[pltpu.VMEM((heads, dv, dk), F32)],
        compiler_params=_cparams("parallel", "arbitrary"),
        name="gla",
    )(z, z, z, zr, zg, w_gate, b_gate, gla_norm, s0)


def _mm_out_body(a1p_ref, a1s_ref, a2p_ref, a2s_ref, w1_ref, w2_ref, xp_ref, xs_ref, o_ref, *, n_full, rem):
    i = pl.program_id(0)

    def go(a1, a2, x):
        o_ref[...] = x + _dot(a1, w1_ref[...]) + _dot(a2, w2_ref[...])

    @pl.when(i < n_full)
    def _():
        go(a1p_ref[...], a2p_ref[...], xp_ref[...])

    @pl.when(i == n_full)
    def _():
        def cat(p_ref, s_ref):
            return jnp.concatenate([p_ref[:rem], s_ref[...].astype(p_ref.dtype)], axis=0)

        go(cat(a1p_ref, a1s_ref), cat(a2p_ref, a2s_ref), cat(xp_ref, xs_ref))


def _mm_out(a1p, a1s, a2p, a2s, w, xp, xs, tm, tn):
    t_p, k1 = a1p.shape
    t_s = a1s.shape[0]
    k2 = a2p.shape[1]
    n = w.shape[1]
    n_full, rem = divmod(t_p, tm)
    assert rem > 0 and rem + t_s == tm and k1 == k2 and w.shape[0] == k1 + k2

    def pspec(k):
        return pl.BlockSpec((tm, k), lambda i, j: (i, 0))

    def sspec(k):
        return pl.BlockSpec((t_s, k), lambda i, j: (0, 0))

    return pl.pallas_call(
        functools.partial(_mm_out_body, n_full=n_full, rem=rem),
        grid=(n_full + 1, n // tn),
        in_specs=[pspec(k1), sspec(k1), pspec(k2), sspec(k2),
                  pl.BlockSpec((k1, tn), lambda i, j: (0, j)), pl.BlockSpec((k2, tn), lambda i, j: (1, j)),
                  pl.BlockSpec((tm, tn), lambda i, j: (i, j)), pl.BlockSpec((t_s, tn), lambda i, j: (0, j))],
        out_specs=pl.BlockSpec((tm, tn), lambda i, j: (i, j)),
        out_shape=jax.ShapeDtypeStruct((t_p + t_s, n), F32),
        compiler_params=_cparams("parallel", "arbitrary"),
        name="mm_out",
    )(a1p, a1s, a2p, a2s, w, w, xp, xs)


def _rows_iota(shape):
    return lax.broadcasted_iota(jnp.int32, shape, 0).astype(F32)


def _topk_cols(s, ranks, k):
    big = float(2 ** 20)
    for _ in range(k):
        m = jnp.max(s, axis=0, keepdims=True)
        r = jnp.min(jnp.where(s == m, ranks, big), axis=0, keepdims=True)
        sel = ranks == r
        yield m, r, sel
        s = jnp.where(sel, -jnp.inf, s)


def _stack_rows(rows):
    n = len(rows)
    rid = lax.broadcasted_iota(jnp.int32, (n, rows[0].shape[1]), 0)
    out = jnp.broadcast_to(rows[0], (n, rows[0].shape[1]))
    for i in range(1, n):
        out = jnp.where(rid == i, rows[i], out)
    return out


def _route_body(q_ref, k1_ref, k2_ref, ii_ref, jj_ref, g_ref):
    tt = q_ref.shape[0]
    half = PEER_DKEY // 2
    topk = PEER_TOPK
    key_rank = _rows_iota((N_KEYS, tt))
    r8 = _rows_iota((8, tt))
    r16 = _rows_iota((topk, tt))
    gates, experts = [], []
    for h in range(PEER_HEADS):
        qh = q_ref[:, h * PEER_DKEY:(h + 1) * PEER_DKEY]
        qn = (qh * lax.rsqrt(jnp.mean(qh * qh, axis=-1, keepdims=True) + EPS)).astype(BF16)
        s1 = _dot_nt(k1_ref[h], qn[:, :half])
        s2 = _dot_nt(k2_ref[h], qn[:, half:])
        v1, i1 = zip(*[(m, r) for m, r, _ in _topk_cols(s1, key_rank, topk)])
        v2, i2 = zip(*[(m, r) for m, r, _ in _topk_cols(s2, key_rank, topk)])
        v1t, i1t, v2t, i2t = (_stack_rows(list(x)) for x in (v1, i1, v2, i2))
        cand = [v1[0] + v2t]
        code = [i1[0] * N_KEYS + i2t]
        flat = [r16]
        for a in range(1, 8):
            ok = r8 < float(topk // (a + 1))
            cand.append(jnp.where(ok, v1[a] + v2t[:8], -jnp.inf))
            code.append(i1[a] * N_KEYS + i2t[:8])
            flat.append(r8 + float(a * topk))
        cand.append(v1t[8:] + v2[0])
        code.append(i1t[8:] * N_KEYS + i2[0])
        flat.append((r8 + 8.0) * float(topk))
        cand, code, flat = (jnp.concatenate(x, axis=0) for x in (cand, code, flat))
        top, sel_codes = [], []
        for m, _, sel in _topk_cols(cand, flat, topk):
            top.append(m)
            sel_codes.append(jnp.max(jnp.where(sel, code, -1.0), axis=0, keepdims=True))
        ex = [jnp.exp(m - top[0]) for m in top]
        inv_z = 1.0 / functools.reduce(jnp.add, ex)
        gates.append(_stack_rows([e * inv_z for e in ex]))
        experts.append(_stack_rows(sel_codes))
    g_ref[...] = jnp.concatenate(gates, axis=0).T
    e = jnp.concatenate(experts, axis=0).T.astype(jnp.int32)
    ii_ref[...] = lax.shift_right_logical(e, N_KEYS.bit_length() - 1)
    jj_ref[...] = lax.bitwise_and(e, N_KEYS - 1)


def _route(q, keys1, keys2, tt):
    t, n = q.shape
    no = PEER_HEADS * PEER_TOPK
    kspec = pl.BlockSpec(keys1.shape, lambda i: (0, 0, 0))
    ospec = pl.BlockSpec((tt, no), lambda i: (i, 0))
    return pl.pallas_call(
        _route_body,
        grid=(t // tt,),
        in_specs=[pl.BlockSpec((tt, n), lambda i: (i, 0)), kspec, kspec],
        out_specs=[ospec, ospec, ospec],
        out_shape=[jax.ShapeDtypeStruct((t, no), jnp.int32), jax.ShapeDtypeStruct((t, no), jnp.int32),
                   jax.ShapeDtypeStruct((t, no), F32)],
        compiler_params=_cparams("parallel"),
        name="peer_route",
    )(q, keys1, keys2)


EXPAND_GROUP = 16
EXPAND_PITCH = N_KEYS + 8


def _expand_body(ii_ref, jj_ref, g_ref, o_ref, scr):
    te = ii_ref.shape[0]
    sub = lax.broadcasted_iota(jnp.int32, (N_KEYS, ii_ref.shape[1]), 0)

    def group(gi, carry):
        base = pl.multiple_of(gi * EXPAND_GROUP, EXPAND_GROUP)
        for u in range(EXPAND_GROUP):
            ii = ii_ref[pl.ds(base + u, 1), :]
            jj = jj_ref[pl.ds(base + u, 1), :]
            g = g_ref[pl.ds(base + u, 1), :]
            a = jnp.where(sub == ii, 1.0, 0.0).astype(BF16)
            b = jnp.where(sub == jj, g, 0.0).astype(BF16)
            scr[u * EXPAND_PITCH:u * EXPAND_PITCH + N_KEYS, :] = _dot_nt(a, b)
        for i in range(N_KEYS):
            rows = scr[pl.ds(i, EXPAND_GROUP, stride=EXPAND_PITCH), :]
            o_ref[pl.ds(base, EXPAND_GROUP), i * N_KEYS:(i + 1) * N_KEYS] = rows.astype(o_ref.dtype)
        return carry

    lax.fori_loop(0, te // EXPAND_GROUP, group, 0)


def _expand(ii, jj, g, te):
    t, no = ii.shape
    spec = pl.BlockSpec((te, no), lambda i: (i, 0))
    return pl.pallas_call(
        _expand_body,
        grid=(t // te,),
        in_specs=[spec, spec, spec],
        out_specs=pl.BlockSpec((te, N_KEYS * N_KEYS), lambda i: (i, 0)),
        out_shape=jax.ShapeDtypeStruct((t, N_KEYS * N_KEYS), BF16),
        scratch_shapes=[pltpu.VMEM((EXPAND_GROUP * EXPAND_PITCH, N_KEYS), F32)],
        compiler_params=_cparams("parallel"),
        name="peer_expand",
    )(ii, jj, g)


def _peer_a_body(h_ref, u_ref, g_ref, o_ref):
    a = _dot_nt(h_ref[...], u_ref[...].astype(BF16))
    o_ref[...] = (g_ref[...].astype(F32) * jax.nn.gelu(a)).astype(o_ref.dtype)


def _peer_a(h, u, gates, tm, tn):
    t, k = h.shape
    e = u.shape[0]
    return pl.pallas_call(
        _peer_a_body,
        grid=(t // tm, e // tn),
        in_specs=[pl.BlockSpec((tm, k), lambda i, j: (i, 0)), pl.BlockSpec((tn, k), lambda i, j: (j, 0)),
                  pl.BlockSpec((tm, tn), lambda i, j: (i, j))],
        out_specs=pl.BlockSpec((tm, tn), lambda i, j: (i, j)),
        out_shape=jax.ShapeDtypeStruct((t, e), BF16),
        compiler_params=_cparams("parallel", "arbitrary"),
        name="peer_act",
    )(h, u, gates)


def _peer_v_body(c_ref, v_ref, x_ref, yp_ref, ys_ref, acc_scr, *, n_full, rem):
    i = pl.program_id(0)
    kk = pl.program_id(2)
    last = pl.num_programs(2) - 1

    @pl.when(kk == 0)
    def _():
        acc_scr[...] = x_ref[...]

    acc_scr[...] += _dot(c_ref[...], v_ref[...].astype(BF16))

    @pl.when(kk == last)
    def _():
        yp_ref[...] = acc_scr[...]

    @pl.when((kk == last) & (i == n_full))
    def _():
        ys_ref[...] = acc_scr[rem:, :]


def _peer_v(c, v, x, t_p, tm, tn, tk):
    t, e = c.shape
    n = v.shape[1]
    t_s = t - t_p
    n_full, rem = divmod(t_p, tm)
    assert rem > 0 and rem + t_s == tm
    return pl.pallas_call(
        functools.partial(_peer_v_body, n_full=n_full, rem=rem),
        grid=(t // tm, n // tn, e // tk),
        in_specs=[pl.BlockSpec((tm, tk), lambda i, j, k: (i, k)), pl.BlockSpec((tk, tn), lambda i, j, k: (k, j)),
                  pl.BlockSpec((tm, tn), lambda i, j, k: (i, j))],
        out_specs=[pl.BlockSpec((tm, tn), lambda i, j, k: (i, j)),
                   pl.BlockSpec((t_s, tn), lambda i, j, k: (0, jnp.where(i == n_full, j, 0)))],
        out_shape=[jax.ShapeDtypeStruct((t_p, n), F32), jax.ShapeDtypeStruct((t_s, n), F32)],
        scratch_shapes=[pltpu.VMEM((tm, tn), F32)],
        compiler_params=_cparams("arbitrary", "arbitrary", "arbitrary"),
        name="peer_mix",
    )(c, v, x)


def _token_tile(t, cap):
    best = 16
    for c in range(16, cap + 1, 16):
        if t % c == 0:
            best = c
    return best


def _rope_tables(pos):
    half = DH_A // 2
    inv = ROPE_THETA ** (-jnp.arange(half, dtype=F32) / half)
    ang = pos.astype(F32)[:, None] * inv[None, :]
    cos, sin = jnp.cos(ang), jnp.sin(ang)
    return jnp.concatenate([cos, cos], axis=1), jnp.concatenate([-sin, sin], axis=1)


def kernel(x_prompt, x_sample, cache_attn_k, cache_attn_v, state_gla, norm_mix, w_in, q_norm, k_norm, w_gate, b_gate,
           gla_norm, w_out, norm_ffn, w_peer_q, peer_keys1, peer_keys2, peer_u, peer_v):
    n_p, seq, d_model = x_prompt.shape
    n_s, dec_seq, _ = x_sample.shape
    depth = w_in.shape[0]
    heads_a = cache_attn_k.shape[3]
    w_a = heads_a * DH_A
    _, _, heads_b, dk, dv = state_gla.shape
    w_b = heads_b * dv
    assert seq == DILATED_BRANCHES[-1][0] and heads_b == H_B and cache_attn_k.shape[2] == seq
    t_p = n_p * seq
    t_s = n_s * dec_seq
    t_all = t_p + t_s
    tm = _token_tile(t_all, 1056)
    tt = _token_tile(t_s, 256)

    xp = x_prompt.reshape(t_p, d_model)
    xs = x_sample.reshape(t_s, d_model)
    pos = jnp.concatenate([jnp.tile(jnp.arange(seq, dtype=jnp.int32), n_p),
                           jnp.tile(PAST_LEN + jnp.arange(dec_seq, dtype=jnp.int32), n_s)])
    cos2, sin2 = _rope_tables(pos)

    c_qb = 3 * w_a
    c_kb = c_qb + heads_b * dk
    c_vb = c_kb + heads_b * dk
    c_g = c_vb + w_b
    c_r = c_g + GATE_RANK

    outs = dict(kp=[], vp=[], ks=[], vs=[], sp=[], ss=[])
    for l in range(depth):
        w_r = w_in[l][:, c_r:]
        w_glow = jnp.pad(w_in[l][:, c_g:c_r], ((0, 0), (0, LANES - GATE_RANK)))
        w_gate_p = jnp.pad(w_gate[l].astype(BF16), ((0, LANES - GATE_RANK), (0, 0)))
        qk_gain = jnp.concatenate([jnp.tile(q_norm[l], heads_a), jnp.tile(k_norm[l], heads_a)]).reshape(1, 2 * w_a)

        if l == 0:
            h = _rmsnorm_pair(xp, xs, norm_mix[l], tt)
        else:
            h = _rmsnorm(x, norm_mix[l], tt)
            xp, xs = x[:t_p], x[t_p:]
        z = _matmul(h, w_in[l], tm, 512, n_cols=c_g)
        zr = _matmul(h, w_r, tm, 512)
        zg = _matmul(h, w_glow, tm, LANES)
        qk = _qk_post(z, qk_gain, cos2, sin2, tt, heads_a)

        o_a_p = _attn_prompt(qk, z, n_p, seq, heads_a)
        o_a_s = _attn_sample(qk, z, cache_attn_k[l], cache_attn_v[l], t_p, dec_seq)
        gla_args = (z, zr, zg, w_gate_p, b_gate[l].reshape(1, -1), gla_norm[l].reshape(1, -1))
        gla_cols = dict(col_q=c_qb, col_k=c_kb, col_v=c_vb)
        o_b_p, s_p = _gla(*gla_args, jnp.zeros((n_p, heads_b, dk, dv), F32), row0=0, chunk=GLA_CHUNK,
                          n_chunks=seq // GLA_CHUNK, out_dtype=BF16, **gla_cols)
        o_b_s, s_s = _gla(*gla_args, state_gla[l], row0=t_p, chunk=dec_seq, n_chunks=1, out_dtype=F32, **gla_cols)

        x1 = _mm_out(o_a_p, o_a_s, o_b_p, o_b_s, w_out[l].astype(BF16), xp, xs, tm, 512)

        h2 = _rmsnorm(x1, norm_ffn[l], tt)
        q = _matmul(h2, w_peer_q[l], tm, 512)
        ii, jj, g = _route(q, peer_keys1[l].astype(BF16), peer_keys2[l].astype(BF16), LANES)
        gates = _expand(ii, jj, g, _token_tile(t_all, 128))
        c_act = _peer_a(h2, peer_u[l], gates, tm, 512)
        yp, ys = _peer_v(c_act, peer_v[l].astype(BF16), x1, t_p, tm, 1024, 2048)
        if l + 1 < depth:
            x = jnp.concatenate([yp, ys], axis=0)

        outs["kp"].append(qk[:t_p, w_a:].reshape(n_p, seq, heads_a, DH_A))
        outs["vp"].append(z[:t_p, 2 * w_a:3 * w_a].reshape(n_p, seq, heads_a, DH_A))
        outs["ks"].append(qk[t_p:, w_a:].reshape(n_s, dec_seq, heads_a, DH_A))
        outs["vs"].append(z[t_p:, 2 * w_a:3 * w_a].reshape(n_s, dec_seq, heads_a, DH_A))
        outs["sp"].append(s_p)
        outs["ss"].append(s_s)

    y_prompt = yp.reshape(n_p, seq, d_model)
    y_sample = ys.reshape(n_s, dec_seq, d_model)
    return (y_prompt, y_sample, jnp.stack(outs["kp"]), jnp.stack(outs["vp"]), jnp.stack(outs["ks"]),
            jnp.stack(outs["vs"]), jnp.stack(outs["sp"]), jnp.stack(outs["ss"]))
```

```python
import functools

import numpy as np
import jax
import jax.numpy as jnp
from jax import lax
from jax.experimental import pallas as pl
from jax.experimental.pallas import tpu as pltpu

F32 = jnp.float32
BF16 = jnp.bfloat16

DH_A = 128
DILATED_BRANCHES = ((128, 1), (512, 4), (2048, 16))
ROPE_THETA = 10000.0
PAST_LEN = 8192
H_B = 4
GATE_RANK = 16
GATE_TAU = 16.0
GLA_CHUNK = 64
N_KEYS = 128
PEER_HEADS = 8
PEER_DKEY = 256
PEER_TOPK = 16
EPS = 1e-6
NEG_INF = -1e30

LANES = 128
VMEM_LIMIT = 56 * 1024 * 1024


def _cparams(*sem):
    return pltpu.CompilerParams(dimension_semantics=sem, vmem_limit_bytes=VMEM_LIMIT)


def _dot(a, b):
    return jnp.dot(a, b, preferred_element_type=F32)


def _dot_nt(a, b):
    return lax.dot_general(a, b, (((1,), (1,)), ((), ())), preferred_element_type=F32)


def _dot_tn(a, b):
    return lax.dot_general(a, b, (((0,), (0,)), ((), ())), preferred_element_type=F32)


def _rmsnorm_rows(x, gain):
    ms = jnp.mean(x * x, axis=-1, keepdims=True)
    return (x * lax.rsqrt(ms + EPS) * gain).astype(BF16)


def _rmsnorm_body(x_ref, g_ref, o_ref):
    o_ref[...] = _rmsnorm_rows(x_ref[...], g_ref[...])


def _rmsnorm(x, gain, tt):
    t, d = x.shape
    return pl.pallas_call(
        _rmsnorm_body,
        grid=(t // tt,),
        in_specs=[pl.BlockSpec((tt, d), lambda i: (i, 0)), pl.BlockSpec((1, d), lambda i: (0, 0))],
        out_specs=pl.BlockSpec((tt, d), lambda i: (i, 0)),
        out_shape=jax.ShapeDtypeStruct((t, d), BF16),
        compiler_params=_cparams("parallel"),
        name="rmsnorm",
    )(x, gain.reshape(1, d))


def _rmsnorm_pair_body(xp_ref, xs_ref, g_ref, o_ref, *, n_p):
    i = pl.program_id(0)

    @pl.when(i < n_p)
    def _():
        o_ref[...] = _rmsnorm_rows(xp_ref[...], g_ref[...])

    @pl.when(i >= n_p)
    def _():
        o_ref[...] = _rmsnorm_rows(xs_ref[...], g_ref[...])


def _rmsnorm_pair(xp, xs, gain, tt):
    t_p, d = xp.shape
    t_s = xs.shape[0]
    assert t_p % tt == 0 and t_s % tt == 0
    n_p = t_p // tt
    return pl.pallas_call(
        functools.partial(_rmsnorm_pair_body, n_p=n_p),
        grid=((t_p + t_s) // tt,),
        in_specs=[pl.BlockSpec((tt, d), lambda i: (jnp.minimum(i, n_p - 1), 0)),
                  pl.BlockSpec((tt, d), lambda i: (jnp.maximum(i - n_p, 0), 0)),
                  pl.BlockSpec((1, d), lambda i: (0, 0))],
        out_specs=pl.BlockSpec((tt, d), lambda i: (i, 0)),
        out_shape=jax.ShapeDtypeStruct((t_p + t_s, d), BF16),
        compiler_params=_cparams("arbitrary"),
        name="rmsnorm_in",
    )(xp, xs, gain.reshape(1, d))


def _mm_body(a_ref, w_ref, o_ref, *, w_is_nk):
    w = w_ref[...].astype(BF16)
    o_ref[...] = _dot_nt(a_ref[...], w) if w_is_nk else _dot(a_ref[...], w)


def _matmul(a, w, tm, tn, n_cols=None, w_is_nk=False):
    t, k = a.shape
    n = (w.shape[0] if w_is_nk else w.shape[1]) if n_cols is None else n_cols
    w_spec = pl.BlockSpec((tn, k), lambda i, j: (j, 0)) if w_is_nk else pl.BlockSpec((k, tn), lambda i, j: (0, j))
    return pl.pallas_call(
        functools.partial(_mm_body, w_is_nk=w_is_nk),
        grid=(t // tm, n // tn),
        in_specs=[pl.BlockSpec((tm, k), lambda i, j: (i, 0)), w_spec],
        out_specs=pl.BlockSpec((tm, tn), lambda i, j: (i, j)),
        out_shape=jax.ShapeDtypeStruct((t, n), F32),
        compiler_params=_cparams("parallel", "arbitrary"),
        name="matmul",
    )(a, w)


def _qk_post_body(z_ref, g_ref, cos_ref, sin_ref, o_ref, *, heads):
    cos = cos_ref[...]
    sin = sin_ref[...]
    for h in range(heads):
        sl = slice(h * DH_A, (h + 1) * DH_A)
        x = z_ref[:, sl]
        ms = jnp.mean(x * x, axis=-1, keepdims=True)
        y = x * lax.rsqrt(ms + EPS) * g_ref[:, sl]
        o_ref[:, sl] = y * cos + pltpu.roll(y, DH_A // 2, axis=1) * sin


def _qk_post(z, gains, cos2, sin2, tt, heads):
    t = z.shape[0]
    n = gains.shape[1]
    bw = heads * DH_A
    return pl.pallas_call(
        functools.partial(_qk_post_body, heads=heads),
        grid=(t // tt, n // bw),
        in_specs=[pl.BlockSpec((tt, bw), lambda i, j: (i, j)),
                  pl.BlockSpec((1, bw), lambda i, j: (0, j)),
                  pl.BlockSpec((tt, DH_A), lambda i, j: (i, 0)),
                  pl.BlockSpec((tt, DH_A), lambda i, j: (i, 0))],
        out_specs=pl.BlockSpec((tt, bw), lambda i, j: (i, j)),
        out_shape=jax.ShapeDtypeStruct((t, n), F32),
        compiler_params=_cparams("parallel", "arbitrary"),
        name="qk_post",
    )(z, gains, cos2, sin2)


ATTN_GROUP = 8


def _attn_prompt_body(q_ref, k_ref, v_ref, *rest, seq, n_side):
    side_in, o_ref, side_out, scr = rest[:n_side], rest[n_side], rest[n_side + 1:2 * n_side + 1], rest[2 * n_side + 1:]
    for src, dst in zip(side_in, side_out):
        dst[...] = src[...].astype(dst.dtype)
    blk = DH_A
    scale = DH_A ** -0.5
    n_br = len(DILATED_BRANCHES)
    m_scr, l_scr, acc_scr = scr[:n_br], scr[n_br:2 * n_br], scr[2 * n_br:]
    qi = lax.broadcasted_iota(jnp.int32, (blk, 2 * blk), 0)
    kc = lax.broadcasted_iota(jnp.int32, (blk, 2 * blk), 1)
    band = (kc >= qi) & (kc <= qi + blk)
    cur_half = kc >= blk
    qi1 = lax.broadcasted_iota(jnp.int32, (blk, blk), 0)
    causal = lax.broadcasted_iota(jnp.int32, (blk, blk), 1) <= qi1
    floor_tile = 2.0 * NEG_INF - qi1.astype(F32)

    for bi, (window, d) in enumerate(DILATED_BRANCHES):
        assert window // d == blk
        nb = seq // d // blk

        def blocks(it, carry, bi=bi, d=d, nb=nb):
            rows, vvs, ss = [], [], []
            for u in range(ATTN_GROUP):
                idx = it * ATTN_GROUP + u
                if nb == 1:
                    rw = pl.ds(idx, blk, stride=d)
                    kk = k_ref[rw, :].astype(BF16)
                    vv = v_ref[rw, :].astype(BF16)
                    mask = causal
                else:
                    r = idx // nb
                    j = idx % nb
                    start = r + j * (blk * d)
                    rw = pl.ds(start, blk, stride=d)
                    prows = pl.ds(jnp.maximum(start - blk * d, r), blk, stride=d)
                    kk = jnp.concatenate([k_ref[prows, :], k_ref[rw, :]], axis=0).astype(BF16)
                    vv = jnp.concatenate([v_ref[prows, :], v_ref[rw, :]], axis=0).astype(BF16)
                    mask = band & (cur_half | (j > 0))
                q = (q_ref[rw, :] * scale).astype(BF16)
                rows.append(rw)
                vvs.append(vv)
                ss.append(jnp.where(mask, _dot_nt(q, kk), NEG_INF))
            ms = [jnp.max(s, axis=-1, keepdims=True) for s in ss]
            ps = [jnp.exp(s - m) for s, m in zip(ss, ms)]
            for rw, m, p, vv in zip(rows, ms, ps, vvs):
                m_scr[bi][rw, :] = jnp.maximum(m, floor_tile)
                l_scr[bi][rw, :] = jnp.maximum(jnp.sum(p, axis=-1, keepdims=True), floor_tile)
                acc_scr[bi][rw, :] = _dot(p.astype(BF16), vv)
            return carry

        assert (d * nb) % ATTN_GROUP == 0
        lax.fori_loop(0, d * nb // ATTN_GROUP, blocks, 0)

    def merge(c, carry):
        rows = pl.ds(pl.multiple_of(c * blk, blk), blk)
        ms = [m[rows, :] for m in m_scr]
        m_max = functools.reduce(jnp.maximum, ms)
        ws = [jnp.exp(m - m_max) for m in ms]
        num = functools.reduce(jnp.add, [w * a[rows, :] for w, a in zip(ws, acc_scr)])
        den = functools.reduce(jnp.add, [w * l[rows, :] for w, l in zip(ws, l_scr)])
        o_ref[rows, :] = (num / den).astype(o_ref.dtype)
        return carry

    lax.fori_loop(0, seq // blk, merge, 0)


def _attn_prompt(qk, z, sides, n_seq, seq, heads):
    steps = n_seq * heads
    assert all(s.shape[0] % steps == 0 for s in sides)
    side_specs = [pl.BlockSpec((s.shape[0] // steps, s.shape[1]), lambda b, h: (b * heads + h, 0)) for s in sides]
    outs = pl.pallas_call(
        functools.partial(_attn_prompt_body, seq=seq, n_side=len(sides)),
        grid=(n_seq, heads),
        in_specs=[pl.BlockSpec((seq, DH_A), lambda b, h: (b, h)),
                  pl.BlockSpec((seq, DH_A), lambda b, h: (b, heads + h)),
                  pl.BlockSpec((seq, DH_A), lambda b, h: (b, 2 * heads + h))] + side_specs,
        out_specs=[pl.BlockSpec((seq, DH_A), lambda b, h: (b, h))] + side_specs,
        out_shape=[jax.ShapeDtypeStruct((n_seq * seq, heads * DH_A), BF16)]
        + [jax.ShapeDtypeStruct(s.shape, BF16) for s in sides],
        scratch_shapes=[pltpu.VMEM((seq, DH_A), F32)] * (3 * len(DILATED_BRANCHES)),
        compiler_params=_cparams("parallel", "arbitrary"),
        name="attn_prompt",
    )(qk, qk, z, *sides)
    return outs[0], outs[1:]


SAMPLE_CHUNK_ROWS = 4096


def _sample_plan(win_buf, dec_seq, heads):
    dil = DILATED_BRANCHES[-1][1]
    dense_from = win_buf - DILATED_BRANCHES[-2][0]
    assert win_buf % dil == 0 and dense_from % dil == 0 and dec_seq * 2 == dil and dec_seq * heads == LANES
    pos_sparse = SAMPLE_CHUNK_ROWS // (dec_seq * heads) * dil
    pos_dense = SAMPLE_CHUNK_ROWS // heads
    assert dense_from % pos_sparse == 0 and (win_buf - dense_from) % pos_dense == 0 and dense_from % pos_dense == 0
    return dil, dense_from, dense_from // pos_sparse, (win_buf - dense_from) // pos_dense


def _sample_weights(win_buf, dec_seq, heads):
    dil, dense_from, n_sparse, n_dense = _sample_plan(win_buf, dec_seq, heads)
    n = np.arange(win_buf + dec_seq)[None, :]
    s = np.arange(dec_seq)[:, None]
    dist = win_buf + s - n
    cnt = np.zeros((dec_seq, win_buf + dec_seq), np.float32)
    for window, d in DILATED_BRANCHES:
        cnt += (dist >= 0) & (dist <= window) & (dist % d == 0)
    assert not cnt[:, :dense_from].reshape(dec_seq, -1, dil)[:, :, dec_seq:].any()
    eye = np.eye(heads, dtype=np.float32)
    sparse = cnt[:, :dense_from].reshape(dec_seq, -1, dil)[:, :, :dec_seq].reshape(dec_seq, -1)
    sparse = np.einsum("sn,hg->nhgs", sparse, eye).reshape(n_sparse, SAMPLE_CHUNK_ROWS, LANES)
    dense = np.einsum("sn,hg->nhgs", cnt[:, dense_from:win_buf], eye).reshape(n_dense, SAMPLE_CHUNK_ROWS, LANES)
    new = np.einsum("sn,hg->hngs", cnt[:, win_buf:], eye).reshape(LANES, LANES)
    return np.concatenate([sparse, dense], axis=0), new


def _attn_sample_body(q_ref, ks_ref, kd_ref, vs_ref, vd_ref, kn_ref, vn_ref, w_ref, wn_ref, o_ref,
                      q_scr, m_scr, l_scr, acc_scr, *, heads, dec_seq, n_sparse, n_dense):
    c = pl.program_id(1)
    n_chunks = n_sparse + n_dense

    def by_head(ref):
        return jnp.concatenate([ref[:, h * DH_A:(h + 1) * DH_A] for h in range(heads)], axis=0)

    @pl.when(c == 0)
    def _():
        q_scr[...] = (by_head(q_ref) * (DH_A ** -0.5)).astype(BF16)
        m_scr[...] = jnp.full(m_scr.shape, NEG_INF, F32)
        l_scr[...] = jnp.zeros(l_scr.shape, F32)
        acc_scr[...] = jnp.zeros(acc_scr.shape, F32)

    def step(kb, vb, w):
        s = jnp.where(w > 0, _dot_nt(kb, q_scr[...]), NEG_INF)
        m_old = m_scr[...]
        m_new = jnp.maximum(m_old, jnp.max(s, axis=0, keepdims=True))
        alpha = jnp.exp(m_old - m_new)
        p = w * jnp.exp(s - m_new)
        l_scr[...] = alpha * l_scr[...] + jnp.sum(p, axis=0, keepdims=True)
        acc_scr[...] = alpha * acc_scr[...] + _dot_tn(vb, p.astype(BF16))
        m_scr[...] = m_new

    @pl.when(c < n_sparse)
    def _():
        rows = (SAMPLE_CHUNK_ROWS, DH_A)
        step(ks_ref[0].reshape(rows).astype(BF16), vs_ref[0].reshape(rows).astype(BF16), w_ref[c].astype(F32))

    @pl.when((c >= n_sparse) & (c < n_chunks))
    def _():
        step(kd_ref[0].astype(BF16), vd_ref[0].astype(BF16), w_ref[c].astype(F32))

    @pl.when(c == n_chunks)
    def _():
        step(by_head(kn_ref).astype(BF16), by_head(vn_ref).astype(BF16), wn_ref[...].astype(F32))
        out = (acc_scr[...] / l_scr[...]).T
        for h in range(heads):
            o_ref[:, h * DH_A:(h + 1) * DH_A] = out[h * dec_seq:(h + 1) * dec_seq, :]


def _attn_sample(qk, z, cache_k, cache_v, row0, dec_seq):
    n_seq, win_buf, heads, _ = cache_k.shape
    width = heads * DH_A
    dil, dense_from, n_sparse, n_dense = _sample_plan(win_buf, dec_seq, heads)
    n_chunks = n_sparse + n_dense
    w_main, w_new = _sample_weights(win_buf, dec_seq, heads)
    rb = row0 // dec_seq
    groups = SAMPLE_CHUNK_ROWS // LANES
    dense0 = dense_from * heads // SAMPLE_CHUNK_ROWS

    def new_spec(col):
        return pl.BlockSpec((dec_seq, width), lambda b, c: (rb + b, col))

    sparse_view = lambda x: x.reshape(n_seq, win_buf // dil, dil * heads, DH_A)
    dense_view = lambda x: x.reshape(n_seq, win_buf * heads, DH_A)
    sparse_spec = pl.BlockSpec((1, groups, LANES, DH_A), lambda b, c: (b, jnp.minimum(c, n_sparse - 1), 0, 0))
    dense_spec = pl.BlockSpec((1, SAMPLE_CHUNK_ROWS, DH_A),
                              lambda b, c: (b, dense0 + jnp.clip(c - n_sparse, 0, n_dense - 1), 0))
    return pl.pallas_call(
        functools.partial(_attn_sample_body, heads=heads, dec_seq=dec_seq, n_sparse=n_sparse, n_dense=n_dense),
        grid=(n_seq, n_chunks + 1),
        in_specs=[new_spec(0), sparse_spec, dense_spec, sparse_spec, dense_spec, new_spec(1), new_spec(2),
                  pl.BlockSpec(w_main.shape, lambda b, c: (0, 0, 0)),
                  pl.BlockSpec(w_new.shape, lambda b, c: (0, 0))],
        out_specs=pl.BlockSpec((dec_seq, width), lambda b, c: (b, 0)),
        out_shape=jax.ShapeDtypeStruct((n_seq * dec_seq, width), F32),
        scratch_shapes=[pltpu.VMEM((LANES, DH_A), BF16), pltpu.VMEM((1, LANES), F32), pltpu.VMEM((1, LANES), F32),
                        pltpu.VMEM((DH_A, LANES), F32)],
        compiler_params=_cparams("parallel", "arbitrary"),
        name="attn_sample",
    )(qk, sparse_view(cache_k), dense_view(cache_k), sparse_view(cache_v), dense_view(cache_v), qk, z,
      jnp.asarray(w_main, BF16), jnp.asarray(w_new, BF16))


def _gla_body(q_ref, k_ref, v_ref, r_ref, gl_ref, wg_ref, bg_ref, gn_ref, s0_ref, o_ref, sn_ref, st_scr,
              *, chunk, heads, dk, dv):
    c = pl.program_id(1)
    hs = range(heads)

    @pl.when(c == 0)
    def _():
        for h in hs:
            st_scr[h] = s0_ref[0, h].T

    cp = max(chunk, LANES // 2)

    def pad(a):
        return a if cp == chunk else jnp.concatenate([a, jnp.zeros((cp - chunk, a.shape[1]), a.dtype)], axis=0)

    def cols(ref, h, w):
        return ref[:, h * w:(h + 1) * w]

    ti = lax.broadcasted_iota(jnp.int32, (cp, cp), 0)
    tj = lax.broadcasted_iota(jnp.int32, (cp, cp), 1)
    tril = ti >= tj
    trilf = tril.astype(F32)
    glow = gl_ref[...].astype(BF16)
    log_a = [pad(jax.nn.log_sigmoid(_dot(glow, cols(wg_ref, h, dk)) + cols(bg_ref, h, dk)) / GATE_TAU) for h in hs]
    b = [jnp.dot(trilf, la, preferred_element_type=F32, precision=lax.Precision.HIGHEST) for la in log_a]
    b_last = [x[cp - 1:cp, :] for x in b]
    k = [pad(cols(k_ref, h, dk)) for h in hs]
    v = [pad(cols(v_ref, h, dv)).astype(BF16) for h in hs]
    q_in = [(pad(cols(q_ref, h, dk)) * (dk ** -0.5) * jnp.exp(b[h])).astype(BF16) for h in hs]
    k_in = [(k[h] * jnp.exp(-b[h])).astype(BF16) for h in hs]
    k_dec = [(k[h] * jnp.exp(b_last[h] - b[h])).astype(BF16) for h in hs]
    att = [jnp.where(tril, _dot_nt(q_in[h], k_in[h]), 0.0).astype(BF16) for h in hs]
    st = [st_scr[h] for h in hs]
    o = [(_dot(att[h], v[h]) + _dot_nt(q_in[h], st[h].astype(BF16)))[:chunk] for h in hs]
    st_new = [st[h] * jnp.exp(b_last[h]) + _dot_tn(v[h], k_dec[h]) for h in hs]
    for h in hs:
        st_scr[h] = st_new[h]
        ms = jnp.mean(o[h] * o[h], axis=-1, keepdims=True)
        gated = (o[h] * lax.rsqrt(ms + EPS) * gn_ref[...]) * jax.nn.silu(cols(r_ref, h, dv))
        o_ref[:, h * dv:(h + 1) * dv] = gated.astype(o_ref.dtype)

    @pl.when(c == pl.num_programs(1) - 1)
    def _():
        for h in hs:
            sn_ref[0, h] = st_new[h].T


def _gla(z, zr, zg, w_gate, b_gate, gla_norm, s0, *, row0, chunk, n_chunks, col_q, col_k, col_v, out_dtype):
    n_seq, heads, dk, dv = s0.shape
    rb = row0 // chunk
    wk, wv = heads * dk, heads * dv
    assert col_q % wk == 0 and col_k % wk == 0 and col_v % wv == 0 and zr.shape[1] == wv

    def rows(b, c):
        return rb + b * n_chunks + c

    return pl.pallas_call(
        functools.partial(_gla_body, chunk=chunk, heads=heads, dk=dk, dv=dv),
        grid=(n_seq, n_chunks),
        in_specs=[pl.BlockSpec((chunk, wk), lambda b, c: (rows(b, c), col_q // wk)),
                  pl.BlockSpec((chunk, wk), lambda b, c: (rows(b, c), col_k // wk)),
                  pl.BlockSpec((chunk, wv), lambda b, c: (rows(b, c), col_v // wv)),
                  pl.BlockSpec((chunk, wv), lambda b, c: (rows(b, c), 0)),
                  pl.BlockSpec((chunk, LANES), lambda b, c: (rows(b, c), 0)),
                  pl.BlockSpec((LANES, wk), lambda b, c: (0, 0)),
                  pl.BlockSpec((1, wk), lambda b, c: (0, 0)),
                  pl.BlockSpec((1, dv), lambda b, c: (0, 0)),
                  pl.BlockSpec((1, heads, dk, dv), lambda b, c: (b, 0, 0, 0))],
        out_specs=[pl.BlockSpec((chunk, wv), lambda b, c: (b * n_chunks + c, 0)),
                   pl.BlockSpec((1, heads, dk, dv), lambda b, c: (b, 0, 0, 0))],
        out_shape=[jax.ShapeDtypeStruct((n_seq * n_chunks * chunk, wv), out_dtype),
                   jax.ShapeDtypeStruct((n_seq, heads, dk, dv), F32)],
        scratch_shapes=[pltpu.VMEM((heads, dv, dk), F32)],
        compiler_params=_cparams("parallel", "arbitrary"),
        name="gla",
    )(z, z, z, zr, zg, w_gate, b_gate, gla_norm, s0)


def _mm_out_body(a1p_ref, a1s_ref, a2p_ref, a2s_ref, w1_ref, w2_ref, xp_ref, xs_ref, o_ref, *, n_full, rem):
    i = pl.program_id(0)

    def go(a1, a2, x):
        o_ref[...] = x + _dot(a1, w1_ref[...]) + _dot(a2, w2_ref[...])

    @pl.when(i < n_full)
    def _():
        go(a1p_ref[...], a2p_ref[...], xp_ref[...])

    @pl.when(i == n_full)
    def _():
        def cat(p_ref, s_ref):
            return jnp.concatenate([p_ref[:rem], s_ref[...].astype(p_ref.dtype)], axis=0)

        go(cat(a1p_ref, a1s_ref), cat(a2p_ref, a2s_ref), cat(xp_ref, xs_ref))


def _mm_out(a1p, a1s, a2p, a2s, w, xp, xs, tm, tn):
    t_p, k1 = a1p.shape
    t_s = a1s.shape[0]
    k2 = a2p.shape[1]
    n = w.shape[1]
    n_full, rem = divmod(t_p, tm)
    assert rem > 0 and rem + t_s == tm and k1 == k2 and w.shape[0] == k1 + k2

    def pspec(k):
        return pl.BlockSpec((tm, k), lambda i, j: (i, 0))

    def sspec(k):
        return pl.BlockSpec((t_s, k), lambda i, j: (0, 0))

    return pl.pallas_call(
        functools.partial(_mm_out_body, n_full=n_full, rem=rem),
        grid=(n_full + 1, n // tn),
        in_specs=[pspec(k1), sspec(k1), pspec(k2), sspec(k2),
                  pl.BlockSpec((k1, tn), lambda i, j: (0, j)), pl.BlockSpec((k2, tn), lambda i, j: (1, j)),
                  pl.BlockSpec((tm, tn), lambda i, j: (i, j)), pl.BlockSpec((t_s, tn), lambda i, j: (0, j))],
        out_specs=pl.BlockSpec((tm, tn), lambda i, j: (i, j)),
        out_shape=jax.ShapeDtypeStruct((t_p + t_s, n), F32),
        compiler_params=_cparams("parallel", "arbitrary"),
        name="mm_out",
    )(a1p, a1s, a2p, a2s, w, w, xp, xs)


def _rows_iota(shape):
    return lax.broadcasted_iota(jnp.int32, shape, 0).astype(F32)


def _topk_cols(s, ranks, k):
    big = float(2 ** 20)
    for _ in range(k):
        m = jnp.max(s, axis=0, keepdims=True)
        r = jnp.min(jnp.where(s == m, ranks, big), axis=0, keepdims=True)
        sel = ranks == r
        yield m, r, sel
        s = jnp.where(sel, -jnp.inf, s)


def _stack_rows(rows):
    n = len(rows)
    rid = lax.broadcasted_iota(jnp.int32, (n, rows[0].shape[1]), 0)
    out = jnp.broadcast_to(rows[0], (n, rows[0].shape[1]))
    for i in range(1, n):
        out = jnp.where(rid == i, rows[i], out)
    return out


def _route_body(q_ref, k1_ref, k2_ref, ii_ref, jj_ref, g_ref):
    tt = q_ref.shape[0]
    half = PEER_DKEY // 2
    topk = PEER_TOPK
    key_rank = _rows_iota((N_KEYS, tt))
    r8 = _rows_iota((8, tt))
    r16 = _rows_iota((topk, tt))
    gates, experts = [], []
    for h in range(PEER_HEADS):
        qh = q_ref[:, h * PEER_DKEY:(h + 1) * PEER_DKEY]
        qn = (qh * lax.rsqrt(jnp.mean(qh * qh, axis=-1, keepdims=True) + EPS)).astype(BF16)
        s1 = _dot_nt(k1_ref[h], qn[:, :half])
        s2 = _dot_nt(k2_ref[h], qn[:, half:])
        v1, i1 = zip(*[(m, r) for m, r, _ in _topk_cols(s1, key_rank, topk)])
        v2, i2 = zip(*[(m, r) for m, r, _ in _topk_cols(s2, key_rank, topk)])
        v1t, i1t, v2t, i2t = (_stack_rows(list(x)) for x in (v1, i1, v2, i2))
        cand = [v1[0] + v2t]
        code = [i1[0] * N_KEYS + i2t]
        flat = [r16]
        for a in range(1, 8):
            ok = r8 < float(topk // (a + 1))
            cand.append(jnp.where(ok, v1[a] + v2t[:8], -jnp.inf))
            code.append(i1[a] * N_KEYS + i2t[:8])
            flat.append(r8 + float(a * topk))
        cand.append(v1t[8:] + v2[0])
        code.append(i1t[8:] * N_KEYS + i2[0])
        flat.append((r8 + 8.0) * float(topk))
        cand, code, flat = (jnp.concatenate(x, axis=0) for x in (cand, code, flat))
        top, sel_codes = [], []
        for m, _, sel in _topk_cols(cand, flat, topk):
            top.append(m)
            sel_codes.append(jnp.max(jnp.where(sel, code, -1.0), axis=0, keepdims=True))
        ex = [jnp.exp(m - top[0]) for m in top]
        inv_z = 1.0 / functools.reduce(jnp.add, ex)
        gates.append(_stack_rows([e * inv_z for e in ex]))
        experts.append(_stack_rows(sel_codes))
    g_ref[...] = jnp.concatenate(gates, axis=0).T
    e = jnp.concatenate(experts, axis=0).T.astype(jnp.int32)
    ii_ref[...] = lax.shift_right_logical(e, N_KEYS.bit_length() - 1)
    jj_ref[...] = lax.bitwise_and(e, N_KEYS - 1)


def _route(q, keys1, keys2, tt):
    t, n = q.shape
    no = PEER_HEADS * PEER_TOPK
    kspec = pl.BlockSpec(keys1.shape, lambda i: (0, 0, 0))
    ospec = pl.BlockSpec((tt, no), lambda i: (i, 0))
    return pl.pallas_call(
        _route_body,
        grid=(t // tt,),
        in_specs=[pl.BlockSpec((tt, n), lambda i: (i, 0)), kspec, kspec],
        out_specs=[ospec, ospec, ospec],
        out_shape=[jax.ShapeDtypeStruct((t, no), jnp.int32), jax.ShapeDtypeStruct((t, no), jnp.int32),
                   jax.ShapeDtypeStruct((t, no), F32)],
        compiler_params=_cparams("parallel"),
        name="peer_route",
    )(q, keys1, keys2)


EXPAND_GROUP = 16
EXPAND_PITCH = N_KEYS + 8


def _expand_body(ii_ref, jj_ref, g_ref, o_ref, scr):
    te = ii_ref.shape[0]
    sub = lax.broadcasted_iota(jnp.int32, (N_KEYS, ii_ref.shape[1]), 0)

    def group(gi, carry):
        base = pl.multiple_of(gi * EXPAND_GROUP, EXPAND_GROUP)
        for u in range(EXPAND_GROUP):
            ii = ii_ref[pl.ds(base + u, 1), :]
            jj = jj_ref[pl.ds(base + u, 1), :]
            g = g_ref[pl.ds(base + u, 1), :]
            a = jnp.where(sub == ii, 1.0, 0.0).astype(BF16)
            b = jnp.where(sub == jj, g, 0.0).astype(BF16)
            scr[u * EXPAND_PITCH:u * EXPAND_PITCH + N_KEYS, :] = _dot_nt(a, b)
        for i in range(N_KEYS):
            rows = scr[pl.ds(i, EXPAND_GROUP, stride=EXPAND_PITCH), :]
            o_ref[pl.ds(base, EXPAND_GROUP), i * N_KEYS:(i + 1) * N_KEYS] = rows.astype(o_ref.dtype)
        return carry

    lax.fori_loop(0, te // EXPAND_GROUP, group, 0)


def _expand(ii, jj, g, te):
    t, no = ii.shape
    spec = pl.BlockSpec((te, no), lambda i: (i, 0))
    return pl.pallas_call(
        _expand_body,
        grid=(t // te,),
        in_specs=[spec, spec, spec],
        out_specs=pl.BlockSpec((te, N_KEYS * N_KEYS), lambda i: (i, 0)),
        out_shape=jax.ShapeDtypeStruct((t, N_KEYS * N_KEYS), BF16),
        scratch_shapes=[pltpu.VMEM((EXPAND_GROUP * EXPAND_PITCH, N_KEYS), F32)],
        compiler_params=_cparams("parallel"),
        name="peer_expand",
    )(ii, jj, g)


PEER_SUB = 512


def _peer_a_body(h_ref, u_ref, g_ref, o_ref):
    h = h_ref[...]
    for s in range(o_ref.shape[1] // PEER_SUB):
        cols = slice(s * PEER_SUB, (s + 1) * PEER_SUB)
        a = _dot_nt(h, u_ref[cols, :])
        o_ref[:, cols] = (g_ref[:, cols].astype(F32) * jax.nn.gelu(a)).astype(o_ref.dtype)


def _peer_a(h, u, gates, tm, tn):
    t, k = h.shape
    e = u.shape[0]
    return pl.pallas_call(
        _peer_a_body,
        grid=(t // tm, e // tn),
        in_specs=[pl.BlockSpec((tm, k), lambda i, j: (i, 0)), pl.BlockSpec((tn, k), lambda i, j: (j, 0)),
                  pl.BlockSpec((tm, tn), lambda i, j: (i, j))],
        out_specs=pl.BlockSpec((tm, tn), lambda i, j: (i, j)),
        out_shape=jax.ShapeDtypeStruct((t, e), BF16),
        compiler_params=_cparams("parallel", "arbitrary"),
        name="peer_act",
    )(h, u, gates)


def _peer_v_body(c_ref, v_ref, x_ref, yp_ref, ys_ref, acc_scr, *, n_full, rem):
    i = pl.program_id(0)
    kk = pl.program_id(2)
    last = pl.num_programs(2) - 1

    @pl.when(kk == 0)
    def _():
        acc_scr[...] = x_ref[...]

    acc_scr[...] += _dot(c_ref[...], v_ref[...].astype(BF16))

    @pl.when(kk == last)
    def _():
        yp_ref[...] = acc_scr[...]

    @pl.when((kk == last) & (i == n_full))
    def _():
        ys_ref[...] = acc_scr[rem:, :]


def _peer_v(c, v, x, t_p, tm, tn, tk):
    t, e = c.shape
    n = v.shape[1]
    t_s = t - t_p
    n_full, rem = divmod(t_p, tm)
    assert rem > 0 and rem + t_s == tm
    return pl.pallas_call(
        functools.partial(_peer_v_body, n_full=n_full, rem=rem),
        grid=(t // tm, n // tn, e // tk),
        in_specs=[pl.BlockSpec((tm, tk), lambda i, j, k: (i, k)), pl.BlockSpec((tk, tn), lambda i, j, k: (k, j)),
                  pl.BlockSpec((tm, tn), lambda i, j, k: (i, j))],
        out_specs=[pl.BlockSpec((tm, tn), lambda i, j, k: (i, j)),
                   pl.BlockSpec((t_s, tn), lambda i, j, k: (0, jnp.where(i == n_full, j, 0)))],
        out_shape=[jax.ShapeDtypeStruct((t_p, n), F32), jax.ShapeDtypeStruct((t_s, n), F32)],
        scratch_shapes=[pltpu.VMEM((tm, tn), F32)],
        compiler_params=_cparams("arbitrary", "arbitrary", "arbitrary"),
        name="peer_mix",
    )(c, v, x)


def _token_tile(t, cap):
    best = 16
    for c in range(16, cap + 1, 16):
        if t % c == 0:
            best = c
    return best


def _rope_tables(pos):
    half = DH_A // 2
    inv = ROPE_THETA ** (-jnp.arange(half, dtype=F32) / half)
    ang = pos.astype(F32)[:, None] * inv[None, :]
    cos, sin = jnp.cos(ang), jnp.sin(ang)
    return jnp.concatenate([cos, cos], axis=1), jnp.concatenate([-sin, sin], axis=1)


def kernel(x_prompt, x_sample, cache_attn_k, cache_attn_v, state_gla, norm_mix, w_in, q_norm, k_norm, w_gate, b_gate,
           gla_norm, w_out, norm_ffn, w_peer_q, peer_keys1, peer_keys2, peer_u, peer_v):
    n_p, seq, d_model = x_prompt.shape
    n_s, dec_seq, _ = x_sample.shape
    depth = w_in.shape[0]
    heads_a = cache_attn_k.shape[3]
    w_a = heads_a * DH_A
    _, _, heads_b, dk, dv = state_gla.shape
    w_b = heads_b * dv
    assert seq == DILATED_BRANCHES[-1][0] and heads_b == H_B and cache_attn_k.shape[2] == seq
    t_p = n_p * seq
    t_s = n_s * dec_seq
    t_all = t_p + t_s
    tm = _token_tile(t_all, 1056)
    tt = _token_tile(t_s, 256)

    xp = x_prompt.reshape(t_p, d_model)
    xs = x_sample.reshape(t_s, d_model)
    pos = jnp.concatenate([jnp.tile(jnp.arange(seq, dtype=jnp.int32), n_p),
                           jnp.tile(PAST_LEN + jnp.arange(dec_seq, dtype=jnp.int32), n_s)])
    cos2, sin2 = _rope_tables(pos)

    c_qb = 3 * w_a
    c_kb = c_qb + heads_b * dk
    c_vb = c_kb + heads_b * dk
    c_g = c_vb + w_b
    c_r = c_g + GATE_RANK

    outs = dict(kp=[], vp=[], ks=[], vs=[], sp=[], ss=[])
    for l in range(depth):
        w_in_t = jnp.swapaxes(w_in[l], 0, 1)
        w_r = w_in_t[c_r:]
        w_glow = jnp.pad(w_in_t[c_g:c_r], ((0, LANES - GATE_RANK), (0, 0)))
        w_gate_p = jnp.pad(w_gate[l].astype(BF16), ((0, LANES - GATE_RANK), (0, 0)))
        qk_gain = jnp.concatenate([jnp.tile(q_norm[l], heads_a), jnp.tile(k_norm[l], heads_a)]).reshape(1, 2 * w_a)

        if l == 0:
            h = _rmsnorm_pair(xp, xs, norm_mix[l], tt)
        else:
            h = _rmsnorm(x, norm_mix[l], tt)
            xp, xs = x[:t_p], x[t_p:]
        z = _matmul(h, w_in_t, tm, 512, n_cols=c_g, w_is_nk=True)
        zr = _matmul(h, w_r, tm, 512, w_is_nk=True)
        zg = _matmul(h, w_glow, tm, LANES, w_is_nk=True)
        qk = _qk_post(z, qk_gain, cos2, sin2, tt, heads_a)

        o_a_p, (peer_u_bf16, peer_v_bf16) = _attn_prompt(qk, z, (peer_u[l], peer_v[l]), n_p, seq, heads_a)
        o_a_s = _attn_sample(qk, z, cache_attn_k[l], cache_attn_v[l], t_p, dec_seq)
        gla_args = (z, zr, zg, w_gate_p, b_gate[l].reshape(1, -1), gla_norm[l].reshape(1, -1))
        gla_cols = dict(col_q=c_qb, col_k=c_kb, col_v=c_vb)
        o_b_p, s_p = _gla(*gla_args, jnp.zeros((n_p, heads_b, dk, dv), F32), row0=0, chunk=GLA_CHUNK,
                          n_chunks=seq // GLA_CHUNK, out_dtype=BF16, **gla_cols)
        o_b_s, s_s = _gla(*gla_args, state_gla[l], row0=t_p, chunk=dec_seq, n_chunks=1, out_dtype=F32, **gla_cols)

        x1 = _mm_out(o_a_p, o_a_s, o_b_p, o_b_s, w_out[l].astype(BF16), xp, xs, tm, 512)

        h2 = _rmsnorm(x1, norm_ffn[l], tt)
        q = _matmul(h2, w_peer_q[l], tm, 512)
        ii, jj, g = _route(q, peer_keys1[l].astype(BF16), peer_keys2[l].astype(BF16), LANES)
        gates = _expand(ii, jj, g, _token_tile(t_all, 128))
        c_act = _peer_a(h2, peer_u_bf16, gates, tm, 2 * PEER_SUB)
        yp, ys = _peer_v(c_act, peer_v_bf16, x1, t_p, tm, 1024, 2048)
        if l + 1 < depth:
            x = jnp.concatenate([yp, ys], axis=0)

        outs["kp"].append(qk[:t_p, w_a:].reshape(n_p, seq, heads_a, DH_A))
        outs["vp"].append(z[:t_p, 2 * w_a:3 * w_a].reshape(n_p, seq, heads_a, DH_A))
        outs["ks"].append(qk[t_p:, w_a:].reshape(n_s, dec_seq, heads_a, DH_A))
        outs["vs"].append(z[t_p:, 2 * w_a:3 * w_a].reshape(n_s, dec_seq, heads_a, DH_A))
        outs["sp"].append(s_p)
        outs["ss"].append(s_s)

    y_prompt = yp.reshape(n_p, seq, d_model)
    y_sample = ys.reshape(n_s, dec_seq, d_model)
    return (y_prompt, y_sample, jnp.stack(outs["kp"]), jnp.stack(outs["vp"]), jnp.stack(outs["ks"]),
            jnp.stack(outs["vs"]), jnp.stack(outs["sp"]), jnp.stack(outs["ss"]))
```

```python
import functools

import numpy as np
import jax
import jax.numpy as jnp
from jax import lax
from jax.experimental import pallas as pl
from jax.experimental.pallas import tpu as pltpu

F32 = jnp.float32
BF16 = jnp.bfloat16

DH_A = 128
DILATED_BRANCHES = ((128, 1), (512, 4), (2048, 16))
ROPE_THETA = 10000.0
PAST_LEN = 8192
H_B = 4
GATE_RANK = 16
GATE_TAU = 16.0
GLA_CHUNK = 64
N_KEYS = 128
PEER_HEADS = 8
PEER_DKEY = 256
PEER_TOPK = 16
EPS = 1e-6
NEG_INF = -1e30

LANES = 128
VMEM_LIMIT = 56 * 1024 * 1024


def _cparams(*sem):
    return pltpu.CompilerParams(dimension_semantics=sem, vmem_limit_bytes=VMEM_LIMIT)


def _dot(a, b):
    return jnp.dot(a, b, preferred_element_type=F32)


def _dot_nt(a, b):
    return lax.dot_general(a, b, (((1,), (1,)), ((), ())), preferred_element_type=F32)


def _dot_tn(a, b):
    return lax.dot_general(a, b, (((0,), (0,)), ((), ())), preferred_element_type=F32)


def _rmsnorm_rows(x, gain):
    ms = jnp.mean(x * x, axis=-1, keepdims=True)
    return (x * lax.rsqrt(ms + EPS) * gain).astype(BF16)


def _rmsnorm_body(x_ref, g_ref, o_ref):
    o_ref[...] = _rmsnorm_rows(x_ref[...], g_ref[...])


def _rmsnorm(x, gain, tt):
    t, d = x.shape
    return pl.pallas_call(
        _rmsnorm_body,
        grid=(t // tt,),
        in_specs=[pl.BlockSpec((tt, d), lambda i: (i, 0)), pl.BlockSpec((1, d), lambda i: (0, 0))],
        out_specs=pl.BlockSpec((tt, d), lambda i: (i, 0)),
        out_shape=jax.ShapeDtypeStruct((t, d), BF16),
        compiler_params=_cparams("parallel"),
        name="rmsnorm",
    )(x, gain.reshape(1, d))


def _rmsnorm_pair_body(xp_ref, xs_ref, g_ref, o_ref, *, n_p):
    i = pl.program_id(0)

    @pl.when(i < n_p)
    def _():
        o_ref[...] = _rmsnorm_rows(xp_ref[...], g_ref[...])

    @pl.when(i >= n_p)
    def _():
        o_ref[...] = _rmsnorm_rows(xs_ref[...], g_ref[...])


def _rmsnorm_pair(xp, xs, gain, tt):
    t_p, d = xp.shape
    t_s = xs.shape[0]
    assert t_p % tt == 0 and t_s % tt == 0
    n_p = t_p // tt
    return pl.pallas_call(
        functools.partial(_rmsnorm_pair_body, n_p=n_p),
        grid=((t_p + t_s) // tt,),
        in_specs=[pl.BlockSpec((tt, d), lambda i: (jnp.minimum(i, n_p - 1), 0)),
                  pl.BlockSpec((tt, d), lambda i: (jnp.maximum(i - n_p, 0), 0)),
                  pl.BlockSpec((1, d), lambda i: (0, 0))],
        out_specs=pl.BlockSpec((tt, d), lambda i: (i, 0)),
        out_shape=jax.ShapeDtypeStruct((t_p + t_s, d), BF16),
        compiler_params=_cparams("arbitrary"),
        name="rmsnorm_in",
    )(xp, xs, gain.reshape(1, d))


def _mm_body(a_ref, w_ref, o_ref, *, w_is_nk):
    w = w_ref[...].astype(BF16)
    o_ref[...] = _dot_nt(a_ref[...], w) if w_is_nk else _dot(a_ref[...], w)


def _matmul(a, w, tm, tn, n_cols=None, w_is_nk=False):
    t, k = a.shape
    n = (w.shape[0] if w_is_nk else w.shape[1]) if n_cols is None else n_cols
    w_spec = pl.BlockSpec((tn, k), lambda i, j: (j, 0)) if w_is_nk else pl.BlockSpec((k, tn), lambda i, j: (0, j))
    return pl.pallas_call(
        functools.partial(_mm_body, w_is_nk=w_is_nk),
        grid=(t // tm, n // tn),
        in_specs=[pl.BlockSpec((tm, k), lambda i, j: (i, 0)), w_spec],
        out_specs=pl.BlockSpec((tm, tn), lambda i, j: (i, j)),
        out_shape=jax.ShapeDtypeStruct((t, n), F32),
        compiler_params=_cparams("parallel", "arbitrary"),
        name="matmul",
    )(a, w)


def _norm_rope(x, gain, cos2, sin2):
    ms = jnp.mean(x * x, axis=-1, keepdims=True)
    y = x * lax.rsqrt(ms + EPS) * gain
    return y * cos2 + pltpu.roll(y, DH_A // 2, axis=1) * sin2


ATTN_GROUP = 8


def _attn_prompt_body(q_ref, k_ref, v_ref, gq_ref, gk_ref, cos_ref, sin_ref, *rest, seq, n_side):
    side_in, o_ref, kr_ref = rest[:n_side], rest[n_side], rest[n_side + 1]
    side_out, scr = rest[n_side + 2:2 * n_side + 2], rest[2 * n_side + 2:]
    for src, dst in zip(side_in, side_out):
        dst[...] = src[...].astype(dst.dtype)
    blk = DH_A
    n_br = len(DILATED_BRANCHES)
    qs_scr, ks_scr, scr = scr[0], scr[1], scr[2:]
    m_scr, l_scr, acc_scr = scr[:n_br], scr[n_br:2 * n_br], scr[2 * n_br:]

    def prep(c, carry):
        rows = pl.ds(pl.multiple_of(c * blk, blk), blk)
        cos2, sin2 = cos_ref[rows, :], sin_ref[rows, :]
        qs_scr[rows, :] = _norm_rope(q_ref[rows, :], gq_ref[...], cos2, sin2) * (DH_A ** -0.5)
        kr = _norm_rope(k_ref[rows, :], gk_ref[...], cos2, sin2)
        ks_scr[rows, :] = kr
        kr_ref[rows, :] = kr
        return carry

    lax.fori_loop(0, seq // blk, prep, 0)
    qi = lax.broadcasted_iota(jnp.int32, (blk, 2 * blk), 0)
    kc = lax.broadcasted_iota(jnp.int32, (blk, 2 * blk), 1)
    band = (kc >= qi) & (kc <= qi + blk)
    cur_half = kc >= blk
    qi1 = lax.broadcasted_iota(jnp.int32, (blk, blk), 0)
    causal = lax.broadcasted_iota(jnp.int32, (blk, blk), 1) <= qi1
    floor_tile = 2.0 * NEG_INF - qi1.astype(F32)

    for bi, (window, d) in enumerate(DILATED_BRANCHES):
        assert window // d == blk
        nb = seq // d // blk

        def blocks(it, carry, bi=bi, d=d, nb=nb):
            rows, vvs, ss = [], [], []
            for u in range(ATTN_GROUP):
                idx = it * ATTN_GROUP + u
                if nb == 1:
                    rw = pl.ds(idx, blk, stride=d)
                    kk = ks_scr[rw, :].astype(BF16)
                    vv = v_ref[rw, :].astype(BF16)
                    mask = causal
                else:
                    r = idx // nb
                    j = idx % nb
                    start = r + j * (blk * d)
                    rw = pl.ds(start, blk, stride=d)
                    prows = pl.ds(jnp.maximum(start - blk * d, r), blk, stride=d)
                    kk = jnp.concatenate([ks_scr[prows, :], ks_scr[rw, :]], axis=0).astype(BF16)
                    vv = jnp.concatenate([v_ref[prows, :], v_ref[rw, :]], axis=0).astype(BF16)
                    mask = band & (cur_half | (j > 0))
                q = qs_scr[rw, :].astype(BF16)
                rows.append(rw)
                vvs.append(vv)
                ss.append(jnp.where(mask, _dot_nt(q, kk), NEG_INF))
            ms = [jnp.max(s, axis=-1, keepdims=True) for s in ss]
            ps = [jnp.exp(s - m) for s, m in zip(ss, ms)]
            for rw, m, p, vv in zip(rows, ms, ps, vvs):
                m_scr[bi][rw, :] = jnp.maximum(m, floor_tile)
                l_scr[bi][rw, :] = jnp.maximum(jnp.sum(p, axis=-1, keepdims=True), floor_tile)
                acc_scr[bi][rw, :] = _dot(p.astype(BF16), vv)
            return carry

        assert (d * nb) % ATTN_GROUP == 0
        lax.fori_loop(0, d * nb // ATTN_GROUP, blocks, 0)

    def merge(c, carry):
        rows = pl.ds(pl.multiple_of(c * blk, blk), blk)
        ms = [m[rows, :] for m in m_scr]
        m_max = functools.reduce(jnp.maximum, ms)
        ws = [jnp.exp(m - m_max) for m in ms]
        num = functools.reduce(jnp.add, [w * a[rows, :] for w, a in zip(ws, acc_scr)])
        den = functools.reduce(jnp.add, [w * l[rows, :] for w, l in zip(ws, l_scr)])
        o_ref[rows, :] = (num / den).astype(o_ref.dtype)
        return carry

    lax.fori_loop(0, seq // blk, merge, 0)


def _attn_prompt(z, q_gain, k_gain, cos2, sin2, sides, n_seq, seq, heads):
    steps = n_seq * heads
    assert all(s.shape[0] % steps == 0 for s in sides)
    side_specs = [pl.BlockSpec((s.shape[0] // steps, s.shape[1]), lambda b, h: (b * heads + h, 0)) for s in sides]
    head_spec = pl.BlockSpec((seq, DH_A), lambda b, h: (b, h))
    vec_spec = pl.BlockSpec((1, DH_A), lambda b, h: (0, 0))
    tab_spec = pl.BlockSpec((seq, DH_A), lambda b, h: (0, 0))
    outs = pl.pallas_call(
        functools.partial(_attn_prompt_body, seq=seq, n_side=len(sides)),
        grid=(n_seq, heads),
        in_specs=[head_spec,
                  pl.BlockSpec((seq, DH_A), lambda b, h: (b, heads + h)),
                  pl.BlockSpec((seq, DH_A), lambda b, h: (b, 2 * heads + h)),
                  vec_spec, vec_spec, tab_spec, tab_spec] + side_specs,
        out_specs=[head_spec, head_spec] + side_specs,
        out_shape=[jax.ShapeDtypeStruct((n_seq * seq, heads * DH_A), BF16),
                   jax.ShapeDtypeStruct((n_seq * seq, heads * DH_A), F32)]
        + [jax.ShapeDtypeStruct(s.shape, BF16) for s in sides],
        scratch_shapes=[pltpu.VMEM((seq, DH_A), F32)] * (2 + 3 * len(DILATED_BRANCHES)),
        compiler_params=_cparams("parallel", "arbitrary"),
        name="attn_prompt",
    )(z, z, z, q_gain.reshape(1, DH_A), k_gain.reshape(1, DH_A), cos2, sin2, *sides)
    return outs[0], outs[1], outs[2:]


SAMPLE_CHUNK_ROWS = 4096


def _sample_plan(win_buf, dec_seq, heads):
    dil = DILATED_BRANCHES[-1][1]
    dense_from = win_buf - DILATED_BRANCHES[-2][0]
    assert win_buf % dil == 0 and dense_from % dil == 0 and dec_seq * 2 == dil and dec_seq * heads == LANES
    pos_sparse = SAMPLE_CHUNK_ROWS // (dec_seq * heads) * dil
    pos_dense = SAMPLE_CHUNK_ROWS // heads
    assert dense_from % pos_sparse == 0 and (win_buf - dense_from) % pos_dense == 0 and dense_from % pos_dense == 0
    return dil, dense_from, dense_from // pos_sparse, (win_buf - dense_from) // pos_dense


def _sample_weights(win_buf, dec_seq, heads):
    dil, dense_from, n_sparse, n_dense = _sample_plan(win_buf, dec_seq, heads)
    n = np.arange(win_buf + dec_seq)[None, :]
    s = np.arange(dec_seq)[:, None]
    dist = win_buf + s - n
    cnt = np.zeros((dec_seq, win_buf + dec_seq), np.float32)
    for window, d in DILATED_BRANCHES:
        cnt += (dist >= 0) & (dist <= window) & (dist % d == 0)
    assert not cnt[:, :dense_from].reshape(dec_seq, -1, dil)[:, :, dec_seq:].any()
    eye = np.eye(heads, dtype=np.float32)
    sparse = cnt[:, :dense_from].reshape(dec_seq, -1, dil)[:, :, :dec_seq].reshape(dec_seq, -1)
    sparse = np.einsum("sn,hg->nhgs", sparse, eye).reshape(n_sparse, SAMPLE_CHUNK_ROWS, LANES)
    dense = np.einsum("sn,hg->nhgs", cnt[:, dense_from:win_buf], eye).reshape(n_dense, SAMPLE_CHUNK_ROWS, LANES)
    new = np.einsum("sn,hg->hngs", cnt[:, win_buf:], eye).reshape(LANES, LANES)
    return np.concatenate([sparse, dense], axis=0), new


def _attn_sample_body(q_ref, ks_ref, kd_ref, vs_ref, vd_ref, kn_ref, vn_ref, w_ref, wn_ref, gq_ref, gk_ref,
                      cos_ref, sin_ref, o_ref, kr_ref, q_scr, m_scr, l_scr, acc_scr,
                      *, heads, dec_seq, n_sparse, n_dense):
    c = pl.program_id(1)
    n_chunks = n_sparse + n_dense

    def by_head(ref, fn=lambda x: x):
        return jnp.concatenate([fn(ref[:, h * DH_A:(h + 1) * DH_A]) for h in range(heads)], axis=0)

    def rotated(gain_ref):
        return lambda x: _norm_rope(x, gain_ref[...], cos_ref[...], sin_ref[...])

    @pl.when(c == 0)
    def _():
        q_scr[...] = (by_head(q_ref, rotated(gq_ref)) * (DH_A ** -0.5)).astype(BF16)
        m_scr[...] = jnp.full(m_scr.shape, NEG_INF, F32)
        l_scr[...] = jnp.zeros(l_scr.shape, F32)
        acc_scr[...] = jnp.zeros(acc_scr.shape, F32)

    def step(kb, vb, w):
        s = jnp.where(w > 0, _dot_nt(kb, q_scr[...]), NEG_INF)
        m_old = m_scr[...]
        m_new = jnp.maximum(m_old, jnp.max(s, axis=0, keepdims=True))
        alpha = jnp.exp(m_old - m_new)
        p = w * jnp.exp(s - m_new)
        l_scr[...] = alpha * l_scr[...] + jnp.sum(p, axis=0, keepdims=True)
        acc_scr[...] = alpha * acc_scr[...] + _dot_tn(vb, p.astype(BF16))
        m_scr[...] = m_new

    @pl.when(c < n_sparse)
    def _():
        rows = (SAMPLE_CHUNK_ROWS, DH_A)
        step(ks_ref[0].reshape(rows).astype(BF16), vs_ref[0].reshape(rows).astype(BF16), w_ref[c].astype(F32))

    @pl.when((c >= n_sparse) & (c < n_chunks))
    def _():
        step(kd_ref[0].astype(BF16), vd_ref[0].astype(BF16), w_ref[c].astype(F32))

    @pl.when(c == n_chunks)
    def _():
        kn = by_head(kn_ref, rotated(gk_ref))
        step(kn.astype(BF16), by_head(vn_ref).astype(BF16), wn_ref[...].astype(F32))
        out = (acc_scr[...] / l_scr[...]).T
        for h in range(heads):
            o_ref[:, h * DH_A:(h + 1) * DH_A] = out[h * dec_seq:(h + 1) * dec_seq, :]
            kr_ref[:, h * DH_A:(h + 1) * DH_A] = kn[h * dec_seq:(h + 1) * dec_seq, :]


def _attn_sample(z, q_gain, k_gain, cos2, sin2, cache_k, cache_v, row0, dec_seq):
    n_seq, win_buf, heads, _ = cache_k.shape
    width = heads * DH_A
    dil, dense_from, n_sparse, n_dense = _sample_plan(win_buf, dec_seq, heads)
    n_chunks = n_sparse + n_dense
    w_main, w_new = _sample_weights(win_buf, dec_seq, heads)
    rb = row0 // dec_seq
    groups = SAMPLE_CHUNK_ROWS // LANES
    dense0 = dense_from * heads // SAMPLE_CHUNK_ROWS

    def new_spec(col):
        return pl.BlockSpec((dec_seq, width), lambda b, c: (rb + b, col))

    def small(shape):
        return pl.BlockSpec(shape, lambda b, c: (0, 0))

    out_spec = pl.BlockSpec((dec_seq, width), lambda b, c: (b, 0))

    sparse_view = lambda x: x.reshape(n_seq, win_buf // dil, dil * heads, DH_A)
    dense_view = lambda x: x.reshape(n_seq, win_buf * heads, DH_A)
    sparse_spec = pl.BlockSpec((1, groups, LANES, DH_A), lambda b, c: (b, jnp.minimum(c, n_sparse - 1), 0, 0))
    dense_spec = pl.BlockSpec((1, SAMPLE_CHUNK_ROWS, DH_A),
                              lambda b, c: (b, dense0 + jnp.clip(c - n_sparse, 0, n_dense - 1), 0))
    return pl.pallas_call(
        functools.partial(_attn_sample_body, heads=heads, dec_seq=dec_seq, n_sparse=n_sparse, n_dense=n_dense),
        grid=(n_seq, n_chunks + 1),
        in_specs=[new_spec(0), sparse_spec, dense_spec, sparse_spec, dense_spec, new_spec(1), new_spec(2),
                  pl.BlockSpec(w_main.shape, lambda b, c: (0, 0, 0)),
                  pl.BlockSpec(w_new.shape, lambda b, c: (0, 0)),
                  small((1, DH_A)), small((1, DH_A)), small((dec_seq, DH_A)), small((dec_seq, DH_A))],
        out_specs=[out_spec, out_spec],
        out_shape=[jax.ShapeDtypeStruct((n_seq * dec_seq, width), F32)] * 2,
        scratch_shapes=[pltpu.VMEM((LANES, DH_A), BF16), pltpu.VMEM((1, LANES), F32), pltpu.VMEM((1, LANES), F32),
                        pltpu.VMEM((DH_A, LANES), F32)],
        compiler_params=_cparams("parallel", "arbitrary"),
        name="attn_sample",
    )(z, sparse_view(cache_k), dense_view(cache_k), sparse_view(cache_v), dense_view(cache_v), z, z,
      jnp.asarray(w_main, BF16), jnp.asarray(w_new, BF16), q_gain.reshape(1, DH_A), k_gain.reshape(1, DH_A),
      cos2, sin2)


def _gla_body(q_ref, k_ref, v_ref, r_ref, gl_ref, wg_ref, bg_ref, gn_ref, s0_ref, o_ref, sn_ref, st_scr,
              *, chunk, heads, dk, dv):
    c = pl.program_id(1)
    hs = range(heads)

    @pl.when(c == 0)
    def _():
        for h in hs:
            st_scr[h] = s0_ref[0, h]

    cp = max(chunk, LANES // 2)

    def pad(a):
        return a if cp == chunk else jnp.concatenate([a, jnp.zeros((cp - chunk, a.shape[1]), a.dtype)], axis=0)

    def cols(ref, h, w):
        return ref[:, h * w:(h + 1) * w]

    ti = lax.broadcasted_iota(jnp.int32, (cp, cp), 0)
    tj = lax.broadcasted_iota(jnp.int32, (cp, cp), 1)
    tril = ti >= tj
    trilf = tril.astype(F32)
    glow = gl_ref[...].astype(BF16)
    log_a = [pad(jax.nn.log_sigmoid(_dot(glow, cols(wg_ref, h, dk)) + cols(bg_ref, h, dk)) / GATE_TAU) for h in hs]
    b = [jnp.dot(trilf, la, preferred_element_type=F32, precision=lax.Precision.HIGHEST) for la in log_a]
    b_last = [x[cp - 1:cp, :] for x in b]
    k = [pad(cols(k_ref, h, dk)) for h in hs]
    v = [pad(cols(v_ref, h, dv)).astype(BF16) for h in hs]
    q_in = [(pad(cols(q_ref, h, dk)) * (dk ** -0.5) * jnp.exp(b[h])).astype(BF16) for h in hs]
    k_in = [(k[h] * jnp.exp(-b[h])).astype(BF16) for h in hs]
    aug_t = [jnp.concatenate([k[h] * jnp.exp(b_last[h] - b[h]), jnp.broadcast_to(jnp.exp(b_last[h]), (cp, dk))],
                             axis=0).T for h in hs]
    att = [jnp.where(tril, _dot_nt(q_in[h], k_in[h]), 0.0).astype(BF16) for h in hs]
    st = [st_scr[h] for h in hs]
    o = [(_dot(att[h], v[h]) + _dot(q_in[h], st[h].astype(BF16)))[:chunk] for h in hs]
    st_new = [st[h] * aug_t[h][:, cp:cp + 1] + _dot(aug_t[h][:, :cp].astype(BF16), v[h]) for h in hs]
    for h in hs:
        st_scr[h] = st_new[h]
        ms = jnp.mean(o[h] * o[h], axis=-1, keepdims=True)
        gated = (o[h] * lax.rsqrt(ms + EPS) * gn_ref[...]) * jax.nn.silu(cols(r_ref, h, dv))
        o_ref[:, h * dv:(h + 1) * dv] = gated.astype(o_ref.dtype)

    @pl.when(c == pl.num_programs(1) - 1)
    def _():
        for h in hs:
            sn_ref[0, h] = st_new[h]


def _gla(z, zr, zg, w_gate, b_gate, gla_norm, s0, *, row0, chunk, n_chunks, col_q, col_k, col_v, out_dtype):
    n_seq, heads, dk, dv = s0.shape
    rb = row0 // chunk
    wk, wv = heads * dk, heads * dv
    assert col_q % wk == 0 and col_k % wk == 0 and col_v % wv == 0 and zr.shape[1] == wv

    def rows(b, c):
        return rb + b * n_chunks + c

    return pl.pallas_call(
        functools.partial(_gla_body, chunk=chunk, heads=heads, dk=dk, dv=dv),
        grid=(n_seq, n_chunks),
        in_specs=[pl.BlockSpec((chunk, wk), lambda b, c: (rows(b, c), col_q // wk)),
                  pl.BlockSpec((chunk, wk), lambda b, c: (rows(b, c), col_k // wk)),
                  pl.BlockSpec((chunk, wv), lambda b, c: (rows(b, c), col_v // wv)),
                  pl.BlockSpec((chunk, wv), lambda b, c: (rows(b, c), 0)),
                  pl.BlockSpec((chunk, LANES), lambda b, c: (rows(b, c), 0)),
                  pl.BlockSpec((LANES, wk), lambda b, c: (0, 0)),
                  pl.BlockSpec((1, wk), lambda b, c: (0, 0)),
                  pl.BlockSpec((1, dv), lambda b, c: (0, 0)),
                  pl.BlockSpec((1, heads, dk, dv), lambda b, c: (b, 0, 0, 0))],
        out_specs=[pl.BlockSpec((chunk, wv), lambda b, c: (b * n_chunks + c, 0)),
                   pl.BlockSpec((1, heads, dk, dv), lambda b, c: (b, 0, 0, 0))],
        out_shape=[jax.ShapeDtypeStruct((n_seq * n_chunks * chunk, wv), out_dtype),
                   jax.ShapeDtypeStruct((n_seq, heads, dk, dv), F32)],
        scratch_shapes=[pltpu.VMEM((heads, dk, dv), F32)],
        compiler_params=_cparams("parallel", "arbitrary"),
        name="gla",
    )(z, z, z, zr, zg, w_gate, b_gate, gla_norm, s0)


def _mm_out_body(a1p_ref, a1s_ref, a2p_ref, a2s_ref, w1_ref, w2_ref, xp_ref, xs_ref, o_ref, *, n_full, rem):
    i = pl.program_id(0)

    def go(a1, a2, x):
        o_ref[...] = x + _dot(a1, w1_ref[...]) + _dot(a2, w2_ref[...])

    @pl.when(i < n_full)
    def _():
        go(a1p_ref[...], a2p_ref[...], xp_ref[...])

    @pl.when(i == n_full)
    def _():
        def cat(p_ref, s_ref):
            return jnp.concatenate([p_ref[:rem], s_ref[...].astype(p_ref.dtype)], axis=0)

        go(cat(a1p_ref, a1s_ref), cat(a2p_ref, a2s_ref), cat(xp_ref, xs_ref))


def _mm_out(a1p, a1s, a2p, a2s, w, xp, xs, tm, tn):
    t_p, k1 = a1p.shape
    t_s = a1s.shape[0]
    k2 = a2p.shape[1]
    n = w.shape[1]
    n_full, rem = divmod(t_p, tm)
    assert rem > 0 and rem + t_s == tm and k1 == k2 and w.shape[0] == k1 + k2

    def pspec(k):
        return pl.BlockSpec((tm, k), lambda i, j: (i, 0))

    def sspec(k):
        return pl.BlockSpec((t_s, k), lambda i, j: (0, 0))

    return pl.pallas_call(
        functools.partial(_mm_out_body, n_full=n_full, rem=rem),
        grid=(n_full + 1, n // tn),
        in_specs=[pspec(k1), sspec(k1), pspec(k2), sspec(k2),
                  pl.BlockSpec((k1, tn), lambda i, j: (0, j)), pl.BlockSpec((k2, tn), lambda i, j: (1, j)),
                  pl.BlockSpec((tm, tn), lambda i, j: (i, j)), pl.BlockSpec((t_s, tn), lambda i, j: (0, j))],
        out_specs=pl.BlockSpec((tm, tn), lambda i, j: (i, j)),
        out_shape=jax.ShapeDtypeStruct((t_p + t_s, n), F32),
        compiler_params=_cparams("parallel", "arbitrary"),
        name="mm_out",
    )(a1p, a1s, a2p, a2s, w, w, xp, xs)


def _rows_iota(shape):
    return lax.broadcasted_iota(jnp.int32, shape, 0).astype(F32)


def _topk_cols(s, ranks, k):
    big = float(2 ** 23)
    for _ in range(k):
        m = jnp.max(s, axis=0, keepdims=True)
        r = jnp.min(jnp.where(s == m, ranks, big), axis=0, keepdims=True)
        sel = ranks == r
        yield m, r, sel
        s = jnp.where(sel, -jnp.inf, s)


def _stack_rows(rows):
    n = len(rows)
    rid = lax.broadcasted_iota(jnp.int32, (n, rows[0].shape[1]), 0)
    out = jnp.broadcast_to(rows[0], (n, rows[0].shape[1]))
    for i in range(1, n):
        out = jnp.where(rid == i, rows[i], out)
    return out


def _route_body(q_ref, k1_ref, k2_ref, ii_ref, jj_ref, g_ref):
    tt = q_ref.shape[0]
    half = PEER_DKEY // 2
    topk = PEER_TOPK
    key_rank = _rows_iota((N_KEYS, tt))
    r8 = _rows_iota((8, tt))
    r16 = _rows_iota((topk, tt))
    gates, experts = [], []
    for h in range(PEER_HEADS):
        qh = q_ref[:, h * PEER_DKEY:(h + 1) * PEER_DKEY]
        qn = (qh * lax.rsqrt(jnp.mean(qh * qh, axis=-1, keepdims=True) + EPS)).astype(BF16)
        s1 = _dot_nt(k1_ref[h], qn[:, :half])
        s2 = _dot_nt(k2_ref[h], qn[:, half:])
        v1, i1 = zip(*[(m, r) for m, r, _ in _topk_cols(s1, key_rank, topk)])
        v2, i2 = zip(*[(m, r) for m, r, _ in _topk_cols(s2, key_rank, topk)])
        v1t, i1t, v2t, i2t = (_stack_rows(list(x)) for x in (v1, i1, v2, i2))
        cand = [v1[0] + v2t]
        code = [i1[0] * N_KEYS + i2t]
        flat = [r16]
        for a in range(1, 8):
            ok = r8 < float(topk // (a + 1))
            cand.append(jnp.where(ok, v1[a] + v2t[:8], -jnp.inf))
            code.append(i1[a] * N_KEYS + i2t[:8])
            flat.append(r8 + float(a * topk))
        cand.append(v1t[8:] + v2[0])
        code.append(i1t[8:] * N_KEYS + i2[0])
        flat.append((r8 + 8.0) * float(topk))
        cand, code, flat = (jnp.concatenate(x, axis=0) for x in (cand, code, flat))
        top, picked = zip(*[(m, r) for m, r, _ in _topk_cols(cand, flat * float(N_KEYS * N_KEYS) + code, topk)])
        ex = [jnp.exp(m - top[0]) for m in top]
        inv_z = 1.0 / functools.reduce(jnp.add, ex)
        gates.append(_stack_rows([e * inv_z for e in ex]))
        experts.append(_stack_rows(list(picked)))
    g_ref[...] = jnp.concatenate(gates, axis=0).T
    e = jnp.concatenate(experts, axis=0).T.astype(jnp.int32)
    e = lax.bitwise_and(e, N_KEYS * N_KEYS - 1)
    ii_ref[...] = lax.shift_right_logical(e, N_KEYS.bit_length() - 1)
    jj_ref[...] = lax.bitwise_and(e, N_KEYS - 1)


def _route(q, keys1, keys2, tt):
    t, n = q.shape
    no = PEER_HEADS * PEER_TOPK
    kspec = pl.BlockSpec(keys1.shape, lambda i: (0, 0, 0))
    ospec = pl.BlockSpec((tt, no), lambda i: (i, 0))
    return pl.pallas_call(
        _route_body,
        grid=(t // tt,),
        in_specs=[pl.BlockSpec((tt, n), lambda i: (i, 0)), kspec, kspec],
        out_specs=[ospec, ospec, ospec],
        out_shape=[jax.ShapeDtypeStruct((t, no), jnp.int32), jax.ShapeDtypeStruct((t, no), jnp.int32),
                   jax.ShapeDtypeStruct((t, no), F32)],
        compiler_params=_cparams("parallel"),
        name="peer_route",
    )(q, keys1, keys2)


EXPAND_GROUP = 16
EXPAND_PITCH = N_KEYS + 8


def _expand_body(ii_ref, jj_ref, g_ref, o_ref, scr):
    te = ii_ref.shape[0]
    sub = lax.broadcasted_iota(jnp.int32, (N_KEYS, ii_ref.shape[1]), 0)

    def group(gi, carry):
        base = pl.multiple_of(gi * EXPAND_GROUP, EXPAND_GROUP)
        for u in range(EXPAND_GROUP):
            ii = ii_ref[pl.ds(base + u, 1), :]
            jj = jj_ref[pl.ds(base + u, 1), :]
            g = g_ref[pl.ds(base + u, 1), :]
            a = jnp.where(sub == ii, 1.0, 0.0).astype(BF16)
            b = jnp.where(sub == jj, g, 0.0).astype(BF16)
            scr[u * EXPAND_PITCH:u * EXPAND_PITCH + N_KEYS, :] = _dot_nt(a, b)
        for i in range(N_KEYS):
            rows = scr[pl.ds(i, EXPAND_GROUP, stride=EXPAND_PITCH), :]
            o_ref[pl.ds(base, EXPAND_GROUP), i * N_KEYS:(i + 1) * N_KEYS] = rows.astype(o_ref.dtype)
        return carry

    lax.fori_loop(0, te // EXPAND_GROUP, group, 0)


def _expand(ii, jj, g, te):
    t, no = ii.shape
    spec = pl.BlockSpec((te, no), lambda i: (i, 0))
    return pl.pallas_call(
        _expand_body,
        grid=(t // te,),
        in_specs=[spec, spec, spec],
        out_specs=pl.BlockSpec((te, N_KEYS * N_KEYS), lambda i: (i, 0)),
        out_shape=jax.ShapeDtypeStruct((t, N_KEYS * N_KEYS), BF16),
        scratch_shapes=[pltpu.VMEM((EXPAND_GROUP * EXPAND_PITCH, N_KEYS), F32)],
        compiler_params=_cparams("parallel"),
        name="peer_expand",
    )(ii, jj, g)


PEER_SUB = 512


def _peer_a_body(h_ref, u_ref, g_ref, o_ref):
    h = h_ref[...]
    for s in range(o_ref.shape[1] // PEER_SUB):
        cols = slice(s * PEER_SUB, (s + 1) * PEER_SUB)
        a = _dot_nt(h, u_ref[cols, :])
        o_ref[:, cols] = (g_ref[:, cols].astype(F32) * jax.nn.gelu(a)).astype(o_ref.dtype)


def _peer_a(h, u, gates, tm, tn):
    t, k = h.shape
    e = u.shape[0]
    return pl.pallas_call(
        _peer_a_body,
        grid=(t // tm, e // tn),
        in_specs=[pl.BlockSpec((tm, k), lambda i, j: (i, 0)), pl.BlockSpec((tn, k), lambda i, j: (j, 0)),
                  pl.BlockSpec((tm, tn), lambda i, j: (i, j))],
        out_specs=pl.BlockSpec((tm, tn), lambda i, j: (i, j)),
        out_shape=jax.ShapeDtypeStruct((t, e), BF16),
        compiler_params=_cparams("parallel", "arbitrary"),
        name="peer_act",
    )(h, u, gates)


def _peer_v_body(c_ref, v_ref, x_ref, yp_ref, ys_ref, acc_scr, *, n_full, rem):
    i = pl.program_id(0)
    kk = pl.program_id(2)
    last = pl.num_programs(2) - 1

    @pl.when(kk == 0)
    def _():
        acc_scr[...] = x_ref[...]

    acc_scr[...] += _dot(c_ref[...], v_ref[...].astype(BF16))

    @pl.when(kk == last)
    def _():
        yp_ref[...] = acc_scr[...]

    @pl.when((kk == last) & (i == n_full))
    def _():
        ys_ref[...] = acc_scr[rem:, :]


def _peer_v(c, v, x, t_p, tm, tn, tk):
    t, e = c.shape
    n = v.shape[1]
    t_s = t - t_p
    n_full, rem = divmod(t_p, tm)
    assert rem > 0 and rem + t_s == tm
    return pl.pallas_call(
        functools.partial(_peer_v_body, n_full=n_full, rem=rem),
        grid=(t // tm, n // tn, e // tk),
        in_specs=[pl.BlockSpec((tm, tk), lambda i, j, k: (i, k)), pl.BlockSpec((tk, tn), lambda i, j, k: (k, j)),
                  pl.BlockSpec((tm, tn), lambda i, j, k: (i, j))],
        out_specs=[pl.BlockSpec((tm, tn), lambda i, j, k: (i, j)),
                   pl.BlockSpec((t_s, tn), lambda i, j, k: (0, jnp.where(i == n_full, j, 0)))],
        out_shape=[jax.ShapeDtypeStruct((t_p, n), F32), jax.ShapeDtypeStruct((t_s, n), F32)],
        scratch_shapes=[pltpu.VMEM((tm, tn), F32)],
        compiler_params=_cparams("arbitrary", "arbitrary", "arbitrary"),
        name="peer_mix",
    )(c, v, x)


def _token_tile(t, cap):
    best = 16
    for c in range(16, cap + 1, 16):
        if t % c == 0:
            best = c
    return best


def _rope_tables(pos):
    half = DH_A // 2
    inv = ROPE_THETA ** (-jnp.arange(half, dtype=F32) / half)
    ang = pos.astype(F32)[:, None] * inv[None, :]
    cos, sin = jnp.cos(ang), jnp.sin(ang)
    return jnp.concatenate([cos, cos], axis=1), jnp.concatenate([-sin, sin], axis=1)


def kernel(x_prompt, x_sample, cache_attn_k, cache_attn_v, state_gla, norm_mix, w_in, q_norm, k_norm, w_gate, b_gate,
           gla_norm, w_out, norm_ffn, w_peer_q, peer_keys1, peer_keys2, peer_u, peer_v):
    n_p, seq, d_model = x_prompt.shape
    n_s, dec_seq, _ = x_sample.shape
    depth = w_in.shape[0]
    heads_a = cache_attn_k.shape[3]
    w_a = heads_a * DH_A
    _, _, heads_b, dk, dv = state_gla.shape
    w_b = heads_b * dv
    assert seq == DILATED_BRANCHES[-1][0] and heads_b == H_B and cache_attn_k.shape[2] == seq
    t_p = n_p * seq
    t_s = n_s * dec_seq
    t_all = t_p + t_s
    tm = _token_tile(t_all, 1056)
    tt = _token_tile(t_s, 256)

    xp = x_prompt.reshape(t_p, d_model)
    xs = x_sample.reshape(t_s, d_model)
    rope_p = _rope_tables(jnp.arange(seq, dtype=jnp.int32))
    rope_s = _rope_tables(PAST_LEN + jnp.arange(dec_seq, dtype=jnp.int32))

    c_qb = 3 * w_a
    c_kb = c_qb + heads_b * dk
    c_vb = c_kb + heads_b * dk
    c_g = c_vb + w_b
    c_r = c_g + GATE_RANK

    outs = dict(kp=[], vp=[], ks=[], vs=[], sp=[], ss=[])
    for l in range(depth):
        w_in_t = jnp.swapaxes(w_in[l], 0, 1)
        w_r = w_in_t[c_r:]
        w_glow = jnp.pad(w_in_t[c_g:c_r], ((0, LANES - GATE_RANK), (0, 0)))
        w_gate_p = jnp.pad(w_gate[l].astype(BF16), ((0, LANES - GATE_RANK), (0, 0)))

        if l == 0:
            h = _rmsnorm_pair(xp, xs, norm_mix[l], tt)
        else:
            h = _rmsnorm(x, norm_mix[l], tt)
            xp, xs = x[:t_p], x[t_p:]
        z = _matmul(h, w_in_t, tm, 512, n_cols=c_g, w_is_nk=True)
        zr = _matmul(h, w_r, tm, 512, w_is_nk=True)
        zg = _matmul(h, w_glow, tm, LANES, w_is_nk=True)

        o_a_p, k_p, (peer_u_bf16, peer_v_bf16) = _attn_prompt(z, q_norm[l], k_norm[l], *rope_p,
                                                              (peer_u[l], peer_v[l]), n_p, seq, heads_a)
        o_a_s, k_s = _attn_sample(z, q_norm[l], k_norm[l], *rope_s, cache_attn_k[l], cache_attn_v[l], t_p, dec_seq)
        gla_args = (z, zr, zg, w_gate_p, b_gate[l].reshape(1, -1), gla_norm[l].reshape(1, -1))
        gla_cols = dict(col_q=c_qb, col_k=c_kb, col_v=c_vb)
        o_b_p, s_p = _gla(*gla_args, jnp.zeros((n_p, heads_b, dk, dv), F32), row0=0, chunk=GLA_CHUNK,
                          n_chunks=seq // GLA_CHUNK, out_dtype=BF16, **gla_cols)
        o_b_s, s_s = _gla(*gla_args, state_gla[l], row0=t_p, chunk=dec_seq, n_chunks=1, out_dtype=F32, **gla_cols)

        x1 = _mm_out(o_a_p, o_a_s, o_b_p, o_b_s, w_out[l].astype(BF16), xp, xs, tm, 512)

        h2 = _rmsnorm(x1, norm_ffn[l], tt)
        q = _matmul(h2, w_peer_q[l], tm, 512)
        ii, jj, g = _route(q, peer_keys1[l].astype(BF16), peer_keys2[l].astype(BF16), LANES)
        gates = _expand(ii, jj, g, _token_tile(t_all, 128))
        c_act = _peer_a(h2, peer_u_bf16, gates, tm, 2 * PEER_SUB)
        yp, ys = _peer_v(c_act, peer_v_bf16, x1, t_p, tm, 1024, 2048)
        if l + 1 < depth:
            x = jnp.concatenate([yp, ys], axis=0)

        outs["kp"].append(k_p.reshape(n_p, seq, heads_a, DH_A))
        outs["vp"].append(z[:t_p, 2 * w_a:3 * w_a].reshape(n_p, seq, heads_a, DH_A))
        outs["ks"].append(k_s.reshape(n_s, dec_seq, heads_a, DH_A))
        outs["vs"].append(z[t_p:, 2 * w_a:3 * w_a].reshape(n_s, dec_seq, heads_a, DH_A))
        outs["sp"].append(s_p)
        outs["ss"].append(s_s)

    y_prompt = yp.reshape(n_p, seq, d_model)
    y_sample = ys.reshape(n_s, dec_seq, d_model)
    return (y_prompt, y_sample, jnp.stack(outs["kp"]), jnp.stack(outs["vp"]), jnp.stack(outs["ks"]),
            jnp.stack(outs["vs"]), jnp.stack(outs["sp"]), jnp.stack(outs["ss"]))
```

```python
import functools

import numpy as np
import jax
import jax.numpy as jnp
from jax import lax
from jax.experimental import pallas as pl
from jax.experimental.pallas import tpu as pltpu

F32 = jnp.float32
BF16 = jnp.bfloat16

DH_A = 128
DILATED_BRANCHES = ((128, 1), (512, 4), (2048, 16))
ROPE_THETA = 10000.0
PAST_LEN = 8192
H_B = 4
GATE_RANK = 16
GATE_TAU = 16.0
GLA_CHUNK = 64
N_KEYS = 128
PEER_HEADS = 8
PEER_DKEY = 256
PEER_TOPK = 16
EPS = 1e-6
NEG_INF = -1e30

LANES = 128
VMEM_LIMIT = 56 * 1024 * 1024


def _cparams(*sem):
    return pltpu.CompilerParams(dimension_semantics=sem, vmem_limit_bytes=VMEM_LIMIT)


def _dot(a, b):
    return jnp.dot(a, b, preferred_element_type=F32)


def _dot_nt(a, b):
    return lax.dot_general(a, b, (((1,), (1,)), ((), ())), preferred_element_type=F32)


def _dot_tn(a, b):
    return lax.dot_general(a, b, (((0,), (0,)), ((), ())), preferred_element_type=F32)


def _rmsnorm_rows(x, gain):
    ms = jnp.mean(x * x, axis=-1, keepdims=True)
    return (x * lax.rsqrt(ms + EPS) * gain).astype(BF16)


def _rmsnorm_body(x_ref, g_ref, o_ref):
    o_ref[...] = _rmsnorm_rows(x_ref[...], g_ref[...])


def _rmsnorm(x, gain, tt):
    t, d = x.shape
    return pl.pallas_call(
        _rmsnorm_body,
        grid=(t // tt,),
        in_specs=[pl.BlockSpec((tt, d), lambda i: (i, 0)), pl.BlockSpec((1, d), lambda i: (0, 0))],
        out_specs=pl.BlockSpec((tt, d), lambda i: (i, 0)),
        out_shape=jax.ShapeDtypeStruct((t, d), BF16),
        compiler_params=_cparams("parallel"),
        name="rmsnorm",
    )(x, gain.reshape(1, d))


def _rmsnorm_pair_body(xp_ref, xs_ref, g_ref, o_ref, *, n_p):
    i = pl.program_id(0)

    @pl.when(i < n_p)
    def _():
        o_ref[...] = _rmsnorm_rows(xp_ref[...], g_ref[...])

    @pl.when(i >= n_p)
    def _():
        o_ref[...] = _rmsnorm_rows(xs_ref[...], g_ref[...])


def _rmsnorm_pair(xp, xs, gain, tt):
    t_p, d = xp.shape
    t_s = xs.shape[0]
    assert t_p % tt == 0 and t_s % tt == 0
    n_p = t_p // tt
    return pl.pallas_call(
        functools.partial(_rmsnorm_pair_body, n_p=n_p),
        grid=((t_p + t_s) // tt,),
        in_specs=[pl.BlockSpec((tt, d), lambda i: (jnp.minimum(i, n_p - 1), 0)),
                  pl.BlockSpec((tt, d), lambda i: (jnp.maximum(i - n_p, 0), 0)),
                  pl.BlockSpec((1, d), lambda i: (0, 0))],
        out_specs=pl.BlockSpec((tt, d), lambda i: (i, 0)),
        out_shape=jax.ShapeDtypeStruct((t_p + t_s, d), BF16),
        compiler_params=_cparams("arbitrary"),
        name="rmsnorm_in",
    )(xp, xs, gain.reshape(1, d))


def _mm_body(a_ref, w_ref, o_ref, *, w_is_nk):
    w = w_ref[...].astype(BF16)
    o_ref[...] = _dot_nt(a_ref[...], w) if w_is_nk else _dot(a_ref[...], w)


def _matmul(a, w, tm, tn, n_cols=None, w_is_nk=False, col0=0):
    t, k = a.shape
    n = (w.shape[0] if w_is_nk else w.shape[1]) if n_cols is None else n_cols
    if w_is_nk:
        assert col0 % 8 == 0 and tn % 8 == 0
        w_spec = pl.BlockSpec((pl.Element(tn), pl.Element(k)), lambda i, j: (pl.multiple_of(col0 + j * tn, 8), 0))
    else:
        assert col0 == 0
        w_spec = pl.BlockSpec((k, tn), lambda i, j: (0, j))
    return pl.pallas_call(
        functools.partial(_mm_body, w_is_nk=w_is_nk),
        grid=(t // tm, n // tn),
        in_specs=[pl.BlockSpec((tm, k), lambda i, j: (i, 0)), w_spec],
        out_specs=pl.BlockSpec((tm, tn), lambda i, j: (i, j)),
        out_shape=jax.ShapeDtypeStruct((t, n), F32),
        compiler_params=_cparams("parallel", "arbitrary"),
        name="matmul",
    )(a, w)


def _norm_rope(x, gain, cos2, sin2):
    ms = jnp.mean(x * x, axis=-1, keepdims=True)
    y = x * lax.rsqrt(ms + EPS) * gain
    return y * cos2 + pltpu.roll(y, DH_A // 2, axis=1) * sin2


ATTN_GROUP = 8


def _attn_prompt_body(q_ref, k_ref, v_ref, gq_ref, gk_ref, cos_ref, sin_ref, *rest, seq, n_side):
    side_in, o_ref, kr_ref = rest[:n_side], rest[n_side], rest[n_side + 1]
    side_out, scr = rest[n_side + 2:2 * n_side + 2], rest[2 * n_side + 2:]
    for src, dst in zip(side_in, side_out):
        dst[...] = src[...].astype(dst.dtype)
    blk = DH_A
    n_br = len(DILATED_BRANCHES)
    qs_scr, ks_scr, scr = scr[0], scr[1], scr[2:]
    m_scr, l_scr, acc_scr = scr[:n_br], scr[n_br:2 * n_br], scr[2 * n_br:]

    prep_rows = 4 * blk

    def prep(c, carry):
        rows = pl.ds(pl.multiple_of(c * prep_rows, prep_rows), prep_rows)
        cos2, sin2 = cos_ref[rows, :], sin_ref[rows, :]
        qs_scr[rows, :] = _norm_rope(q_ref[rows, :], gq_ref[...], cos2, sin2) * (DH_A ** -0.5)
        kr = _norm_rope(k_ref[rows, :], gk_ref[...], cos2, sin2)
        ks_scr[rows, :] = kr
        kr_ref[rows, :] = kr
        return carry

    lax.fori_loop(0, seq // prep_rows, prep, 0)
    qi = lax.broadcasted_iota(jnp.int32, (blk, 2 * blk), 0)
    kc = lax.broadcasted_iota(jnp.int32, (blk, 2 * blk), 1)
    band = (kc >= qi) & (kc <= qi + blk)
    cur_half = kc >= blk
    qi1 = lax.broadcasted_iota(jnp.int32, (blk, blk), 0)
    causal = lax.broadcasted_iota(jnp.int32, (blk, blk), 1) <= qi1
    floor_tile = 2.0 * NEG_INF - qi1.astype(F32)

    for bi, (window, d) in enumerate(DILATED_BRANCHES):
        assert window // d == blk
        nb = seq // d // blk

        def blocks(it, carry, bi=bi, d=d, nb=nb):
            rows, vvs, ss = [], [], []
            for u in range(ATTN_GROUP):
                idx = it * ATTN_GROUP + u
                if nb == 1:
                    rw = pl.ds(idx, blk, stride=d)
                    kk = ks_scr[rw, :].astype(BF16)
                    vv = v_ref[rw, :].astype(BF16)
                    mask = causal
                else:
                    r = idx // nb
                    j = idx % nb
                    start = r + j * (blk * d)
                    rw = pl.ds(start, blk, stride=d)
                    prows = pl.ds(jnp.maximum(start - blk * d, r), blk, stride=d)
                    kk = jnp.concatenate([ks_scr[prows, :], ks_scr[rw, :]], axis=0).astype(BF16)
                    vv = jnp.concatenate([v_ref[prows, :], v_ref[rw, :]], axis=0).astype(BF16)
                    mask = band & (cur_half | (j > 0))
                q = qs_scr[rw, :].astype(BF16)
                rows.append(rw)
                vvs.append(vv)
                ss.append(jnp.where(mask, _dot_nt(q, kk), NEG_INF))
            ms = [jnp.max(s, axis=-1, keepdims=True) for s in ss]
            ps = [jnp.exp(s - m) for s, m in zip(ss, ms)]
            for rw, m, p, vv in zip(rows, ms, ps, vvs):
                m_scr[bi][rw, :] = jnp.maximum(m, floor_tile)
                l_scr[bi][rw, :] = jnp.maximum(jnp.sum(p, axis=-1, keepdims=True), floor_tile)
                acc_scr[bi][rw, :] = _dot(p.astype(BF16), vv)
            return carry

        assert (d * nb) % ATTN_GROUP == 0
        lax.fori_loop(0, d * nb // ATTN_GROUP, blocks, 0)

    def merge(c, carry):
        rows = pl.ds(pl.multiple_of(c * blk, blk), blk)
        ms = [m[rows, :] for m in m_scr]
        m_max = functools.reduce(jnp.maximum, ms)
        ws = [jnp.exp(m - m_max) for m in ms]
        num = functools.reduce(jnp.add, [w * a[rows, :] for w, a in zip(ws, acc_scr)])
        den = functools.reduce(jnp.add, [w * l[rows, :] for w, l in zip(ws, l_scr)])
        o_ref[rows, :] = (num / den).astype(o_ref.dtype)
        return carry

    lax.fori_loop(0, seq // blk, merge, 0)


def _attn_prompt(z, q_gain, k_gain, cos2, sin2, sides, n_seq, seq, heads):
    steps = n_seq * heads
    assert all(s.shape[0] % steps == 0 for s in sides)
    side_specs = [pl.BlockSpec((s.shape[0] // steps, s.shape[1]), lambda b, h: (b * heads + h, 0)) for s in sides]
    head_spec = pl.BlockSpec((seq, DH_A), lambda b, h: (b, h))
    vec_spec = pl.BlockSpec((1, DH_A), lambda b, h: (0, 0))
    tab_spec = pl.BlockSpec((seq, DH_A), lambda b, h: (0, 0))
    outs = pl.pallas_call(
        functools.partial(_attn_prompt_body, seq=seq, n_side=len(sides)),
        grid=(n_seq, heads),
        in_specs=[head_spec,
                  pl.BlockSpec((seq, DH_A), lambda b, h: (b, heads + h)),
                  pl.BlockSpec((seq, DH_A), lambda b, h: (b, 2 * heads + h)),
                  vec_spec, vec_spec, tab_spec, tab_spec] + side_specs,
        out_specs=[head_spec, head_spec] + side_specs,
        out_shape=[jax.ShapeDtypeStruct((n_seq * seq, heads * DH_A), BF16),
                   jax.ShapeDtypeStruct((n_seq * seq, heads * DH_A), F32)]
        + [jax.ShapeDtypeStruct(s.shape, BF16) for s in sides],
        scratch_shapes=[pltpu.VMEM((seq, DH_A), F32)] * (2 + 3 * len(DILATED_BRANCHES)),
        compiler_params=_cparams("parallel", "arbitrary"),
        name="attn_prompt",
    )(z, z, z, q_gain.reshape(1, DH_A), k_gain.reshape(1, DH_A), cos2, sin2, *sides)
    return outs[0], outs[1], outs[2:]


SAMPLE_CHUNK_ROWS = 4096


def _sample_plan(win_buf, dec_seq, heads):
    dil = DILATED_BRANCHES[-1][1]
    dense_from = win_buf - DILATED_BRANCHES[-2][0]
    assert win_buf % dil == 0 and dense_from % dil == 0 and dec_seq * 2 == dil and dec_seq * heads == LANES
    pos_sparse = SAMPLE_CHUNK_ROWS // (dec_seq * heads) * dil
    pos_dense = SAMPLE_CHUNK_ROWS // heads
    assert dense_from % pos_sparse == 0 and (win_buf - dense_from) % pos_dense == 0 and dense_from % pos_dense == 0
    return dil, dense_from, dense_from // pos_sparse, (win_buf - dense_from) // pos_dense


def _sample_weights(win_buf, dec_seq, heads):
    dil, dense_from, n_sparse, n_dense = _sample_plan(win_buf, dec_seq, heads)
    n = np.arange(win_buf + dec_seq)[None, :]
    s = np.arange(dec_seq)[:, None]
    dist = win_buf + s - n
    cnt = np.zeros((dec_seq, win_buf + dec_seq), np.float32)
    for window, d in DILATED_BRANCHES:
        cnt += (dist >= 0) & (dist <= window) & (dist % d == 0)
    assert not cnt[:, :dense_from].reshape(dec_seq, -1, dil)[:, :, dec_seq:].any()
    eye = np.eye(heads, dtype=np.float32)
    sparse = cnt[:, :dense_from].reshape(dec_seq, -1, dil)[:, :, :dec_seq].reshape(dec_seq, -1)
    sparse = np.einsum("sn,hg->nhgs", sparse, eye).reshape(n_sparse, SAMPLE_CHUNK_ROWS, LANES)
    dense = np.einsum("sn,hg->nhgs", cnt[:, dense_from:win_buf], eye).reshape(n_dense, SAMPLE_CHUNK_ROWS, LANES)
    new = np.einsum("sn,hg->hngs", cnt[:, win_buf:], eye).reshape(LANES, LANES)
    return np.concatenate([sparse, dense], axis=0), new


def _attn_sample_body(q_ref, ks_ref, kd_ref, vs_ref, vd_ref, kn_ref, vn_ref, w_ref, wn_ref, gq_ref, gk_ref,
                      cos_ref, sin_ref, o_ref, kr_ref, q_scr, m_scr, l_scr, acc_scr,
                      *, heads, dec_seq, n_sparse, n_dense):
    c = pl.program_id(1)
    n_chunks = n_sparse + n_dense

    def by_head(ref, fn=lambda x: x):
        return jnp.concatenate([fn(ref[:, h * DH_A:(h + 1) * DH_A]) for h in range(heads)], axis=0)

    def rotated(gain_ref):
        return lambda x: _norm_rope(x, gain_ref[...], cos_ref[...], sin_ref[...])

    @pl.when(c == 0)
    def _():
        q_scr[...] = (by_head(q_ref, rotated(gq_ref)) * (DH_A ** -0.5)).astype(BF16)
        m_scr[...] = jnp.full(m_scr.shape, NEG_INF, F32)
        l_scr[...] = jnp.zeros(l_scr.shape, F32)
        acc_scr[...] = jnp.zeros(acc_scr.shape, F32)

    def step(kb, vb, w):
        s = jnp.where(w > 0, _dot_nt(kb, q_scr[...]), NEG_INF)
        m_old = m_scr[...]
        m_new = jnp.maximum(m_old, jnp.max(s, axis=0, keepdims=True))
        alpha = jnp.exp(m_old - m_new)
        p = w * jnp.exp(s - m_new)
        l_scr[...] = alpha * l_scr[...] + jnp.sum(p, axis=0, keepdims=True)
        acc_scr[...] = alpha * acc_scr[...] + _dot_tn(vb, p.astype(BF16))
        m_scr[...] = m_new

    @pl.when(c < n_sparse)
    def _():
        rows = (SAMPLE_CHUNK_ROWS, DH_A)
        step(ks_ref[0].reshape(rows).astype(BF16), vs_ref[0].reshape(rows).astype(BF16), w_ref[c])

    @pl.when((c >= n_sparse) & (c < n_chunks))
    def _():
        step(kd_ref[0].astype(BF16), vd_ref[0].astype(BF16), w_ref[c])

    @pl.when(c == n_chunks)
    def _():
        kn = by_head(kn_ref, rotated(gk_ref))
        step(kn.astype(BF16), by_head(vn_ref).astype(BF16), wn_ref[...])
        out = (acc_scr[...] / l_scr[...]).T
        for h in range(heads):
            o_ref[:, h * DH_A:(h + 1) * DH_A] = out[h * dec_seq:(h + 1) * dec_seq, :]
            kr_ref[:, h * DH_A:(h + 1) * DH_A] = kn[h * dec_seq:(h + 1) * dec_seq, :]


def _attn_sample(z, q_gain, k_gain, cos2, sin2, cache_k, cache_v, row0, dec_seq):
    n_seq, win_buf, heads, _ = cache_k.shape
    width = heads * DH_A
    dil, dense_from, n_sparse, n_dense = _sample_plan(win_buf, dec_seq, heads)
    n_chunks = n_sparse + n_dense
    w_main, w_new = _sample_weights(win_buf, dec_seq, heads)
    rb = row0 // dec_seq
    groups = SAMPLE_CHUNK_ROWS // LANES
    dense0 = dense_from * heads // SAMPLE_CHUNK_ROWS

    def new_spec(col):
        return pl.BlockSpec((dec_seq, width), lambda b, c: (rb + b, col))

    def small(shape):
        return pl.BlockSpec(shape, lambda b, c: (0, 0))

    out_spec = pl.BlockSpec((dec_seq, width), lambda b, c: (b, 0))

    sparse_view = lambda x: x.reshape(n_seq, win_buf // dil, dil * heads, DH_A)
    dense_view = lambda x: x.reshape(n_seq, win_buf * heads, DH_A)
    sparse_spec = pl.BlockSpec((1, groups, LANES, DH_A), lambda b, c: (b, jnp.minimum(c, n_sparse - 1), 0, 0))
    dense_spec = pl.BlockSpec((1, SAMPLE_CHUNK_ROWS, DH_A),
                              lambda b, c: (b, dense0 + jnp.clip(c - n_sparse, 0, n_dense - 1), 0))
    return pl.pallas_call(
        functools.partial(_attn_sample_body, heads=heads, dec_seq=dec_seq, n_sparse=n_sparse, n_dense=n_dense),
        grid=(n_seq, n_chunks + 1),
        in_specs=[new_spec(0), sparse_spec, dense_spec, sparse_spec, dense_spec, new_spec(1), new_spec(2),
                  pl.BlockSpec(w_main.shape, lambda b, c: (0, 0, 0)),
                  pl.BlockSpec(w_new.shape, lambda b, c: (0, 0)),
                  small((1, DH_A)), small((1, DH_A)), small((dec_seq, DH_A)), small((dec_seq, DH_A))],
        out_specs=[out_spec, out_spec],
        out_shape=[jax.ShapeDtypeStruct((n_seq * dec_seq, width), F32)] * 2,
        scratch_shapes=[pltpu.VMEM((LANES, DH_A), BF16), pltpu.VMEM((1, LANES), F32), pltpu.VMEM((1, LANES), F32),
                        pltpu.VMEM((DH_A, LANES), F32)],
        compiler_params=_cparams("parallel", "arbitrary"),
        name="attn_sample",
    )(z, sparse_view(cache_k), dense_view(cache_k), sparse_view(cache_v), dense_view(cache_v), z, z,
      jnp.asarray(w_main), jnp.asarray(w_new), q_gain.reshape(1, DH_A), k_gain.reshape(1, DH_A),
      cos2, sin2)


def _gla_body(q_ref, k_ref, v_ref, r_ref, gl_ref, wg_ref, bg_ref, gn_ref, s0_ref, o_ref, sn_ref, st_scr,
              *, chunk, heads, dk, dv):
    c = pl.program_id(1)
    hs = range(heads)

    @pl.when(c == 0)
    def _():
        for h in hs:
            st_scr[h] = s0_ref[0, h]

    cp = max(chunk, LANES // 2)

    def pad(a):
        return a if cp == chunk else jnp.concatenate([a, jnp.zeros((cp - chunk, a.shape[1]), a.dtype)], axis=0)

    def cols(ref, h, w):
        return ref[:, h * w:(h + 1) * w]

    ti = lax.broadcasted_iota(jnp.int32, (cp, cp), 0)
    tj = lax.broadcasted_iota(jnp.int32, (cp, cp), 1)
    tril = ti >= tj
    trilf = tril.astype(F32)
    glow = gl_ref[...].astype(BF16)
    log_a = [pad(jax.nn.log_sigmoid(_dot(glow, cols(wg_ref, h, dk)) + cols(bg_ref, h, dk)) / GATE_TAU) for h in hs]
    b = [jnp.dot(trilf, la, preferred_element_type=F32, precision=lax.Precision.HIGHEST) for la in log_a]
    b_last = [x[cp - 1:cp, :] for x in b]
    k = [pad(cols(k_ref, h, dk)) for h in hs]
    v = [pad(cols(v_ref, h, dv)).astype(BF16) for h in hs]
    q_in = [(pad(cols(q_ref, h, dk)) * (dk ** -0.5) * jnp.exp(b[h])).astype(BF16) for h in hs]
    k_in = [(k[h] * jnp.exp(-b[h])).astype(BF16) for h in hs]
    aug_t = [jnp.concatenate([k[h] * jnp.exp(b_last[h] - b[h]), jnp.broadcast_to(jnp.exp(b_last[h]), (cp, dk))],
                             axis=0).T for h in hs]
    att = [jnp.where(tril, _dot_nt(q_in[h], k_in[h]), 0.0).astype(BF16) for h in hs]
    st = [st_scr[h] for h in hs]
    o = [(_dot(att[h], v[h]) + _dot(q_in[h], st[h].astype(BF16)))[:chunk] for h in hs]
    st_new = [st[h] * aug_t[h][:, cp:cp + 1] + _dot(aug_t[h][:, :cp].astype(BF16), v[h]) for h in hs]
    for h in hs:
        st_scr[h] = st_new[h]
        ms = jnp.mean(o[h] * o[h], axis=-1, keepdims=True)
        gated = (o[h] * lax.rsqrt(ms + EPS) * gn_ref[...]) * jax.nn.silu(cols(r_ref, h, dv))
        o_ref[:, h * dv:(h + 1) * dv] = gated.astype(o_ref.dtype)

    @pl.when(c == pl.num_programs(1) - 1)
    def _():
        for h in hs:
            sn_ref[0, h] = st_new[h]


def _gla(z, zr, zg, w_gate, b_gate, gla_norm, s0, *, row0, chunk, n_chunks, col_q, col_k, col_v, out_dtype):
    n_seq, heads, dk, dv = s0.shape
    rb = row0 // chunk
    wk, wv = heads * dk, heads * dv
    assert col_q % wk == 0 and col_k % wk == 0 and col_v % wv == 0 and zr.shape[1] == wv

    def rows(b, c):
        return rb + b * n_chunks + c

    return pl.pallas_call(
        functools.partial(_gla_body, chunk=chunk, heads=heads, dk=dk, dv=dv),
        grid=(n_seq, n_chunks),
        in_specs=[pl.BlockSpec((chunk, wk), lambda b, c: (rows(b, c), col_q // wk)),
                  pl.BlockSpec((chunk, wk), lambda b, c: (rows(b, c), col_k // wk)),
                  pl.BlockSpec((chunk, wv), lambda b, c: (rows(b, c), col_v // wv)),
                  pl.BlockSpec((chunk, wv), lambda b, c: (rows(b, c), 0)),
                  pl.BlockSpec((chunk, LANES), lambda b, c: (rows(b, c), 0)),
                  pl.BlockSpec((LANES, wk), lambda b, c: (0, 0)),
                  pl.BlockSpec((1, wk), lambda b, c: (0, 0)),
                  pl.BlockSpec((1, dv), lambda b, c: (0, 0)),
                  pl.BlockSpec((1, heads, dk, dv), lambda b, c: (b, 0, 0, 0))],
        out_specs=[pl.BlockSpec((chunk, wv), lambda b, c: (b * n_chunks + c, 0)),
                   pl.BlockSpec((1, heads, dk, dv), lambda b, c: (b, 0, 0, 0))],
        out_shape=[jax.ShapeDtypeStruct((n_seq * n_chunks * chunk, wv), out_dtype),
                   jax.ShapeDtypeStruct((n_seq, heads, dk, dv), F32)],
        scratch_shapes=[pltpu.VMEM((heads, dk, dv), F32)],
        compiler_params=_cparams("parallel", "arbitrary"),
        name="gla",
    )(z, z, z, zr, zg, w_gate, b_gate, gla_norm, s0)


def _mm_out_body(a1p_ref, a1s_ref, a2p_ref, a2s_ref, w1_ref, w2_ref, xp_ref, xs_ref, o_ref, *, n_full, rem):
    i = pl.program_id(0)

    def go(a1, a2, x):
        o_ref[...] = x + _dot(a1, w1_ref[...]) + _dot(a2, w2_ref[...])

    @pl.when(i < n_full)
    def _():
        go(a1p_ref[...], a2p_ref[...], xp_ref[...])

    @pl.when(i == n_full)
    def _():
        def cat(p_ref, s_ref):
            return jnp.concatenate([p_ref[:rem], s_ref[...].astype(p_ref.dtype)], axis=0)

        go(cat(a1p_ref, a1s_ref), cat(a2p_ref, a2s_ref), cat(xp_ref, xs_ref))


def _mm_out(a1p, a1s, a2p, a2s, w, xp, xs, tm, tn):
    t_p, k1 = a1p.shape
    t_s = a1s.shape[0]
    k2 = a2p.shape[1]
    n = w.shape[1]
    n_full, rem = divmod(t_p, tm)
    assert rem > 0 and rem + t_s == tm and k1 == k2 and w.shape[0] == k1 + k2

    def pspec(k):
        return pl.BlockSpec((tm, k), lambda i, j: (i, 0))

    def sspec(k):
        return pl.BlockSpec((t_s, k), lambda i, j: (0, 0))

    return pl.pallas_call(
        functools.partial(_mm_out_body, n_full=n_full, rem=rem),
        grid=(n_full + 1, n // tn),
        in_specs=[pspec(k1), sspec(k1), pspec(k2), sspec(k2),
                  pl.BlockSpec((k1, tn), lambda i, j: (0, j)), pl.BlockSpec((k2, tn), lambda i, j: (1, j)),
                  pl.BlockSpec((tm, tn), lambda i, j: (i, j)), pl.BlockSpec((t_s, tn), lambda i, j: (0, j))],
        out_specs=pl.BlockSpec((tm, tn), lambda i, j: (i, j)),
        out_shape=jax.ShapeDtypeStruct((t_p + t_s, n), F32),
        compiler_params=_cparams("parallel", "arbitrary"),
        name="mm_out",
    )(a1p, a1s, a2p, a2s, w, w, xp, xs)


def _rows_iota(shape):
    return lax.broadcasted_iota(jnp.int32, shape, 0).astype(F32)


def _topk_cols(s, ranks, k):
    big = float(2 ** 23)
    for _ in range(k):
        m = jnp.max(s, axis=0, keepdims=True)
        r = jnp.min(jnp.where(s == m, ranks, big), axis=0, keepdims=True)
        sel = ranks == r
        yield m, r, sel
        s = jnp.where(sel, -jnp.inf, s)


def _stack_rows(rows):
    n = len(rows)
    rid = lax.broadcasted_iota(jnp.int32, (n, rows[0].shape[1]), 0)
    out = jnp.broadcast_to(rows[0], (n, rows[0].shape[1]))
    for i in range(1, n):
        out = jnp.where(rid == i, rows[i], out)
    return out


def _route_body(q_ref, k1_ref, k2_ref, ii_ref, jj_ref, g_ref):
    tt = q_ref.shape[0]
    half = PEER_DKEY // 2
    topk = PEER_TOPK
    key_rank = _rows_iota((N_KEYS, tt))
    r8 = _rows_iota((8, tt))
    r16 = _rows_iota((topk, tt))
    gates, experts = [], []
    for h in range(PEER_HEADS):
        qh = q_ref[:, h * PEER_DKEY:(h + 1) * PEER_DKEY]
        qn = (qh * lax.rsqrt(jnp.mean(qh * qh, axis=-1, keepdims=True) + EPS)).astype(BF16)
        s1 = _dot_nt(k1_ref[h], qn[:, :half])
        s2 = _dot_nt(k2_ref[h], qn[:, half:])
        v1, i1 = zip(*[(m, r) for m, r, _ in _topk_cols(s1, key_rank, topk)])
        v2, i2 = zip(*[(m, r) for m, r, _ in _topk_cols(s2, key_rank, topk)])
        v1t, i1t, v2t, i2t = (_stack_rows(list(x)) for x in (v1, i1, v2, i2))
        cand = [v1[0] + v2t]
        code = [i1[0] * N_KEYS + i2t]
        flat = [r16]
        for a in range(1, 8):
            ok = r8 < float(topk // (a + 1))
            cand.append(jnp.where(ok, v1[a] + v2t[:8], -jnp.inf))
            code.append(i1[a] * N_KEYS + i2t[:8])
            flat.append(r8 + float(a * topk))
        cand.append(v1t[8:] + v2[0])
        code.append(i1t[8:] * N_KEYS + i2[0])
        flat.append((r8 + 8.0) * float(topk))
        cand, code, flat = (jnp.concatenate(x, axis=0) for x in (cand, code, flat))
        top, picked = zip(*[(m, r) for m, r, _ in _topk_cols(cand, flat * float(N_KEYS * N_KEYS) + code, topk)])
        ex = [jnp.exp(m - top[0]) for m in top]
        inv_z = 1.0 / functools.reduce(jnp.add, ex)
        gates.append(_stack_rows([e * inv_z for e in ex]))
        experts.append(_stack_rows(list(picked)))
    g_ref[...] = jnp.concatenate(gates, axis=0).T
    e = jnp.concatenate(experts, axis=0).T.astype(jnp.int32)
    e = lax.bitwise_and(e, N_KEYS * N_KEYS - 1)
    ii_ref[...] = lax.shift_right_logical(e, N_KEYS.bit_length() - 1)
    jj_ref[...] = lax.bitwise_and(e, N_KEYS - 1)


def _route(q, keys1, keys2, tt):
    t, n = q.shape
    no = PEER_HEADS * PEER_TOPK
    kspec = pl.BlockSpec(keys1.shape, lambda i: (0, 0, 0))
    ospec = pl.BlockSpec((tt, no), lambda i: (i, 0))
    return pl.pallas_call(
        _route_body,
        grid=(t // tt,),
        in_specs=[pl.BlockSpec((tt, n), lambda i: (i, 0)), kspec, kspec],
        out_specs=[ospec, ospec, ospec],
        out_shape=[jax.ShapeDtypeStruct((t, no), jnp.int32), jax.ShapeDtypeStruct((t, no), jnp.int32),
                   jax.ShapeDtypeStruct((t, no), F32)],
        compiler_params=_cparams("parallel"),
        name="peer_route",
    )(q, keys1, keys2)


EXPAND_GROUP = 16
EXPAND_PITCH = N_KEYS + 8
EXPAND_BUFS = 4


def _expand_body(ii_ref, jj_ref, g_ref, o_ref, *scrs):
    te = ii_ref.shape[0]
    sub = lax.broadcasted_iota(jnp.int32, (N_KEYS, ii_ref.shape[1]), 0)
    per_trip = len(scrs) * EXPAND_GROUP

    def trip(ti, carry):
        bases = [pl.multiple_of(ti * per_trip + n * EXPAND_GROUP, EXPAND_GROUP) for n in range(len(scrs))]
        for base, scr in zip(bases, scrs):
            for u in range(EXPAND_GROUP):
                ii = ii_ref[pl.ds(base + u, 1), :]
                jj = jj_ref[pl.ds(base + u, 1), :]
                g = g_ref[pl.ds(base + u, 1), :]
                a = jnp.where(sub == ii, 1.0, 0.0).astype(BF16)
                b = jnp.where(sub == jj, g, 0.0).astype(BF16)
                scr[u * EXPAND_PITCH:u * EXPAND_PITCH + N_KEYS, :] = _dot_nt(a, b)
        for base, scr in zip(bases, scrs):
            for i in range(N_KEYS):
                rows = scr[pl.ds(i, EXPAND_GROUP, stride=EXPAND_PITCH), :]
                o_ref[pl.ds(base, EXPAND_GROUP), i * N_KEYS:(i + 1) * N_KEYS] = rows.astype(o_ref.dtype)
        return carry

    assert te % per_trip == 0
    lax.fori_loop(0, te // per_trip, trip, 0)


def _expand(ii, jj, g, te):
    t, no = ii.shape
    spec = pl.BlockSpec((te, no), lambda i: (i, 0))
    return pl.pallas_call(
        _expand_body,
        grid=(t // te,),
        in_specs=[spec, spec, spec],
        out_specs=pl.BlockSpec((te, N_KEYS * N_KEYS), lambda i: (i, 0)),
        out_shape=jax.ShapeDtypeStruct((t, N_KEYS * N_KEYS), BF16),
        scratch_shapes=[pltpu.VMEM((EXPAND_GROUP * EXPAND_PITCH, N_KEYS), F32)] * EXPAND_BUFS,
        compiler_params=_cparams("parallel"),
        name="peer_expand",
    )(ii, jj, g)


PEER_SUB = 512


def _peer_a_body(h_ref, u_ref, g_ref, o_ref):
    h = h_ref[...]
    for s in range(o_ref.shape[1] // PEER_SUB):
        cols = slice(s * PEER_SUB, (s + 1) * PEER_SUB)
        a = _dot_nt(h, u_ref[cols, :])
        o_ref[:, cols] = (g_ref[:, cols].astype(F32) * jax.nn.gelu(a)).astype(o_ref.dtype)


def _peer_a(h, u, gates, tm, tn):
    t, k = h.shape
    e = u.shape[0]
    return pl.pallas_call(
        _peer_a_body,
        grid=(t // tm, e // tn),
        in_specs=[pl.BlockSpec((tm, k), lambda i, j: (i, 0)), pl.BlockSpec((tn, k), lambda i, j: (j, 0)),
                  pl.BlockSpec((tm, tn), lambda i, j: (i, j))],
        out_specs=pl.BlockSpec((tm, tn), lambda i, j: (i, j)),
        out_shape=jax.ShapeDtypeStruct((t, e), BF16),
        compiler_params=_cparams("parallel", "arbitrary"),
        name="peer_act",
    )(h, u, gates)


def _peer_v_body(c_ref, v_ref, x_ref, yp_ref, ys_ref, acc_scr, *, n_full, rem):
    i = pl.program_id(0)
    kk = pl.program_id(2)
    last = pl.num_programs(2) - 1

    @pl.when(kk == 0)
    def _():
        acc_scr[...] = x_ref[...]

    acc_scr[...] += _dot(c_ref[...], v_ref[...].astype(BF16))

    @pl.when(kk == last)
    def _():
        yp_ref[...] = acc_scr[...]

    @pl.when((kk == last) & (i == n_full))
    def _():
        ys_ref[...] = acc_scr[rem:, :]


def _peer_v(c, v, x, t_p, tm, tn, tk):
    t, e = c.shape
    n = v.shape[1]
    t_s = t - t_p
    n_full, rem = divmod(t_p, tm)
    assert rem > 0 and rem + t_s == tm
    return pl.pallas_call(
        functools.partial(_peer_v_body, n_full=n_full, rem=rem),
        grid=(t // tm, n // tn, e // tk),
        in_specs=[pl.BlockSpec((tm, tk), lambda i, j, k: (i, k)), pl.BlockSpec((tk, tn), lambda i, j, k: (k, j)),
                  pl.BlockSpec((tm, tn), lambda i, j, k: (i, j))],
        out_specs=[pl.BlockSpec((tm, tn), lambda i, j, k: (i, j)),
                   pl.BlockSpec((t_s, tn), lambda i, j, k: (0, jnp.where(i == n_full, j, 0)))],
        out_shape=[jax.ShapeDtypeStruct((t_p, n), F32), jax.ShapeDtypeStruct((t_s, n), F32)],
        scratch_shapes=[pltpu.VMEM((tm, tn), F32)],
        compiler_params=_cparams("arbitrary", "arbitrary", "arbitrary"),
        name="peer_mix",
    )(c, v, x)


def _token_tile(t, cap):
    best = 16
    for c in range(16, cap + 1, 16):
        if t % c == 0:
            best = c
    return best


def _rope_tables(pos):
    half = DH_A // 2
    inv = ROPE_THETA ** (-jnp.arange(half, dtype=F32) / half)
    ang = pos.astype(F32)[:, None] * inv[None, :]
    cos, sin = jnp.cos(ang), jnp.sin(ang)
    return jnp.concatenate([cos, cos], axis=1), jnp.concatenate([-sin, sin], axis=1)


def kernel(x_prompt, x_sample, cache_attn_k, cache_attn_v, state_gla, norm_mix, w_in, q_norm, k_norm, w_gate, b_gate,
           gla_norm, w_out, norm_ffn, w_peer_q, peer_keys1, peer_keys2, peer_u, peer_v):
    n_p, seq, d_model = x_prompt.shape
    n_s, dec_seq, _ = x_sample.shape
    depth = w_in.shape[0]
    heads_a = cache_attn_k.shape[3]
    w_a = heads_a * DH_A
    _, _, heads_b, dk, dv = state_gla.shape
    w_b = heads_b * dv
    assert seq == DILATED_BRANCHES[-1][0] and heads_b == H_B and cache_attn_k.shape[2] == seq
    t_p = n_p * seq
    t_s = n_s * dec_seq
    t_all = t_p + t_s
    tm = _token_tile(t_all, 1056)
    tt = _token_tile(t_s, 256)

    xp = x_prompt.reshape(t_p, d_model)
    xs = x_sample.reshape(t_s, d_model)
    rope_p = _rope_tables(jnp.arange(seq, dtype=jnp.int32))
    rope_s = _rope_tables(PAST_LEN + jnp.arange(dec_seq, dtype=jnp.int32))

    c_qb = 3 * w_a
    c_kb = c_qb + heads_b * dk
    c_vb = c_kb + heads_b * dk
    c_g = c_vb + w_b
    c_r = c_g + GATE_RANK

    outs = dict(kp=[], vp=[], ks=[], vs=[], sp=[], ss=[])
    for l in range(depth):
        w_in_t = jnp.swapaxes(w_in[l], 0, 1)
        w_gate_p = jnp.pad(w_gate[l].astype(BF16), ((0, LANES - GATE_RANK), (0, 0)))

        if l == 0:
            h = _rmsnorm_pair(xp, xs, norm_mix[l], tt)
        else:
            h = _rmsnorm(x, norm_mix[l], tt)
            xp, xs = x[:t_p], x[t_p:]
        z = _matmul(h, w_in_t, tm, 512, n_cols=c_g, w_is_nk=True)
        zr = _matmul(h, w_in_t, tm, 512, n_cols=w_b, w_is_nk=True, col0=c_r)
        zg = _matmul(h, w_in_t, tm, LANES, n_cols=LANES, w_is_nk=True, col0=c_g)

        o_a_p, k_p, (peer_u_bf16, peer_v_bf16) = _attn_prompt(z, q_norm[l], k_norm[l], *rope_p,
                                                              (peer_u[l], peer_v[l]), n_p, seq, heads_a)
        o_a_s, k_s = _attn_sample(z, q_norm[l], k_norm[l], *rope_s, cache_attn_k[l], cache_attn_v[l], t_p, dec_seq)
        gla_args = (z, zr, zg, w_gate_p, b_gate[l].reshape(1, -1), gla_norm[l].reshape(1, -1))
        gla_cols = dict(col_q=c_qb, col_k=c_kb, col_v=c_vb)
        o_b_p, s_p = _gla(*gla_args, jnp.zeros((n_p, heads_b, dk, dv), F32), row0=0, chunk=GLA_CHUNK,
                          n_chunks=seq // GLA_CHUNK, out_dtype=BF16, **gla_cols)
        o_b_s, s_s = _gla(*gla_args, state_gla[l], row0=t_p, chunk=dec_seq, n_chunks=1, out_dtype=F32, **gla_cols)

        x1 = _mm_out(o_a_p, o_a_s, o_b_p, o_b_s, w_out[l].astype(BF16), xp, xs, tm, 512)

        h2 = _rmsnorm(x1, norm_ffn[l], tt)
        q = _matmul(h2, w_peer_q[l], tm, 512)
        ii, jj, g = _route(q, peer_keys1[l].astype(BF16), peer_keys2[l].astype(BF16), LANES)
        gates = _expand(ii, jj, g, _token_tile(t_all, 128))
        c_act = _peer_a(h2, peer_u_bf16, gates, tm, 2 * PEER_SUB)
        yp, ys = _peer_v(c_act, peer_v_bf16, x1, t_p, tm, 1024, 2048)
        if l + 1 < depth:
            x = jnp.concatenate([yp, ys], axis=0)

        outs["kp"].append(k_p.reshape(n_p, seq, heads_a, DH_A))
        outs["vp"].append(z[:t_p, 2 * w_a:3 * w_a].reshape(n_p, seq, heads_a, DH_A))
        outs["ks"].append(k_s.reshape(n_s, dec_seq, heads_a, DH_A))
        outs["vs"].append(z[t_p:, 2 * w_a:3 * w_a].reshape(n_s, dec_seq, heads_a, DH_A))
        outs["sp"].append(s_p)
        outs["ss"].append(s_s)

    y_prompt = yp.reshape(n_p, seq, d_model)
    y_sample = ys.reshape(n_s, dec_seq, d_model)
    return (y_prompt, y_sample, jnp.stack(outs["kp"]), jnp.stack(outs["vp"]), jnp.stack(outs["ks"]),
            jnp.stack(outs["vs"]), jnp.stack(outs["sp"]), jnp.stack(outs["ss"]))
```

```python
import functools

import numpy as np
import jax
import jax.numpy as jnp
from jax import lax
from jax.experimental import pallas as pl
from jax.experimental.pallas import tpu as pltpu

F32 = jnp.float32
BF16 = jnp.bfloat16

DH_A = 128
DILATED_BRANCHES = ((128, 1), (512, 4), (2048, 16))
ROPE_THETA = 10000.0
PAST_LEN = 8192
H_B = 4
GATE_RANK = 16
GATE_TAU = 16.0
GLA_CHUNK = 64
N_KEYS = 128
PEER_HEADS = 8
PEER_DKEY = 256
PEER_TOPK = 16
EPS = 1e-6
NEG_INF = -1e30

LANES = 128
VMEM_LIMIT = 56 * 1024 * 1024


def _cparams(*sem):
    return pltpu.CompilerParams(dimension_semantics=sem, vmem_limit_bytes=VMEM_LIMIT)


def _dot(a, b):
    return jnp.dot(a, b, preferred_element_type=F32)


def _dot_nt(a, b):
    return lax.dot_general(a, b, (((1,), (1,)), ((), ())), preferred_element_type=F32)


def _dot_tn(a, b):
    return lax.dot_general(a, b, (((0,), (0,)), ((), ())), preferred_element_type=F32)


def _rmsnorm_rows(x, gain):
    ms = jnp.mean(x * x, axis=-1, keepdims=True)
    return (x * lax.rsqrt(ms + EPS) * gain).astype(BF16)


def _rmsnorm_body(x_ref, g_ref, o_ref):
    o_ref[...] = _rmsnorm_rows(x_ref[...], g_ref[...])


def _rmsnorm(x, gain, tt):
    t, d = x.shape
    return pl.pallas_call(
        _rmsnorm_body,
        grid=(t // tt,),
        in_specs=[pl.BlockSpec((tt, d), lambda i: (i, 0)), pl.BlockSpec((1, d), lambda i: (0, 0))],
        out_specs=pl.BlockSpec((tt, d), lambda i: (i, 0)),
        out_shape=jax.ShapeDtypeStruct((t, d), BF16),
        compiler_params=_cparams("parallel"),
        name="rmsnorm",
    )(x, gain.reshape(1, d))


def _rmsnorm_pair_body(xp_ref, xs_ref, g_ref, o_ref, *, n_p):
    i = pl.program_id(0)

    @pl.when(i < n_p)
    def _():
        o_ref[...] = _rmsnorm_rows(xp_ref[...], g_ref[...])

    @pl.when(i >= n_p)
    def _():
        o_ref[...] = _rmsnorm_rows(xs_ref[...], g_ref[...])


def _rmsnorm_pair(xp, xs, gain, tt):
    t_p, d = xp.shape
    t_s = xs.shape[0]
    assert t_p % tt == 0 and t_s % tt == 0
    n_p = t_p // tt
    return pl.pallas_call(
        functools.partial(_rmsnorm_pair_body, n_p=n_p),
        grid=((t_p + t_s) // tt,),
        in_specs=[pl.BlockSpec((tt, d), lambda i: (jnp.minimum(i, n_p - 1), 0)),
                  pl.BlockSpec((tt, d), lambda i: (jnp.maximum(i - n_p, 0), 0)),
                  pl.BlockSpec((1, d), lambda i: (0, 0))],
        out_specs=pl.BlockSpec((tt, d), lambda i: (i, 0)),
        out_shape=jax.ShapeDtypeStruct((t_p + t_s, d), BF16),
        compiler_params=_cparams("arbitrary"),
        name="rmsnorm_in",
    )(xp, xs, gain.reshape(1, d))


def _mm_body(a_ref, w_ref, o_ref, *, w_is_nk):
    w = w_ref[...].astype(BF16)
    o_ref[...] = _dot_nt(a_ref[...], w) if w_is_nk else _dot(a_ref[...], w)


def _matmul(a, w, tm, tn, n_cols=None, w_is_nk=False, col0=0):
    t, k = a.shape
    n = (w.shape[0] if w_is_nk else w.shape[1]) if n_cols is None else n_cols
    if w_is_nk:
        assert col0 % 8 == 0 and tn % 8 == 0
        w_spec = pl.BlockSpec((pl.Element(tn), pl.Element(k)), lambda i, j: (pl.multiple_of(col0 + j * tn, 8), 0))
    else:
        assert col0 == 0
        w_spec = pl.BlockSpec((k, tn), lambda i, j: (0, j))
    return pl.pallas_call(
        functools.partial(_mm_body, w_is_nk=w_is_nk),
        grid=(t // tm, n // tn),
        in_specs=[pl.BlockSpec((tm, k), lambda i, j: (i, 0)), w_spec],
        out_specs=pl.BlockSpec((tm, tn), lambda i, j: (i, j)),
        out_shape=jax.ShapeDtypeStruct((t, n), F32),
        compiler_params=_cparams("parallel", "arbitrary"),
        name="matmul",
    )(a, w)


def _norm_rope(x, gain, cos2, sin2):
    ms = jnp.mean(x * x, axis=-1, keepdims=True)
    y = x * lax.rsqrt(ms + EPS) * gain
    return y * cos2 + pltpu.roll(y, DH_A // 2, axis=1) * sin2


ATTN_GROUP = 8


def _attn_prompt_body(q_ref, k_ref, v_ref, gq_ref, gk_ref, cos_ref, sin_ref, *rest, seq, n_side):
    side_in, o_ref, kr_ref = rest[:n_side], rest[n_side], rest[n_side + 1]
    side_out, scr = rest[n_side + 2:2 * n_side + 2], rest[2 * n_side + 2:]
    for src, dst in zip(side_in, side_out):
        dst[...] = src[...].astype(dst.dtype)
    blk = DH_A
    n_br = len(DILATED_BRANCHES)
    qs_scr, ks_scr, scr = scr[0], scr[1], scr[2:]
    m_scr, l_scr, acc_scr = scr[:n_br], scr[n_br:2 * n_br], scr[2 * n_br:]

    prep_rows = 4 * blk

    def prep(c, carry):
        rows = pl.ds(pl.multiple_of(c * prep_rows, prep_rows), prep_rows)
        cos2, sin2 = cos_ref[rows, :], sin_ref[rows, :]
        qs_scr[rows, :] = _norm_rope(q_ref[rows, :], gq_ref[...], cos2, sin2) * (DH_A ** -0.5)
        kr = _norm_rope(k_ref[rows, :], gk_ref[...], cos2, sin2)
        ks_scr[rows, :] = kr
        kr_ref[rows, :] = kr
        return carry

    lax.fori_loop(0, seq // prep_rows, prep, 0)
    qi = lax.broadcasted_iota(jnp.int32, (blk, 2 * blk), 0)
    kc = lax.broadcasted_iota(jnp.int32, (blk, 2 * blk), 1)
    band = (kc >= qi) & (kc <= qi + blk)
    cur_half = kc >= blk
    qi1 = lax.broadcasted_iota(jnp.int32, (blk, blk), 0)
    causal = lax.broadcasted_iota(jnp.int32, (blk, blk), 1) <= qi1
    floor_tile = 2.0 * NEG_INF - qi1.astype(F32)

    for bi, (window, d) in enumerate(DILATED_BRANCHES):
        assert window // d == blk
        nb = seq // d // blk

        def blocks(it, carry, bi=bi, d=d, nb=nb):
            rows, vvs, ss = [], [], []
            for u in range(ATTN_GROUP):
                idx = it * ATTN_GROUP + u
                if nb == 1:
                    rw = pl.ds(idx, blk, stride=d)
                    kk = ks_scr[rw, :].astype(BF16)
                    vv = v_ref[rw, :].astype(BF16)
                    mask = causal
                else:
                    r = idx // nb
                    j = idx % nb
                    start = r + j * (blk * d)
                    rw = pl.ds(start, blk, stride=d)
                    prows = pl.ds(jnp.maximum(start - blk * d, r), blk, stride=d)
                    kk = jnp.concatenate([ks_scr[prows, :], ks_scr[rw, :]], axis=0).astype(BF16)
                    vv = jnp.concatenate([v_ref[prows, :], v_ref[rw, :]], axis=0).astype(BF16)
                    mask = band & (cur_half | (j > 0))
                q = qs_scr[rw, :].astype(BF16)
                rows.append(rw)
                vvs.append(vv)
                ss.append(jnp.where(mask, _dot_nt(q, kk), NEG_INF))
            ms = [jnp.max(s, axis=-1, keepdims=True) for s in ss]
            ps = [jnp.exp(s - m) for s, m in zip(ss, ms)]
            for rw, m, p, vv in zip(rows, ms, ps, vvs):
                m_scr[bi][rw, :] = jnp.maximum(m, floor_tile)
                l_scr[bi][rw, :] = jnp.maximum(jnp.sum(p, axis=-1, keepdims=True), floor_tile)
                acc_scr[bi][rw, :] = _dot(p.astype(BF16), vv)
            return carry

        assert (d * nb) % ATTN_GROUP == 0
        lax.fori_loop(0, d * nb // ATTN_GROUP, blocks, 0)

    def merge(c, carry):
        rows = pl.ds(pl.multiple_of(c * blk, blk), blk)
        ms = [m[rows, :] for m in m_scr]
        m_max = functools.reduce(jnp.maximum, ms)
        ws = [jnp.exp(m - m_max) for m in ms]
        num = functools.reduce(jnp.add, [w * a[rows, :] for w, a in zip(ws, acc_scr)])
        den = functools.reduce(jnp.add, [w * l[rows, :] for w, l in zip(ws, l_scr)])
        o_ref[rows, :] = (num / den).astype(o_ref.dtype)
        return carry

    lax.fori_loop(0, seq // blk, merge, 0)


def _attn_prompt(z, q_gain, k_gain, cos2, sin2, sides, n_seq, seq, heads):
    steps = n_seq * heads
    assert all(s.shape[0] % steps == 0 for s in sides)
    side_specs = [pl.BlockSpec((s.shape[0] // steps, s.shape[1]), lambda b, h: (b * heads + h, 0)) for s in sides]
    head_spec = pl.BlockSpec((seq, DH_A), lambda b, h: (b, h))
    vec_spec = pl.BlockSpec((1, DH_A), lambda b, h: (0, 0))
    tab_spec = pl.BlockSpec((seq, DH_A), lambda b, h: (0, 0))
    outs = pl.pallas_call(
        functools.partial(_attn_prompt_body, seq=seq, n_side=len(sides)),
        grid=(n_seq, heads),
        in_specs=[head_spec,
                  pl.BlockSpec((seq, DH_A), lambda b, h: (b, heads + h)),
                  pl.BlockSpec((seq, DH_A), lambda b, h: (b, 2 * heads + h)),
                  vec_spec, vec_spec, tab_spec, tab_spec] + side_specs,
        out_specs=[head_spec, head_spec] + side_specs,
        out_shape=[jax.ShapeDtypeStruct((n_seq * seq, heads * DH_A), BF16),
                   jax.ShapeDtypeStruct((n_seq * seq, heads * DH_A), F32)]
        + [jax.ShapeDtypeStruct(s.shape, BF16) for s in sides],
        scratch_shapes=[pltpu.VMEM((seq, DH_A), F32)] * (2 + 3 * len(DILATED_BRANCHES)),
        compiler_params=_cparams("parallel", "arbitrary"),
        name="attn_prompt",
    )(z, z, z, q_gain.reshape(1, DH_A), k_gain.reshape(1, DH_A), cos2, sin2, *sides)
    return outs[0], outs[1], outs[2:]


SAMPLE_CHUNK_ROWS = 4096
SAMPLE_STREAMS = 2


def _sample_plan(win_buf, dec_seq, heads):
    dil = DILATED_BRANCHES[-1][1]
    dense_from = win_buf - DILATED_BRANCHES[-2][0]
    assert win_buf % dil == 0 and dense_from % dil == 0 and dec_seq * 2 == dil and dec_seq * heads == LANES
    pos_sparse = SAMPLE_CHUNK_ROWS // (dec_seq * heads) * dil
    pos_dense = SAMPLE_CHUNK_ROWS // heads
    assert dense_from % pos_sparse == 0 and (win_buf - dense_from) % pos_dense == 0 and dense_from % pos_dense == 0
    return dil, dense_from, dense_from // pos_sparse, (win_buf - dense_from) // pos_dense


def _sample_weights(win_buf, dec_seq, heads):
    dil, dense_from, n_sparse, n_dense = _sample_plan(win_buf, dec_seq, heads)
    n = np.arange(win_buf + dec_seq)[None, :]
    s = np.arange(dec_seq)[:, None]
    dist = win_buf + s - n
    cnt = np.zeros((dec_seq, win_buf + dec_seq), np.float32)
    for window, d in DILATED_BRANCHES:
        cnt += (dist >= 0) & (dist <= window) & (dist % d == 0)
    assert not cnt[:, :dense_from].reshape(dec_seq, -1, dil)[:, :, dec_seq:].any()
    eye = np.eye(heads, dtype=np.float32)
    sparse = cnt[:, :dense_from].reshape(dec_seq, -1, dil)[:, :, :dec_seq].reshape(dec_seq, -1)
    sparse = np.einsum("sn,hg->nhgs", sparse, eye).reshape(n_sparse, SAMPLE_CHUNK_ROWS, LANES)
    dense = np.einsum("sn,hg->nhgs", cnt[:, dense_from:win_buf], eye).reshape(n_dense, SAMPLE_CHUNK_ROWS, LANES)
    new = np.einsum("sn,hg->hngs", cnt[:, win_buf:], eye).reshape(LANES, LANES)
    return np.concatenate([sparse, dense], axis=0), new


def _attn_sample_body(q_ref, ks_ref, kd_ref, vs_ref, vd_ref, kn_ref, vn_ref, w_ref, wn_ref, gq_ref, gk_ref,
                      cos_ref, sin_ref, o_ref, kr_ref, q_scr, m_scr, l_scr, acc_scr,
                      *, heads, dec_seq, n_sparse, n_dense):
    c = pl.program_id(1)
    n_chunks = n_sparse + n_dense

    def by_head(ref, fn=lambda x: x):
        return jnp.concatenate([fn(ref[:, h * DH_A:(h + 1) * DH_A]) for h in range(heads)], axis=0)

    def rotated(gain_ref):
        return lambda x: _norm_rope(x, gain_ref[...], cos_ref[...], sin_ref[...])

    @pl.when(c == 0)
    def _():
        q_scr[...] = (by_head(q_ref, rotated(gq_ref)) * (DH_A ** -0.5)).astype(BF16)
        m_scr[...] = jnp.full(m_scr.shape, NEG_INF, F32)
        l_scr[...] = jnp.zeros(l_scr.shape, F32)
        acc_scr[...] = jnp.zeros(acc_scr.shape, F32)

    def step(kb, vb, w, n_streams):
        rows = kb.shape[0] // n_streams
        part = [slice(t * rows, (t + 1) * rows) for t in range(n_streams)]
        s = [jnp.where(w[r] > 0, _dot_nt(kb[r], q_scr[...]), NEG_INF) for r in part]
        m_old = [m_scr[t] for t in range(n_streams)]
        m_new = [jnp.maximum(m_old[t], jnp.max(s[t], axis=0, keepdims=True)) for t in range(n_streams)]
        alpha = [jnp.exp(m_old[t] - m_new[t]) for t in range(n_streams)]
        p = [w[part[t]] * jnp.exp(s[t] - m_new[t]) for t in range(n_streams)]
        for t in range(n_streams):
            l_scr[t] = alpha[t] * l_scr[t] + jnp.sum(p[t], axis=0, keepdims=True)
            acc_scr[t] = alpha[t] * acc_scr[t] + _dot_tn(vb[part[t]], p[t].astype(BF16))
            m_scr[t] = m_new[t]

    @pl.when(c < n_sparse)
    def _():
        rows = (SAMPLE_CHUNK_ROWS, DH_A)
        step(ks_ref[0].reshape(rows).astype(BF16), vs_ref[0].reshape(rows).astype(BF16), w_ref[c], SAMPLE_STREAMS)

    @pl.when((c >= n_sparse) & (c < n_chunks))
    def _():
        step(kd_ref[0].astype(BF16), vd_ref[0].astype(BF16), w_ref[c], SAMPLE_STREAMS)

    @pl.when(c == n_chunks)
    def _():
        kn = by_head(kn_ref, rotated(gk_ref))
        step(kn.astype(BF16), by_head(vn_ref).astype(BF16), wn_ref[...], 1)
        ms = [m_scr[t] for t in range(SAMPLE_STREAMS)]
        m_all = functools.reduce(jnp.maximum, ms)
        wt = [jnp.exp(m - m_all) for m in ms]
        num = functools.reduce(jnp.add, [wt[t] * acc_scr[t] for t in range(SAMPLE_STREAMS)])
        den = functools.reduce(jnp.add, [wt[t] * l_scr[t] for t in range(SAMPLE_STREAMS)])
        out = (num / den).T
        for h in range(heads):
            o_ref[:, h * DH_A:(h + 1) * DH_A] = out[h * dec_seq:(h + 1) * dec_seq, :]
            kr_ref[:, h * DH_A:(h + 1) * DH_A] = kn[h * dec_seq:(h + 1) * dec_seq, :]


def _attn_sample(z, q_gain, k_gain, cos2, sin2, cache_k, cache_v, row0, dec_seq):
    n_seq, win_buf, heads, _ = cache_k.shape
    width = heads * DH_A
    dil, dense_from, n_sparse, n_dense = _sample_plan(win_buf, dec_seq, heads)
    n_chunks = n_sparse + n_dense
    w_main, w_new = _sample_weights(win_buf, dec_seq, heads)
    rb = row0 // dec_seq
    groups = SAMPLE_CHUNK_ROWS // LANES
    dense0 = dense_from * heads // SAMPLE_CHUNK_ROWS

    def new_spec(col):
        return pl.BlockSpec((dec_seq, width), lambda b, c: (rb + b, col))

    def small(shape):
        return pl.BlockSpec(shape, lambda b, c: (0, 0))

    out_spec = pl.BlockSpec((dec_seq, width), lambda b, c: (b, 0))

    sparse_view = lambda x: x.reshape(n_seq, win_buf // dil, dil * heads, DH_A)
    dense_view = lambda x: x.reshape(n_seq, win_buf * heads, DH_A)
    sparse_spec = pl.BlockSpec((1, groups, LANES, DH_A), lambda b, c: (b, jnp.minimum(c, n_sparse - 1), 0, 0))
    dense_spec = pl.BlockSpec((1, SAMPLE_CHUNK_ROWS, DH_A),
                              lambda b, c: (b, dense0 + jnp.clip(c - n_sparse, 0, n_dense - 1), 0))
    return pl.pallas_call(
        functools.partial(_attn_sample_body, heads=heads, dec_seq=dec_seq, n_sparse=n_sparse, n_dense=n_dense),
        grid=(n_seq, n_chunks + 1),
        in_specs=[new_spec(0), sparse_spec, dense_spec, sparse_spec, dense_spec, new_spec(1), new_spec(2),
                  pl.BlockSpec(w_main.shape, lambda b, c: (0, 0, 0)),
                  pl.BlockSpec(w_new.shape, lambda b, c: (0, 0)),
                  small((1, DH_A)), small((1, DH_A)), small((dec_seq, DH_A)), small((dec_seq, DH_A))],
        out_specs=[out_spec, out_spec],
        out_shape=[jax.ShapeDtypeStruct((n_seq * dec_seq, width), F32)] * 2,
        scratch_shapes=[pltpu.VMEM((LANES, DH_A), BF16), pltpu.VMEM((SAMPLE_STREAMS, 1, LANES), F32),
                        pltpu.VMEM((SAMPLE_STREAMS, 1, LANES), F32), pltpu.VMEM((SAMPLE_STREAMS, DH_A, LANES), F32)],
        compiler_params=_cparams("parallel", "arbitrary"),
        name="attn_sample",
    )(z, sparse_view(cache_k), dense_view(cache_k), sparse_view(cache_v), dense_view(cache_v), z, z,
      jnp.asarray(w_main), jnp.asarray(w_new), q_gain.reshape(1, DH_A), k_gain.reshape(1, DH_A),
      cos2, sin2)


def _gla_body(q_ref, k_ref, v_ref, r_ref, gl_ref, wg_ref, bg_ref, gn_ref, s0_ref, o_ref, sn_ref, st_scr,
              *, chunk, heads, dk, dv):
    c = pl.program_id(1)
    hs = range(heads)

    @pl.when(c == 0)
    def _():
        for h in hs:
            st_scr[h] = s0_ref[0, h]

    cp = max(chunk, LANES // 2)

    def pad(a):
        return a if cp == chunk else jnp.concatenate([a, jnp.zeros((cp - chunk, a.shape[1]), a.dtype)], axis=0)

    def cols(ref, h, w):
        return ref[:, h * w:(h + 1) * w]

    ti = lax.broadcasted_iota(jnp.int32, (cp, cp), 0)
    tj = lax.broadcasted_iota(jnp.int32, (cp, cp), 1)
    tril = ti >= tj
    trilf = tril.astype(F32)
    glow = gl_ref[...].astype(BF16)
    log_a = [pad(jax.nn.log_sigmoid(_dot(glow, cols(wg_ref, h, dk)) + cols(bg_ref, h, dk)) / GATE_TAU) for h in hs]
    b = [jnp.dot(trilf, la, preferred_element_type=F32, precision=lax.Precision.HIGHEST) for la in log_a]
    b_last = [x[cp - 1:cp, :] for x in b]
    k = [pad(cols(k_ref, h, dk)) for h in hs]
    v = [pad(cols(v_ref, h, dv)).astype(BF16) for h in hs]
    q_in = [(pad(cols(q_ref, h, dk)) * (dk ** -0.5) * jnp.exp(b[h])).astype(BF16) for h in hs]
    k_in = [(k[h] * jnp.exp(-b[h])).astype(BF16) for h in hs]
    aug_t = [jnp.concatenate([k[h] * jnp.exp(b_last[h] - b[h]), jnp.broadcast_to(jnp.exp(b_last[h]), (cp, dk))],
                             axis=0).T for h in hs]
    att = [jnp.where(tril, _dot_nt(q_in[h], k_in[h]), 0.0).astype(BF16) for h in hs]
    st = [st_scr[h] for h in hs]
    o = [(_dot(att[h], v[h]) + _dot(q_in[h], st[h].astype(BF16)))[:chunk] for h in hs]
    st_new = [st[h] * aug_t[h][:, cp:cp + 1] + _dot(aug_t[h][:, :cp].astype(BF16), v[h]) for h in hs]
    for h in hs:
        st_scr[h] = st_new[h]
        ms = jnp.mean(o[h] * o[h], axis=-1, keepdims=True)
        gated = (o[h] * lax.rsqrt(ms + EPS) * gn_ref[...]) * jax.nn.silu(cols(r_ref, h, dv))
        o_ref[:, h * dv:(h + 1) * dv] = gated.astype(o_ref.dtype)

    @pl.when(c == pl.num_programs(1) - 1)
    def _():
        for h in hs:
            sn_ref[0, h] = st_new[h]


def _gla(z, zr, zg, w_gate, b_gate, gla_norm, s0, *, row0, chunk, n_chunks, col_q, col_k, col_v, out_dtype):
    n_seq, heads, dk, dv = s0.shape
    rb = row0 // chunk
    wk, wv = heads * dk, heads * dv
    assert col_q % wk == 0 and col_k % wk == 0 and col_v % wv == 0 and zr.shape[1] == wv

    def rows(b, c):
        return rb + b * n_chunks + c

    return pl.pallas_call(
        functools.partial(_gla_body, chunk=chunk, heads=heads, dk=dk, dv=dv),
        grid=(n_seq, n_chunks),
        in_specs=[pl.BlockSpec((chunk, wk), lambda b, c: (rows(b, c), col_q // wk)),
                  pl.BlockSpec((chunk, wk), lambda b, c: (rows(b, c), col_k // wk)),
                  pl.BlockSpec((chunk, wv), lambda b, c: (rows(b, c), col_v // wv)),
                  pl.BlockSpec((chunk, wv), lambda b, c: (rows(b, c), 0)),
                  pl.BlockSpec((chunk, LANES), lambda b, c: (rows(b, c), 0)),
                  pl.BlockSpec((LANES, wk), lambda b, c: (0, 0)),
                  pl.BlockSpec((1, wk), lambda b, c: (0, 0)),
                  pl.BlockSpec((1, dv), lambda b, c: (0, 0)),
                  pl.BlockSpec((1, heads, dk, dv), lambda b, c: (b, 0, 0, 0))],
        out_specs=[pl.BlockSpec((chunk, wv), lambda b, c: (b * n_chunks + c, 0)),
                   pl.BlockSpec((1, heads, dk, dv), lambda b, c: (b, 0, 0, 0))],
        out_shape=[jax.ShapeDtypeStruct((n_seq * n_chunks * chunk, wv), out_dtype),
                   jax.ShapeDtypeStruct((n_seq, heads, dk, dv), F32)],
        scratch_shapes=[pltpu.VMEM((heads, dk, dv), F32)],
        compiler_params=_cparams("parallel", "arbitrary"),
        name="gla",
    )(z, z, z, zr, zg, w_gate, b_gate, gla_norm, s0)


def _mm_out_body(a1p_ref, a1s_ref, a2p_ref, a2s_ref, w1_ref, w2_ref, xp_ref, xs_ref, o_ref, *, n_full, rem):
    i = pl.program_id(0)

    def go(a1, a2, x):
        o_ref[...] = x + _dot(a1, w1_ref[...]) + _dot(a2, w2_ref[...])

    @pl.when(i < n_full)
    def _():
        go(a1p_ref[...], a2p_ref[...], xp_ref[...])

    @pl.when(i == n_full)
    def _():
        def cat(p_ref, s_ref):
            return jnp.concatenate([p_ref[:rem], s_ref[...].astype(p_ref.dtype)], axis=0)

        go(cat(a1p_ref, a1s_ref), cat(a2p_ref, a2s_ref), cat(xp_ref, xs_ref))


def _mm_out(a1p, a1s, a2p, a2s, w, xp, xs, tm, tn):
    t_p, k1 = a1p.shape
    t_s = a1s.shape[0]
    k2 = a2p.shape[1]
    n = w.shape[1]
    n_full, rem = divmod(t_p, tm)
    assert rem > 0 and rem + t_s == tm and k1 == k2 and w.shape[0] == k1 + k2

    def pspec(k):
        return pl.BlockSpec((tm, k), lambda i, j: (i, 0))

    def sspec(k):
        return pl.BlockSpec((t_s, k), lambda i, j: (0, 0))

    return pl.pallas_call(
        functools.partial(_mm_out_body, n_full=n_full, rem=rem),
        grid=(n_full + 1, n // tn),
        in_specs=[pspec(k1), sspec(k1), pspec(k2), sspec(k2),
                  pl.BlockSpec((k1, tn), lambda i, j: (0, j)), pl.BlockSpec((k2, tn), lambda i, j: (1, j)),
                  pl.BlockSpec((tm, tn), lambda i, j: (i, j)), pl.BlockSpec((t_s, tn), lambda i, j: (0, j))],
        out_specs=pl.BlockSpec((tm, tn), lambda i, j: (i, j)),
        out_shape=jax.ShapeDtypeStruct((t_p + t_s, n), F32),
        compiler_params=_cparams("parallel", "arbitrary"),
        name="mm_out",
    )(a1p, a1s, a2p, a2s, w, w, xp, xs)


def _rows_iota(shape):
    return lax.broadcasted_iota(jnp.int32, shape, 0).astype(F32)


def _topk_cols(s, ranks, k):
    big = float(2 ** 23)
    for _ in range(k):
        m = jnp.max(s, axis=0, keepdims=True)
        r = jnp.min(jnp.where(s == m, ranks, big), axis=0, keepdims=True)
        sel = ranks == r
        yield m, r, sel
        s = jnp.where(sel, -jnp.inf, s)


def _stack_rows(rows):
    n = len(rows)
    rid = lax.broadcasted_iota(jnp.int32, (n, rows[0].shape[1]), 0)
    out = jnp.broadcast_to(rows[0], (n, rows[0].shape[1]))
    for i in range(1, n):
        out = jnp.where(rid == i, rows[i], out)
    return out


def _route_body(q_ref, k1_ref, k2_ref, ii_ref, jj_ref, g_ref):
    tt = q_ref.shape[0]
    half = PEER_DKEY // 2
    topk = PEER_TOPK
    key_rank = _rows_iota((N_KEYS, tt))
    r8 = _rows_iota((8, tt))
    r16 = _rows_iota((topk, tt))
    gates, experts = [], []
    for h in range(PEER_HEADS):
        qh = q_ref[:, h * PEER_DKEY:(h + 1) * PEER_DKEY]
        qn = (qh * lax.rsqrt(jnp.mean(qh * qh, axis=-1, keepdims=True) + EPS)).astype(BF16)
        s1 = _dot_nt(k1_ref[h], qn[:, :half])
        s2 = _dot_nt(k2_ref[h], qn[:, half:])
        v1, i1 = zip(*[(m, r) for m, r, _ in _topk_cols(s1, key_rank, topk)])
        v2, i2 = zip(*[(m, r) for m, r, _ in _topk_cols(s2, key_rank, topk)])
        v1t, i1t, v2t, i2t = (_stack_rows(list(x)) for x in (v1, i1, v2, i2))
        cand = [v1[0] + v2t]
        code = [i1[0] * N_KEYS + i2t]
        flat = [r16]
        for a in range(1, 8):
            ok = r8 < float(topk // (a + 1))
            cand.append(jnp.where(ok, v1[a] + v2t[:8], -jnp.inf))
            code.append(i1[a] * N_KEYS + i2t[:8])
            flat.append(r8 + float(a * topk))
        cand.append(v1t[8:] + v2[0])
        code.append(i1t[8:] * N_KEYS + i2[0])
        flat.append((r8 + 8.0) * float(topk))
        cand, code, flat = (jnp.concatenate(x, axis=0) for x in (cand, code, flat))
        top, picked = zip(*[(m, r) for m, r, _ in _topk_cols(cand, flat * float(N_KEYS * N_KEYS) + code, topk)])
        ex = [jnp.exp(m - top[0]) for m in top]
        inv_z = 1.0 / functools.reduce(jnp.add, ex)
        gates.append(_stack_rows([e * inv_z for e in ex]))
        experts.append(_stack_rows(list(picked)))
    g_ref[...] = jnp.concatenate(gates, axis=0).T
    e = jnp.concatenate(experts, axis=0).T.astype(jnp.int32)
    e = lax.bitwise_and(e, N_KEYS * N_KEYS - 1)
    ii_ref[...] = lax.shift_right_logical(e, N_KEYS.bit_length() - 1)
    jj_ref[...] = lax.bitwise_and(e, N_KEYS - 1)


def _route(q, keys1, keys2, tt):
    t, n = q.shape
    no = PEER_HEADS * PEER_TOPK
    kspec = pl.BlockSpec(keys1.shape, lambda i: (0, 0, 0))
    ospec = pl.BlockSpec((tt, no), lambda i: (i, 0))
    return pl.pallas_call(
        _route_body,
        grid=(t // tt,),
        in_specs=[pl.BlockSpec((tt, n), lambda i: (i, 0)), kspec, kspec],
        out_specs=[ospec, ospec, ospec],
        out_shape=[jax.ShapeDtypeStruct((t, no), jnp.int32), jax.ShapeDtypeStruct((t, no), jnp.int32),
                   jax.ShapeDtypeStruct((t, no), F32)],
        compiler_params=_cparams("parallel"),
        name="peer_route",
    )(q, keys1, keys2)


EXPAND_GROUP = 16
EXPAND_PITCH = N_KEYS + 8
EXPAND_BUFS = 4


def _expand_body(ii_ref, jj_ref, g_ref, o_ref, *scrs):
    te = ii_ref.shape[0]
    sub = lax.broadcasted_iota(jnp.int32, (N_KEYS, ii_ref.shape[1]), 0)
    per_trip = len(scrs) * EXPAND_GROUP

    def trip(ti, carry):
        bases = [pl.multiple_of(ti * per_trip + n * EXPAND_GROUP, EXPAND_GROUP) for n in range(len(scrs))]
        for base, scr in zip(bases, scrs):
            for u in range(EXPAND_GROUP):
                ii = ii_ref[pl.ds(base + u, 1), :]
                jj = jj_ref[pl.ds(base + u, 1), :]
                g = g_ref[pl.ds(base + u, 1), :]
                a = jnp.where(sub == ii, 1.0, 0.0).astype(BF16)
                b = jnp.where(sub == jj, g, 0.0).astype(BF16)
                scr[u * EXPAND_PITCH:u * EXPAND_PITCH + N_KEYS, :] = _dot_nt(a, b)
        for base, scr in zip(bases, scrs):
            for i in range(N_KEYS):
                rows = scr[pl.ds(i, EXPAND_GROUP, stride=EXPAND_PITCH), :]
                o_ref[pl.ds(base, EXPAND_GROUP), i * N_KEYS:(i + 1) * N_KEYS] = rows.astype(o_ref.dtype)
        return carry

    assert te % per_trip == 0
    lax.fori_loop(0, te // per_trip, trip, 0)


def _expand(ii, jj, g, te):
    t, no = ii.shape
    spec = pl.BlockSpec((te, no), lambda i: (i, 0))
    return pl.pallas_call(
        _expand_body,
        grid=(t // te,),
        in_specs=[spec, spec, spec],
        out_specs=pl.BlockSpec((te, N_KEYS * N_KEYS), lambda i: (i, 0)),
        out_shape=jax.ShapeDtypeStruct((t, N_KEYS * N_KEYS), BF16),
        scratch_shapes=[pltpu.VMEM((EXPAND_GROUP * EXPAND_PITCH, N_KEYS), F32)] * EXPAND_BUFS,
        compiler_params=_cparams("parallel"),
        name="peer_expand",
    )(ii, jj, g)


PEER_SUB = 512


def _peer_a_body(h_ref, u_ref, g_ref, o_ref):
    h = h_ref[...]
    for s in range(o_ref.shape[1] // PEER_SUB):
        cols = slice(s * PEER_SUB, (s + 1) * PEER_SUB)
        a = _dot_nt(h, u_ref[cols, :])
        o_ref[:, cols] = (g_ref[:, cols].astype(F32) * jax.nn.gelu(a)).astype(o_ref.dtype)


def _peer_a(h, u, gates, tm, tn):
    t, k = h.shape
    e = u.shape[0]
    return pl.pallas_call(
        _peer_a_body,
        grid=(t // tm, e // tn),
        in_specs=[pl.BlockSpec((tm, k), lambda i, j: (i, 0)), pl.BlockSpec((tn, k), lambda i, j: (j, 0)),
                  pl.BlockSpec((tm, tn), lambda i, j: (i, j))],
        out_specs=pl.BlockSpec((tm, tn), lambda i, j: (i, j)),
        out_shape=jax.ShapeDtypeStruct((t, e), BF16),
        compiler_params=_cparams("parallel", "arbitrary"),
        name="peer_act",
    )(h, u, gates)


def _peer_v_body(c_ref, v_ref, x_ref, yp_ref, ys_ref, acc_scr, *, n_full, rem):
    i = pl.program_id(0)
    kk = pl.program_id(2)
    last = pl.num_programs(2) - 1

    @pl.when(kk == 0)
    def _():
        acc_scr[...] = x_ref[...]

    acc_scr[...] += _dot(c_ref[...], v_ref[...].astype(BF16))

    @pl.when(kk == last)
    def _():
        yp_ref[...] = acc_scr[...]

    @pl.when((kk == last) & (i == n_full))
    def _():
        ys_ref[...] = acc_scr[rem:, :]


def _peer_v(c, v, x, t_p, tm, tn, tk):
    t, e = c.shape
    n = v.shape[1]
    t_s = t - t_p
    n_full, rem = divmod(t_p, tm)
    assert rem > 0 and rem + t_s == tm
    return pl.pallas_call(
        functools.partial(_peer_v_body, n_full=n_full, rem=rem),
        grid=(t // tm, n // tn, e // tk),
        in_specs=[pl.BlockSpec((tm, tk), lambda i, j, k: (i, k)), pl.BlockSpec((tk, tn), lambda i, j, k: (k, j)),
                  pl.BlockSpec((tm, tn), lambda i, j, k: (i, j))],
        out_specs=[pl.BlockSpec((tm, tn), lambda i, j, k: (i, j)),
                   pl.BlockSpec((t_s, tn), lambda i, j, k: (0, jnp.where(i == n_full, j, 0)))],
        out_shape=[jax.ShapeDtypeStruct((t_p, n), F32), jax.ShapeDtypeStruct((t_s, n), F32)],
        scratch_shapes=[pltpu.VMEM((tm, tn), F32)],
        compiler_params=_cparams("arbitrary", "arbitrary", "arbitrary"),
        name="peer_mix",
    )(c, v, x)


def _token_tile(t, cap):
    best = 16
    for c in range(16, cap + 1, 16):
        if t % c == 0:
            best = c
    return best


def _rope_tables(pos):
    half = DH_A // 2
    inv = ROPE_THETA ** (-jnp.arange(half, dtype=F32) / half)
    ang = pos.astype(F32)[:, None] * inv[None, :]
    cos, sin = jnp.cos(ang), jnp.sin(ang)
    return jnp.concatenate([cos, cos], axis=1), jnp.concatenate([-sin, sin], axis=1)


def kernel(x_prompt, x_sample, cache_attn_k, cache_attn_v, state_gla, norm_mix, w_in, q_norm, k_norm, w_gate, b_gate,
           gla_norm, w_out, norm_ffn, w_peer_q, peer_keys1, peer_keys2, peer_u, peer_v):
    n_p, seq, d_model = x_prompt.shape
    n_s, dec_seq, _ = x_sample.shape
    depth = w_in.shape[0]
    heads_a = cache_attn_k.shape[3]
    w_a = heads_a * DH_A
    _, _, heads_b, dk, dv = state_gla.shape
    w_b = heads_b * dv
    assert seq == DILATED_BRANCHES[-1][0] and heads_b == H_B and cache_attn_k.shape[2] == seq
    t_p = n_p * seq
    t_s = n_s * dec_seq
    t_all = t_p + t_s
    tm = _token_tile(t_all, 1056)
    tt = _token_tile(t_s, 256)

    xp = x_prompt.reshape(t_p, d_model)
    xs = x_sample.reshape(t_s, d_model)
    rope_p = _rope_tables(jnp.arange(seq, dtype=jnp.int32))
    rope_s = _rope_tables(PAST_LEN + jnp.arange(dec_seq, dtype=jnp.int32))

    c_qb = 3 * w_a
    c_kb = c_qb + heads_b * dk
    c_vb = c_kb + heads_b * dk
    c_g = c_vb + w_b
    c_r = c_g + GATE_RANK

    outs = dict(kp=[], vp=[], ks=[], vs=[], sp=[], ss=[])
    for l in range(depth):
        w_in_t = jnp.swapaxes(w_in[l], 0, 1)
        w_gate_p = jnp.pad(w_gate[l].astype(BF16), ((0, LANES - GATE_RANK), (0, 0)))

        if l == 0:
            h = _rmsnorm_pair(xp, xs, norm_mix[l], tt)
        else:
            h = _rmsnorm(x, norm_mix[l], tt)
            xp, xs = x[:t_p], x[t_p:]
        z = _matmul(h, w_in_t, tm, 512, n_cols=c_g, w_is_nk=True)
        zr = _matmul(h, w_in_t, tm, 512, n_cols=w_b, w_is_nk=True, col0=c_r)
        zg = _matmul(h, w_in_t, tm, LANES, n_cols=LANES, w_is_nk=True, col0=c_g)

        o_a_p, k_p, (peer_u_bf16, peer_v_bf16, w_out_bf16) = _attn_prompt(
            z, q_norm[l], k_norm[l], *rope_p, (peer_u[l], peer_v[l], w_out[l]), n_p, seq, heads_a)
        o_a_s, k_s = _attn_sample(z, q_norm[l], k_norm[l], *rope_s, cache_attn_k[l], cache_attn_v[l], t_p, dec_seq)
        gla_args = (z, zr, zg, w_gate_p, b_gate[l].reshape(1, -1), gla_norm[l].reshape(1, -1))
        gla_cols = dict(col_q=c_qb, col_k=c_kb, col_v=c_vb)
        o_b_p, s_p = _gla(*gla_args, jnp.zeros((n_p, heads_b, dk, dv), F32), row0=0, chunk=GLA_CHUNK,
                          n_chunks=seq // GLA_CHUNK, out_dtype=BF16, **gla_cols)
        o_b_s, s_s = _gla(*gla_args, state_gla[l], row0=t_p, chunk=dec_seq, n_chunks=1, out_dtype=F32, **gla_cols)

        x1 = _mm_out(o_a_p, o_a_s, o_b_p, o_b_s, w_out_bf16, xp, xs, tm, 512)

        h2 = _rmsnorm(x1, norm_ffn[l], tt)
        q = _matmul(h2, w_peer_q[l], tm, 512)
        ii, jj, g = _route(q, peer_keys1[l].astype(BF16), peer_keys2[l].astype(BF16), LANES)
        gates = _expand(ii, jj, g, _token_tile(t_all, 128))
        c_act = _peer_a(h2, peer_u_bf16, gates, tm, 2 * PEER_SUB)
        yp, ys = _peer_v(c_act, peer_v_bf16, x1, t_p, tm, 1024, 2048)
        if l + 1 < depth:
            x = jnp.concatenate([yp, ys], axis=0)

        outs["kp"].append(k_p.reshape(n_p, seq, heads_a, DH_A))
        outs["vp"].append(z[:t_p, 2 * w_a:3 * w_a].reshape(n_p, seq, heads_a, DH_A))
        outs["ks"].append(k_s.reshape(n_s, dec_seq, heads_a, DH_A))
        outs["vs"].append(z[t_p:, 2 * w_a:3 * w_a].reshape(n_s, dec_seq, heads_a, DH_A))
        outs["sp"].append(s_p)
        outs["ss"].append(s_s)

    y_prompt = yp.reshape(n_p, seq, d_model)
    y_sample = ys.reshape(n_s, dec_seq, d_model)
    return (y_prompt, y_sample, jnp.stack(outs["kp"]), jnp.stack(outs["vp"]), jnp.stack(outs["ks"]),
            jnp.stack(outs["vs"]), jnp.stack(outs["sp"]), jnp.stack(outs["ss"]))
```

```python
import functools

import numpy as np
import jax
import jax.numpy as jnp
from jax import lax
from jax.experimental import pallas as pl
from jax.experimental.pallas import tpu as pltpu

F32 = jnp.float32
BF16 = jnp.bfloat16

DH_A = 128
DILATED_BRANCHES = ((128, 1), (512, 4), (2048, 16))
ROPE_THETA = 10000.0
PAST_LEN = 8192
H_B = 4
GATE_RANK = 16
GATE_TAU = 16.0
GLA_CHUNK = 64
N_KEYS = 128
PEER_HEADS = 8
PEER_DKEY = 256
PEER_TOPK = 16
EPS = 1e-6
NEG_INF = -1e30

LANES = 128
VMEM_LIMIT = 56 * 1024 * 1024


def _cparams(*sem):
    return pltpu.CompilerParams(dimension_semantics=sem, vmem_limit_bytes=VMEM_LIMIT)


def _dot(a, b):
    return jnp.dot(a, b, preferred_element_type=F32)


def _dot_nt(a, b):
    return lax.dot_general(a, b, (((1,), (1,)), ((), ())), preferred_element_type=F32)


def _dot_tn(a, b):
    return lax.dot_general(a, b, (((0,), (0,)), ((), ())), preferred_element_type=F32)


def _rmsnorm_rows(x, gain):
    ms = jnp.mean(x * x, axis=-1, keepdims=True)
    return (x * lax.rsqrt(ms + EPS) * gain).astype(BF16)


def _rmsnorm_body(x_ref, g_ref, o_ref):
    o_ref[...] = _rmsnorm_rows(x_ref[...], g_ref[...])


def _rmsnorm(x, gain, tt):
    t, d = x.shape
    return pl.pallas_call(
        _rmsnorm_body,
        grid=(t // tt,),
        in_specs=[pl.BlockSpec((tt, d), lambda i: (i, 0)), pl.BlockSpec((1, d), lambda i: (0, 0))],
        out_specs=pl.BlockSpec((tt, d), lambda i: (i, 0)),
        out_shape=jax.ShapeDtypeStruct((t, d), BF16),
        compiler_params=_cparams("parallel"),
        name="rmsnorm",
    )(x, gain.reshape(1, d))


def _rmsnorm_pair_body(xp_ref, xs_ref, g_ref, o_ref, *, n_p):
    i = pl.program_id(0)

    @pl.when(i < n_p)
    def _():
        o_ref[...] = _rmsnorm_rows(xp_ref[...], g_ref[...])

    @pl.when(i >= n_p)
    def _():
        o_ref[...] = _rmsnorm_rows(xs_ref[...], g_ref[...])


def _rmsnorm_pair(xp, xs, gain, tt):
    t_p, d = xp.shape
    t_s = xs.shape[0]
    assert t_p % tt == 0 and t_s % tt == 0
    n_p = t_p // tt
    return pl.pallas_call(
        functools.partial(_rmsnorm_pair_body, n_p=n_p),
        grid=((t_p + t_s) // tt,),
        in_specs=[pl.BlockSpec((tt, d), lambda i: (jnp.minimum(i, n_p - 1), 0)),
                  pl.BlockSpec((tt, d), lambda i: (jnp.maximum(i - n_p, 0), 0)),
                  pl.BlockSpec((1, d), lambda i: (0, 0))],
        out_specs=pl.BlockSpec((tt, d), lambda i: (i, 0)),
        out_shape=jax.ShapeDtypeStruct((t_p + t_s, d), BF16),
        compiler_params=_cparams("arbitrary"),
        name="rmsnorm_in",
    )(xp, xs, gain.reshape(1, d))


def _mm_body(a_ref, w_ref, o_ref, *, w_is_nk):
    w = w_ref[...].astype(BF16)
    o_ref[...] = _dot_nt(a_ref[...], w) if w_is_nk else _dot(a_ref[...], w)


def _matmul(a, w, tm, tn, n_cols=None, w_is_nk=False, col0=0):
    t, k = a.shape
    n = (w.shape[0] if w_is_nk else w.shape[1]) if n_cols is None else n_cols
    if w_is_nk:
        assert col0 % 8 == 0 and tn % 8 == 0
        w_spec = pl.BlockSpec((pl.Element(tn), pl.Element(k)), lambda i, j: (pl.multiple_of(col0 + j * tn, 8), 0))
    else:
        assert col0 == 0
        w_spec = pl.BlockSpec((k, tn), lambda i, j: (0, j))
    return pl.pallas_call(
        functools.partial(_mm_body, w_is_nk=w_is_nk),
        grid=(t // tm, n // tn),
        in_specs=[pl.BlockSpec((tm, k), lambda i, j: (i, 0)), w_spec],
        out_specs=pl.BlockSpec((tm, tn), lambda i, j: (i, j)),
        out_shape=jax.ShapeDtypeStruct((t, n), F32),
        compiler_params=_cparams("parallel", "arbitrary"),
        name="matmul",
    )(a, w)


def _norm_rope(x, gain, cos2, sin2):
    ms = jnp.mean(x * x, axis=-1, keepdims=True)
    y = x * lax.rsqrt(ms + EPS) * gain
    return y * cos2 + pltpu.roll(y, DH_A // 2, axis=1) * sin2


ATTN_GROUP = 8


def _attn_prompt_body(q_ref, k_ref, v_ref, gq_ref, gk_ref, cos_ref, sin_ref, *rest, seq, n_side):
    side_in, o_ref, kr_ref = rest[:n_side], rest[n_side], rest[n_side + 1]
    side_out, scr = rest[n_side + 2:2 * n_side + 2], rest[2 * n_side + 2:]
    for src, dst in zip(side_in, side_out):
        dst[...] = src[...].astype(dst.dtype)
    blk = DH_A
    n_br = len(DILATED_BRANCHES)
    qs_scr, ks_scr, scr = scr[0], scr[1], scr[2:]
    m_scr, l_scr, acc_scr = scr[:n_br], scr[n_br:2 * n_br], scr[2 * n_br:]

    prep_rows = 4 * blk

    def prep(c, carry):
        rows = pl.ds(pl.multiple_of(c * prep_rows, prep_rows), prep_rows)
        cos2, sin2 = cos_ref[rows, :], sin_ref[rows, :]
        qs_scr[rows, :] = _norm_rope(q_ref[rows, :], gq_ref[...], cos2, sin2) * (DH_A ** -0.5)
        kr = _norm_rope(k_ref[rows, :], gk_ref[...], cos2, sin2)
        ks_scr[rows, :] = kr
        kr_ref[rows, :] = kr
        return carry

    lax.fori_loop(0, seq // prep_rows, prep, 0)
    qi = lax.broadcasted_iota(jnp.int32, (blk, 2 * blk), 0)
    kc = lax.broadcasted_iota(jnp.int32, (blk, 2 * blk), 1)
    band = (kc >= qi) & (kc <= qi + blk)
    cur_half = kc >= blk
    qi1 = lax.broadcasted_iota(jnp.int32, (blk, blk), 0)
    causal = lax.broadcasted_iota(jnp.int32, (blk, blk), 1) <= qi1
    floor_tile = 2.0 * NEG_INF - qi1.astype(F32)

    for bi, (window, d) in enumerate(DILATED_BRANCHES):
        assert window // d == blk
        nb = seq // d // blk

        def blocks(it, carry, bi=bi, d=d, nb=nb):
            rows, vvs, ss = [], [], []
            for u in range(ATTN_GROUP):
                idx = it * ATTN_GROUP + u
                if nb == 1:
                    rw = pl.ds(idx, blk, stride=d)
                    kk = ks_scr[rw, :].astype(BF16)
                    vv = v_ref[rw, :].astype(BF16)
                    mask = causal
                else:
                    r = idx // nb
                    j = idx % nb
                    start = r + j * (blk * d)
                    rw = pl.ds(start, blk, stride=d)
                    prows = pl.ds(jnp.maximum(start - blk * d, r), blk, stride=d)
                    kk = jnp.concatenate([ks_scr[prows, :], ks_scr[rw, :]], axis=0).astype(BF16)
                    vv = jnp.concatenate([v_ref[prows, :], v_ref[rw, :]], axis=0).astype(BF16)
                    mask = band & (cur_half | (j > 0))
                q = qs_scr[rw, :].astype(BF16)
                rows.append(rw)
                vvs.append(vv)
                ss.append(jnp.where(mask, _dot_nt(q, kk), NEG_INF))
            ms = [jnp.max(s, axis=-1, keepdims=True) for s in ss]
            ps = [jnp.exp(s - m) for s, m in zip(ss, ms)]
            for rw, m, p, vv in zip(rows, ms, ps, vvs):
                m_scr[bi][rw, :] = jnp.maximum(m, floor_tile)
                l_scr[bi][rw, :] = jnp.maximum(jnp.sum(p, axis=-1, keepdims=True), floor_tile)
                acc_scr[bi][rw, :] = _dot(p.astype(BF16), vv)
            return carry

        assert (d * nb) % ATTN_GROUP == 0
        lax.fori_loop(0, d * nb // ATTN_GROUP, blocks, 0)

    def merge(c, carry):
        rows = pl.ds(pl.multiple_of(c * blk, blk), blk)
        ms = [m[rows, :] for m in m_scr]
        m_max = functools.reduce(jnp.maximum, ms)
        ws = [jnp.exp(m - m_max) for m in ms]
        num = functools.reduce(jnp.add, [w * a[rows, :] for w, a in zip(ws, acc_scr)])
        den = functools.reduce(jnp.add, [w * l[rows, :] for w, l in zip(ws, l_scr)])
        o_ref[rows, :] = (num / den).astype(o_ref.dtype)
        return carry

    lax.fori_loop(0, seq // blk, merge, 0)


def _attn_prompt(z, q_gain, k_gain, cos2, sin2, sides, n_seq, seq, heads):
    steps = n_seq * heads
    sides = [s if isinstance(s, tuple) else (s, 0, s.shape[0]) for s in sides]
    assert all(n % steps == 0 and (n // steps) % 16 == 0 and r0 % 8 == 0 for _, r0, n in sides)
    side_specs = [pl.BlockSpec((n // steps, s.shape[1]), lambda b, h: (b * heads + h, 0)) for s, _, n in sides]
    side_in_specs = [
        pl.BlockSpec((pl.Element(n // steps), pl.Element(s.shape[1])),
                     lambda b, h, r0=r0, slab=n // steps: (pl.multiple_of(r0 + (b * heads + h) * slab, 8), 0))
        for s, r0, n in sides]
    head_spec = pl.BlockSpec((seq, DH_A), lambda b, h: (b, h))
    vec_spec = pl.BlockSpec((1, DH_A), lambda b, h: (0, 0))
    tab_spec = pl.BlockSpec((seq, DH_A), lambda b, h: (0, 0))
    outs = pl.pallas_call(
        functools.partial(_attn_prompt_body, seq=seq, n_side=len(sides)),
        grid=(n_seq, heads),
        in_specs=[head_spec,
                  pl.BlockSpec((seq, DH_A), lambda b, h: (b, heads + h)),
                  pl.BlockSpec((seq, DH_A), lambda b, h: (b, 2 * heads + h)),
                  vec_spec, vec_spec, tab_spec, tab_spec] + side_in_specs,
        out_specs=[head_spec, head_spec] + side_specs,
        out_shape=[jax.ShapeDtypeStruct((n_seq * seq, heads * DH_A), BF16),
                   jax.ShapeDtypeStruct((n_seq * seq, heads * DH_A), F32)]
        + [jax.ShapeDtypeStruct((n, s.shape[1]), BF16) for s, _, n in sides],
        scratch_shapes=[pltpu.VMEM((seq, DH_A), F32)] * (2 + 3 * len(DILATED_BRANCHES)),
        compiler_params=_cparams("parallel", "arbitrary"),
        name="attn_prompt",
    )(z, z, z, q_gain.reshape(1, DH_A), k_gain.reshape(1, DH_A), cos2, sin2, *[s for s, _, _ in sides])
    return outs[0], outs[1], outs[2:]


SAMPLE_CHUNK_ROWS = 4096
SAMPLE_STREAMS = 2


def _sample_plan(win_buf, dec_seq, heads):
    dil = DILATED_BRANCHES[-1][1]
    dense_from = win_buf - DILATED_BRANCHES[-2][0]
    assert win_buf % dil == 0 and dense_from % dil == 0 and dec_seq * 2 == dil and dec_seq * heads == LANES
    pos_sparse = SAMPLE_CHUNK_ROWS // (dec_seq * heads) * dil
    pos_dense = SAMPLE_CHUNK_ROWS // heads
    assert dense_from % pos_sparse == 0 and (win_buf - dense_from) % pos_dense == 0 and dense_from % pos_dense == 0
    return dil, dense_from, dense_from // pos_sparse, (win_buf - dense_from) // pos_dense


def _sample_weights(win_buf, dec_seq, heads):
    dil, dense_from, n_sparse, n_dense = _sample_plan(win_buf, dec_seq, heads)
    n = np.arange(win_buf + dec_seq)[None, :]
    s = np.arange(dec_seq)[:, None]
    dist = win_buf + s - n
    cnt = np.zeros((dec_seq, win_buf + dec_seq), np.float32)
    for window, d in DILATED_BRANCHES:
        cnt += (dist >= 0) & (dist <= window) & (dist % d == 0)
    assert not cnt[:, :dense_from].reshape(dec_seq, -1, dil)[:, :, dec_seq:].any()
    eye = np.eye(heads, dtype=np.float32)
    sparse = cnt[:, :dense_from].reshape(dec_seq, -1, dil)[:, :, :dec_seq].reshape(dec_seq, -1)
    sparse = np.einsum("sn,hg->nhgs", sparse, eye).reshape(n_sparse, SAMPLE_CHUNK_ROWS, LANES)
    dense = np.einsum("sn,hg->nhgs", cnt[:, dense_from:win_buf], eye).reshape(n_dense, SAMPLE_CHUNK_ROWS, LANES)
    new = np.einsum("sn,hg->hngs", cnt[:, win_buf:], eye).reshape(LANES, LANES)
    return np.concatenate([sparse, dense], axis=0), new


def _attn_sample_body(q_ref, ks_ref, kd_ref, vs_ref, vd_ref, kn_ref, vn_ref, w_ref, wn_ref, gq_ref, gk_ref,
                      cos_ref, sin_ref, o_ref, kr_ref, q_scr, m_scr, l_scr, acc_scr,
                      *, heads, dec_seq, n_sparse, n_dense):
    c = pl.program_id(1)
    n_chunks = n_sparse + n_dense

    def by_head(ref, fn=lambda x: x):
        return jnp.concatenate([fn(ref[:, h * DH_A:(h + 1) * DH_A]) for h in range(heads)], axis=0)

    def rotated(gain_ref):
        return lambda x: _norm_rope(x, gain_ref[...], cos_ref[...], sin_ref[...])

    @pl.when(c == 0)
    def _():
        q_scr[...] = (by_head(q_ref, rotated(gq_ref)) * (DH_A ** -0.5)).astype(BF16)
        m_scr[...] = jnp.full(m_scr.shape, NEG_INF, F32)
        l_scr[...] = jnp.zeros(l_scr.shape, F32)
        acc_scr[...] = jnp.zeros(acc_scr.shape, F32)

    def step(kb, vb, w, n_streams):
        rows = kb.shape[0] // n_streams
        part = [slice(t * rows, (t + 1) * rows) for t in range(n_streams)]
        s = [jnp.where(w[r] > 0, _dot_nt(kb[r], q_scr[...]), NEG_INF) for r in part]
        m_old = [m_scr[t] for t in range(n_streams)]
        m_new = [jnp.maximum(m_old[t], jnp.max(s[t], axis=0, keepdims=True)) for t in range(n_streams)]
        alpha = [jnp.exp(m_old[t] - m_new[t]) for t in range(n_streams)]
        p = [w[part[t]] * jnp.exp(s[t] - m_new[t]) for t in range(n_streams)]
        for t in range(n_streams):
            l_scr[t] = alpha[t] * l_scr[t] + jnp.sum(p[t], axis=0, keepdims=True)
            acc_scr[t] = alpha[t] * acc_scr[t] + _dot_tn(vb[part[t]], p[t].astype(BF16))
            m_scr[t] = m_new[t]

    @pl.when(c < n_sparse)
    def _():
        rows = (SAMPLE_CHUNK_ROWS, DH_A)
        step(ks_ref[0].reshape(rows).astype(BF16), vs_ref[0].reshape(rows).astype(BF16), w_ref[c], SAMPLE_STREAMS)

    @pl.when((c >= n_sparse) & (c < n_chunks))
    def _():
        step(kd_ref[0].astype(BF16), vd_ref[0].astype(BF16), w_ref[c], SAMPLE_STREAMS)

    @pl.when(c == n_chunks)
    def _():
        kn = by_head(kn_ref, rotated(gk_ref))
        step(kn.astype(BF16), by_head(vn_ref).astype(BF16), wn_ref[...], 1)
        ms = [m_scr[t] for t in range(SAMPLE_STREAMS)]
        m_all = functools.reduce(jnp.maximum, ms)
        wt = [jnp.exp(m - m_all) for m in ms]
        num = functools.reduce(jnp.add, [wt[t] * acc_scr[t] for t in range(SAMPLE_STREAMS)])
        den = functools.reduce(jnp.add, [wt[t] * l_scr[t] for t in range(SAMPLE_STREAMS)])
        out = (num / den).T
        for h in range(heads):
            o_ref[:, h * DH_A:(h + 1) * DH_A] = out[h * dec_seq:(h + 1) * dec_seq, :]
            kr_ref[:, h * DH_A:(h + 1) * DH_A] = kn[h * dec_seq:(h + 1) * dec_seq, :]


def _attn_sample(z, q_gain, k_gain, cos2, sin2, cache_k, cache_v, row0, dec_seq):
    n_seq, win_buf, heads, _ = cache_k.shape
    width = heads * DH_A
    dil, dense_from, n_sparse, n_dense = _sample_plan(win_buf, dec_seq, heads)
    n_chunks = n_sparse + n_dense
    w_main, w_new = _sample_weights(win_buf, dec_seq, heads)
    rb = row0 // dec_seq
    groups = SAMPLE_CHUNK_ROWS // LANES
    dense0 = dense_from * heads // SAMPLE_CHUNK_ROWS

    def new_spec(col):
        return pl.BlockSpec((dec_seq, width), lambda b, c: (rb + b, col))

    def small(shape):
        return pl.BlockSpec(shape, lambda b, c: (0, 0))

    out_spec = pl.BlockSpec((dec_seq, width), lambda b, c: (b, 0))

    sparse_view = lambda x: x.reshape(n_seq, win_buf // dil, dil * heads, DH_A)
    dense_view = lambda x: x.reshape(n_seq, win_buf * heads, DH_A)
    sparse_spec = pl.BlockSpec((1, groups, LANES, DH_A), lambda b, c: (b, jnp.minimum(c, n_sparse - 1), 0, 0))
    dense_spec = pl.BlockSpec((1, SAMPLE_CHUNK_ROWS, DH_A),
                              lambda b, c: (b, dense0 + jnp.clip(c - n_sparse, 0, n_dense - 1), 0))
    return pl.pallas_call(
        functools.partial(_attn_sample_body, heads=heads, dec_seq=dec_seq, n_sparse=n_sparse, n_dense=n_dense),
        grid=(n_seq, n_chunks + 1),
        in_specs=[new_spec(0), sparse_spec, dense_spec, sparse_spec, dense_spec, new_spec(1), new_spec(2),
                  pl.BlockSpec(w_main.shape, lambda b, c: (0, 0, 0)),
                  pl.BlockSpec(w_new.shape, lambda b, c: (0, 0)),
                  small((1, DH_A)), small((1, DH_A)), small((dec_seq, DH_A)), small((dec_seq, DH_A))],
        out_specs=[out_spec, out_spec],
        out_shape=[jax.ShapeDtypeStruct((n_seq * dec_seq, width), F32)] * 2,
        scratch_shapes=[pltpu.VMEM((LANES, DH_A), BF16), pltpu.VMEM((SAMPLE_STREAMS, 1, LANES), F32),
                        pltpu.VMEM((SAMPLE_STREAMS, 1, LANES), F32), pltpu.VMEM((SAMPLE_STREAMS, DH_A, LANES), F32)],
        compiler_params=_cparams("parallel", "arbitrary"),
        name="attn_sample",
    )(z, sparse_view(cache_k), dense_view(cache_k), sparse_view(cache_v), dense_view(cache_v), z, z,
      jnp.asarray(w_main), jnp.asarray(w_new), q_gain.reshape(1, DH_A), k_gain.reshape(1, DH_A),
      cos2, sin2)


def _gla_body(q_ref, k_ref, v_ref, r_ref, gl_ref, wg_ref, bg_ref, gn_ref, s0_ref, o_ref, sn_ref, st_scr,
              *, chunk, heads, dk, dv):
    c = pl.program_id(1)
    hs = range(heads)

    @pl.when(c == 0)
    def _():
        for h in hs:
            st_scr[h] = s0_ref[0, h]

    cp = max(chunk, LANES // 2)

    def pad(a):
        return a if cp == chunk else jnp.concatenate([a, jnp.zeros((cp - chunk, a.shape[1]), a.dtype)], axis=0)

    def cols(ref, h, w):
        return ref[:, h * w:(h + 1) * w]

    ti = lax.broadcasted_iota(jnp.int32, (cp, cp), 0)
    tj = lax.broadcasted_iota(jnp.int32, (cp, cp), 1)
    tril = ti >= tj
    trilf = tril.astype(F32)
    glow = gl_ref[...].astype(BF16)
    log_a = [pad(jax.nn.log_sigmoid(_dot(glow, cols(wg_ref, h, dk)) + cols(bg_ref, h, dk)) / GATE_TAU) for h in hs]
    b = [jnp.dot(trilf, la, preferred_element_type=F32, precision=lax.Precision.HIGHEST) for la in log_a]
    b_last = [x[cp - 1:cp, :] for x in b]
    k = [pad(cols(k_ref, h, dk)) for h in hs]
    v = [pad(cols(v_ref, h, dv)).astype(BF16) for h in hs]
    q_in = [(pad(cols(q_ref, h, dk)) * (dk ** -0.5) * jnp.exp(b[h])).astype(BF16) for h in hs]
    k_in = [(k[h] * jnp.exp(-b[h])).astype(BF16) for h in hs]
    aug_t = [jnp.concatenate([k[h] * jnp.exp(b_last[h] - b[h]), jnp.broadcast_to(jnp.exp(b_last[h]), (cp, dk))],
                             axis=0).T for h in hs]
    att = [jnp.where(tril, _dot_nt(q_in[h], k_in[h]), 0.0).astype(BF16) for h in hs]
    st = [st_scr[h] for h in hs]
    o = [(_dot(att[h], v[h]) + _dot(q_in[h], st[h].astype(BF16)))[:chunk] for h in hs]
    st_new = [st[h] * aug_t[h][:, cp:cp + 1] + _dot(aug_t[h][:, :cp].astype(BF16), v[h]) for h in hs]
    for h in hs:
        st_scr[h] = st_new[h]
        ms = jnp.mean(o[h] * o[h], axis=-1, keepdims=True)
        gated = (o[h] * lax.rsqrt(ms + EPS) * gn_ref[...]) * jax.nn.silu(cols(r_ref, h, dv))
        o_ref[:, h * dv:(h + 1) * dv] = gated.astype(o_ref.dtype)

    @pl.when(c == pl.num_programs(1) - 1)
    def _():
        for h in hs:
            sn_ref[0, h] = st_new[h]


def _gla(z, zr, zg, w_gate, b_gate, gla_norm, s0, *, row0, chunk, n_chunks, col_q, col_k, col_v, out_dtype):
    n_seq, heads, dk, dv = s0.shape
    rb = row0 // chunk
    wk, wv = heads * dk, heads * dv
    assert col_q % wk == 0 and col_k % wk == 0 and col_v % wv == 0 and zr.shape[1] == wv

    def rows(b, c):
        return rb + b * n_chunks + c

    return pl.pallas_call(
        functools.partial(_gla_body, chunk=chunk, heads=heads, dk=dk, dv=dv),
        grid=(n_seq, n_chunks),
        in_specs=[pl.BlockSpec((chunk, wk), lambda b, c: (rows(b, c), col_q // wk)),
                  pl.BlockSpec((chunk, wk), lambda b, c: (rows(b, c), col_k // wk)),
                  pl.BlockSpec((chunk, wv), lambda b, c: (rows(b, c), col_v // wv)),
                  pl.BlockSpec((chunk, wv), lambda b, c: (rows(b, c), 0)),
                  pl.BlockSpec((chunk, LANES), lambda b, c: (rows(b, c), 0)),
                  pl.BlockSpec((LANES, wk), lambda b, c: (0, 0)),
                  pl.BlockSpec((1, wk), lambda b, c: (0, 0)),
                  pl.BlockSpec((1, dv), lambda b, c: (0, 0)),
                  pl.BlockSpec((1, heads, dk, dv), lambda b, c: (b, 0, 0, 0))],
        out_specs=[pl.BlockSpec((chunk, wv), lambda b, c: (b * n_chunks + c, 0)),
                   pl.BlockSpec((1, heads, dk, dv), lambda b, c: (b, 0, 0, 0))],
        out_shape=[jax.ShapeDtypeStruct((n_seq * n_chunks * chunk, wv), out_dtype),
                   jax.ShapeDtypeStruct((n_seq, heads, dk, dv), F32)],
        scratch_shapes=[pltpu.VMEM((heads, dk, dv), F32)],
        compiler_params=_cparams("parallel", "arbitrary"),
        name="gla",
    )(z, z, z, zr, zg, w_gate, b_gate, gla_norm, s0)


def _mm_out_body(a1p_ref, a1s_ref, a2p_ref, a2s_ref, w1_ref, w2_ref, xp_ref, xs_ref, o_ref, *, n_full, rem):
    i = pl.program_id(0)

    def go(a1, a2, x):
        o_ref[...] = x + _dot(a1, w1_ref[...]) + _dot(a2, w2_ref[...])

    @pl.when(i < n_full)
    def _():
        go(a1p_ref[...], a2p_ref[...], xp_ref[...])

    @pl.when(i == n_full)
    def _():
        def cat(p_ref, s_ref):
            return jnp.concatenate([p_ref[:rem], s_ref[...].astype(p_ref.dtype)], axis=0)

        go(cat(a1p_ref, a1s_ref), cat(a2p_ref, a2s_ref), cat(xp_ref, xs_ref))


def _mm_out(a1p, a1s, a2p, a2s, w, xp, xs, tm, tn):
    t_p, k1 = a1p.shape
    t_s = a1s.shape[0]
    k2 = a2p.shape[1]
    n = w.shape[1]
    n_full, rem = divmod(t_p, tm)
    assert rem > 0 and rem + t_s == tm and k1 == k2 and w.shape[0] == k1 + k2

    def pspec(k):
        return pl.BlockSpec((tm, k), lambda i, j: (i, 0))

    def sspec(k):
        return pl.BlockSpec((t_s, k), lambda i, j: (0, 0))

    return pl.pallas_call(
        functools.partial(_mm_out_body, n_full=n_full, rem=rem),
        grid=(n_full + 1, n // tn),
        in_specs=[pspec(k1), sspec(k1), pspec(k2), sspec(k2),
                  pl.BlockSpec((k1, tn), lambda i, j: (0, j)), pl.BlockSpec((k2, tn), lambda i, j: (1, j)),
                  pl.BlockSpec((tm, tn), lambda i, j: (i, j)), pl.BlockSpec((t_s, tn), lambda i, j: (0, j))],
        out_specs=pl.BlockSpec((tm, tn), lambda i, j: (i, j)),
        out_shape=jax.ShapeDtypeStruct((t_p + t_s, n), F32),
        compiler_params=_cparams("parallel", "arbitrary"),
        name="mm_out",
    )(a1p, a1s, a2p, a2s, w, w, xp, xs)


def _rows_iota(shape):
    return lax.broadcasted_iota(jnp.int32, shape, 0).astype(F32)


def _topk_cols(s, ranks, k):
    big = float(2 ** 23)
    for _ in range(k):
        m = jnp.max(s, axis=0, keepdims=True)
        r = jnp.min(jnp.where(s == m, ranks, big), axis=0, keepdims=True)
        sel = ranks == r
        yield m, r, sel
        s = jnp.where(sel, -jnp.inf, s)


def _stack_rows(rows):
    n = len(rows)
    rid = lax.broadcasted_iota(jnp.int32, (n, rows[0].shape[1]), 0)
    out = jnp.broadcast_to(rows[0], (n, rows[0].shape[1]))
    for i in range(1, n):
        out = jnp.where(rid == i, rows[i], out)
    return out


def _route_body(q_ref, k1_ref, k2_ref, ii_ref, jj_ref, g_ref):
    tt = q_ref.shape[0]
    half = PEER_DKEY // 2
    topk = PEER_TOPK
    key_rank = _rows_iota((N_KEYS, tt))
    r8 = _rows_iota((8, tt))
    r16 = _rows_iota((topk, tt))
    gates, experts = [], []
    for h in range(PEER_HEADS):
        qh = q_ref[:, h * PEER_DKEY:(h + 1) * PEER_DKEY]
        qn = (qh * lax.rsqrt(jnp.mean(qh * qh, axis=-1, keepdims=True) + EPS)).astype(BF16)
        s1 = _dot_nt(k1_ref[h], qn[:, :half])
        s2 = _dot_nt(k2_ref[h], qn[:, half:])
        v1, i1 = zip(*[(m, r) for m, r, _ in _topk_cols(s1, key_rank, topk)])
        v2, i2 = zip(*[(m, r) for m, r, _ in _topk_cols(s2, key_rank, topk)])
        v1t, i1t, v2t, i2t = (_stack_rows(list(x)) for x in (v1, i1, v2, i2))
        cand = [v1[0] + v2t]
        code = [i1[0] * N_KEYS + i2t]
        flat = [r16]
        for a in range(1, 8):
            ok = r8 < float(topk // (a + 1))
            cand.append(jnp.where(ok, v1[a] + v2t[:8], -jnp.inf))
            code.append(i1[a] * N_KEYS + i2t[:8])
            flat.append(r8 + float(a * topk))
        cand.append(v1t[8:] + v2[0])
        code.append(i1t[8:] * N_KEYS + i2[0])
        flat.append((r8 + 8.0) * float(topk))
        cand, code, flat = (jnp.concatenate(x, axis=0) for x in (cand, code, flat))
        top, picked = zip(*[(m, r) for m, r, _ in _topk_cols(cand, flat * float(N_KEYS * N_KEYS) + code, topk)])
        ex = [jnp.exp(m - top[0]) for m in top]
        inv_z = 1.0 / functools.reduce(jnp.add, ex)
        gates.append(_stack_rows([e * inv_z for e in ex]))
        experts.append(_stack_rows(list(picked)))
    g_ref[...] = jnp.concatenate(gates, axis=0).T
    e = jnp.concatenate(experts, axis=0).T.astype(jnp.int32)
    e = lax.bitwise_and(e, N_KEYS * N_KEYS - 1)
    ii_ref[...] = lax.shift_right_logical(e, N_KEYS.bit_length() - 1)
    jj_ref[...] = lax.bitwise_and(e, N_KEYS - 1)


def _route(q, keys1, keys2, tt):
    t, n = q.shape
    no = PEER_HEADS * PEER_TOPK
    kspec = pl.BlockSpec(keys1.shape, lambda i: (0, 0, 0))
    ospec = pl.BlockSpec((tt, no), lambda i: (i, 0))
    return pl.pallas_call(
        _route_body,
        grid=(t // tt,),
        in_specs=[pl.BlockSpec((tt, n), lambda i: (i, 0)), kspec, kspec],
        out_specs=[ospec, ospec, ospec],
        out_shape=[jax.ShapeDtypeStruct((t, no), jnp.int32), jax.ShapeDtypeStruct((t, no), jnp.int32),
                   jax.ShapeDtypeStruct((t, no), F32)],
        compiler_params=_cparams("parallel"),
        name="peer_route",
    )(q, keys1, keys2)


EXPAND_GROUP = 16
EXPAND_PITCH = N_KEYS + 8
EXPAND_BUFS = 4


def _expand_body(ii_ref, jj_ref, g_ref, o_ref, *scrs):
    te = ii_ref.shape[0]
    sub = lax.broadcasted_iota(jnp.int32, (N_KEYS, ii_ref.shape[1]), 0)
    per_trip = len(scrs) * EXPAND_GROUP

    def trip(ti, carry):
        bases = [pl.multiple_of(ti * per_trip + n * EXPAND_GROUP, EXPAND_GROUP) for n in range(len(scrs))]
        for base, scr in zip(bases, scrs):
            for u in range(EXPAND_GROUP):
                ii = ii_ref[pl.ds(base + u, 1), :]
                jj = jj_ref[pl.ds(base + u, 1), :]
                g = g_ref[pl.ds(base + u, 1), :]
                a = jnp.where(sub == ii, 1.0, 0.0).astype(BF16)
                b = jnp.where(sub == jj, g, 0.0).astype(BF16)
                scr[u * EXPAND_PITCH:u * EXPAND_PITCH + N_KEYS, :] = _dot_nt(a, b)
        for base, scr in zip(bases, scrs):
            for i in range(N_KEYS):
                rows = scr[pl.ds(i, EXPAND_GROUP, stride=EXPAND_PITCH), :]
                o_ref[pl.ds(base, EXPAND_GROUP), i * N_KEYS:(i + 1) * N_KEYS] = rows.astype(o_ref.dtype)
        return carry

    assert te % per_trip == 0
    lax.fori_loop(0, te // per_trip, trip, 0)


def _expand(ii, jj, g, te):
    t, no = ii.shape
    spec = pl.BlockSpec((te, no), lambda i: (i, 0))
    return pl.pallas_call(
        _expand_body,
        grid=(t // te,),
        in_specs=[spec, spec, spec],
        out_specs=pl.BlockSpec((te, N_KEYS * N_KEYS), lambda i: (i, 0)),
        out_shape=jax.ShapeDtypeStruct((t, N_KEYS * N_KEYS), BF16),
        scratch_shapes=[pltpu.VMEM((EXPAND_GROUP * EXPAND_PITCH, N_KEYS), F32)] * EXPAND_BUFS,
        compiler_params=_cparams("parallel"),
        name="peer_expand",
    )(ii, jj, g)


PEER_SUB = 512


def _peer_a_body(h_ref, u_ref, g_ref, o_ref):
    h = h_ref[...]
    for s in range(o_ref.shape[1] // PEER_SUB):
        cols = slice(s * PEER_SUB, (s + 1) * PEER_SUB)
        a = _dot_nt(h, u_ref[cols, :])
        o_ref[:, cols] = (g_ref[:, cols].astype(F32) * jax.nn.gelu(a)).astype(o_ref.dtype)


def _peer_a(h, u, gates, tm, tn):
    t, k = h.shape
    e = u.shape[0]
    return pl.pallas_call(
        _peer_a_body,
        grid=(t // tm, e // tn),
        in_specs=[pl.BlockSpec((tm, k), lambda i, j: (i, 0)), pl.BlockSpec((tn, k), lambda i, j: (j, 0)),
                  pl.BlockSpec((tm, tn), lambda i, j: (i, j))],
        out_specs=pl.BlockSpec((tm, tn), lambda i, j: (i, j)),
        out_shape=jax.ShapeDtypeStruct((t, e), BF16),
        compiler_params=_cparams("parallel", "arbitrary"),
        name="peer_act",
    )(h, u, gates)


def _peer_v_body(c_ref, v_ref, x_ref, yp_ref, ys_ref, acc_scr, *, n_full, rem):
    i = pl.program_id(0)
    kk = pl.program_id(2)
    last = pl.num_programs(2) - 1

    @pl.when(kk == 0)
    def _():
        acc_scr[...] = x_ref[...]

    acc_scr[...] += _dot(c_ref[...], v_ref[...].astype(BF16))

    @pl.when(kk == last)
    def _():
        yp_ref[...] = acc_scr[...]

    @pl.when((kk == last) & (i == n_full))
    def _():
        ys_ref[...] = acc_scr[rem:, :]


def _peer_v(c, v, x, t_p, tm, tn, tk):
    t, e = c.shape
    n = v.shape[1]
    t_s = t - t_p
    n_full, rem = divmod(t_p, tm)
    assert rem > 0 and rem + t_s == tm
    return pl.pallas_call(
        functools.partial(_peer_v_body, n_full=n_full, rem=rem),
        grid=(t // tm, n // tn, e // tk),
        in_specs=[pl.BlockSpec((tm, tk), lambda i, j, k: (i, k)), pl.BlockSpec((tk, tn), lambda i, j, k: (k, j)),
                  pl.BlockSpec((tm, tn), lambda i, j, k: (i, j))],
        out_specs=[pl.BlockSpec((tm, tn), lambda i, j, k: (i, j)),
                   pl.BlockSpec((t_s, tn), lambda i, j, k: (0, jnp.where(i == n_full, j, 0)))],
        out_shape=[jax.ShapeDtypeStruct((t_p, n), F32), jax.ShapeDtypeStruct((t_s, n), F32)],
        scratch_shapes=[pltpu.VMEM((tm, tn), F32)],
        compiler_params=_cparams("arbitrary", "arbitrary", "arbitrary"),
        name="peer_mix",
    )(c, v, x)


def _token_tile(t, cap):
    best = 16
    for c in range(16, cap + 1, 16):
        if t % c == 0:
            best = c
    return best


def _rope_tables(pos):
    half = DH_A // 2
    inv = ROPE_THETA ** (-jnp.arange(half, dtype=F32) / half)
    ang = pos.astype(F32)[:, None] * inv[None, :]
    cos, sin = jnp.cos(ang), jnp.sin(ang)
    return jnp.concatenate([cos, cos], axis=1), jnp.concatenate([-sin, sin], axis=1)


def kernel(x_prompt, x_sample, cache_attn_k, cache_attn_v, state_gla, norm_mix, w_in, q_norm, k_norm, w_gate, b_gate,
           gla_norm, w_out, norm_ffn, w_peer_q, peer_keys1, peer_keys2, peer_u, peer_v):
    n_p, seq, d_model = x_prompt.shape
    n_s, dec_seq, _ = x_sample.shape
    depth = w_in.shape[0]
    heads_a = cache_attn_k.shape[3]
    w_a = heads_a * DH_A
    _, _, heads_b, dk, dv = state_gla.shape
    w_b = heads_b * dv
    assert seq == DILATED_BRANCHES[-1][0] and heads_b == H_B and cache_attn_k.shape[2] == seq
    t_p = n_p * seq
    t_s = n_s * dec_seq
    t_all = t_p + t_s
    tm = _token_tile(t_all, 1056)
    tt = _token_tile(t_s, 256)

    xp = x_prompt.reshape(t_p, d_model)
    xs = x_sample.reshape(t_s, d_model)
    rope_p = _rope_tables(jnp.arange(seq, dtype=jnp.int32))
    rope_s = _rope_tables(PAST_LEN + jnp.arange(dec_seq, dtype=jnp.int32))

    c_qb = 3 * w_a
    c_kb = c_qb + heads_b * dk
    c_vb = c_kb + heads_b * dk
    c_g = c_vb + w_b
    c_r = c_g + GATE_RANK

    outs = dict(kp=[], vp=[], ks=[], vs=[], sp=[], ss=[])
    for l in range(depth):
        w_in_t = jnp.swapaxes(w_in[l], 0, 1)
        w_gate_p = jnp.pad(w_gate[l].astype(BF16), ((0, LANES - GATE_RANK), (0, 0)))

        if l == 0:
            h = _rmsnorm_pair(xp, xs, norm_mix[l], tt)
        else:
            h = _rmsnorm(x, norm_mix[l], tt)
            xp, xs = x[:t_p], x[t_p:]
        z = _matmul(h, w_in_t, tm, 512, n_cols=c_g, w_is_nk=True)
        zg = _matmul(h, w_in_t, tm, LANES, n_cols=LANES, w_is_nk=True, col0=c_g)

        o_a_p, k_p, (peer_u_bf16, peer_v_bf16, w_out_bf16, w_peer_q_bf16, w_r_bf16) = _attn_prompt(
            z, q_norm[l], k_norm[l], *rope_p,
            (peer_u[l], peer_v[l], w_out[l], w_peer_q[l], (w_in_t, c_r, w_b)), n_p, seq, heads_a)
        zr = _matmul(h, w_r_bf16, tm, 1024, w_is_nk=True)
        o_a_s, k_s = _attn_sample(z, q_norm[l], k_norm[l], *rope_s, cache_attn_k[l], cache_attn_v[l], t_p, dec_seq)
        gla_args = (z, zr, zg, w_gate_p, b_gate[l].reshape(1, -1), gla_norm[l].reshape(1, -1))
        gla_cols = dict(col_q=c_qb, col_k=c_kb, col_v=c_vb)
        o_b_p, s_p = _gla(*gla_args, jnp.zeros((n_p, heads_b, dk, dv), F32), row0=0, chunk=GLA_CHUNK,
                          n_chunks=seq // GLA_CHUNK, out_dtype=BF16, **gla_cols)
        o_b_s, s_s = _gla(*gla_args, state_gla[l], row0=t_p, chunk=dec_seq, n_chunks=1, out_dtype=F32, **gla_cols)

        x1 = _mm_out(o_a_p, o_a_s, o_b_p, o_b_s, w_out_bf16, xp, xs, tm, 512)

        h2 = _rmsnorm(x1, norm_ffn[l], tt)
        q = _matmul(h2, w_peer_q_bf16, tm, 1024)
        ii, jj, g = _route(q, peer_keys1[l].astype(BF16), peer_keys2[l].astype(BF16), LANES)
        gates = _expand(ii, jj, g, _token_tile(t_all, 128))
        c_act = _peer_a(h2, peer_u_bf16, gates, tm, 2 * PEER_SUB)
        yp, ys = _peer_v(c_act, peer_v_bf16, x1, t_p, tm, 1024, 2048)
        if l + 1 < depth:
            x = jnp.concatenate([yp, ys], axis=0)

        outs["kp"].append(k_p.reshape(n_p, seq, heads_a, DH_A))
        outs["vp"].append(z[:t_p, 2 * w_a:3 * w_a].reshape(n_p, seq, heads_a, DH_A))
        outs["ks"].append(k_s.reshape(n_s, dec_seq, heads_a, DH_A))
        outs["vs"].append(z[t_p:, 2 * w_a:3 * w_a].reshape(n_s, dec_seq, heads_a, DH_A))
        outs["sp"].append(s_p)
        outs["ss"].append(s_s)

    y_prompt = yp.reshape(n_p, seq, d_model)
    y_sample = ys.reshape(n_s, dec_seq, d_model)
    return (y_prompt, y_sample, jnp.stack(outs["kp"]), jnp.stack(outs["vp"]), jnp.stack(outs["ks"]),
            jnp.stack(outs["vs"]), jnp.stack(outs["sp"]), jnp.stack(outs["ss"]))
```

```python
import functools

import numpy as np
import jax
import jax.numpy as jnp
from jax import lax
from jax.experimental import pallas as pl
from jax.experimental.pallas import tpu as pltpu

F32 = jnp.float32
BF16 = jnp.bfloat16

DH_A = 128
DILATED_BRANCHES = ((128, 1), (512, 4), (2048, 16))
ROPE_THETA = 10000.0
PAST_LEN = 8192
H_B = 4
GATE_RANK = 16
GATE_TAU = 16.0
GLA_CHUNK = 64
N_KEYS = 128
PEER_HEADS = 8
PEER_DKEY = 256
PEER_TOPK = 16
EPS = 1e-6
NEG_INF = -1e30

LANES = 128
VMEM_LIMIT = 56 * 1024 * 1024


def _cparams(*sem):
    return pltpu.CompilerParams(dimension_semantics=sem, vmem_limit_bytes=VMEM_LIMIT)


def _dot(a, b):
    return jnp.dot(a, b, preferred_element_type=F32)


def _dot_nt(a, b):
    return lax.dot_general(a, b, (((1,), (1,)), ((), ())), preferred_element_type=F32)


def _dot_tn(a, b):
    return lax.dot_general(a, b, (((0,), (0,)), ((), ())), preferred_element_type=F32)


def _rmsnorm_rows(x, gain):
    ms = jnp.mean(x * x, axis=-1, keepdims=True)
    return (x * lax.rsqrt(ms + EPS) * gain).astype(BF16)


def _rmsnorm_body(x_ref, g_ref, o_ref):
    o_ref[...] = _rmsnorm_rows(x_ref[...], g_ref[...])


def _rmsnorm(x, gain, tt):
    t, d = x.shape
    return pl.pallas_call(
        _rmsnorm_body,
        grid=(t // tt,),
        in_specs=[pl.BlockSpec((tt, d), lambda i: (i, 0)), pl.BlockSpec((1, d), lambda i: (0, 0))],
        out_specs=pl.BlockSpec((tt, d), lambda i: (i, 0)),
        out_shape=jax.ShapeDtypeStruct((t, d), BF16),
        compiler_params=_cparams("parallel"),
        name="rmsnorm",
    )(x, gain.reshape(1, d))


def _rmsnorm_pair_body(xp_ref, xs_ref, g_ref, o_ref, *, n_p):
    i = pl.program_id(0)

    @pl.when(i < n_p)
    def _():
        o_ref[...] = _rmsnorm_rows(xp_ref[...], g_ref[...])

    @pl.when(i >= n_p)
    def _():
        o_ref[...] = _rmsnorm_rows(xs_ref[...], g_ref[...])


def _rmsnorm_pair(xp, xs, gain, tt):
    t_p, d = xp.shape
    t_s = xs.shape[0]
    assert t_p % tt == 0 and t_s % tt == 0
    n_p = t_p // tt
    return pl.pallas_call(
        functools.partial(_rmsnorm_pair_body, n_p=n_p),
        grid=((t_p + t_s) // tt,),
        in_specs=[pl.BlockSpec((tt, d), lambda i: (jnp.minimum(i, n_p - 1), 0)),
                  pl.BlockSpec((tt, d), lambda i: (jnp.maximum(i - n_p, 0), 0)),
                  pl.BlockSpec((1, d), lambda i: (0, 0))],
        out_specs=pl.BlockSpec((tt, d), lambda i: (i, 0)),
        out_shape=jax.ShapeDtypeStruct((t_p + t_s, d), BF16),
        compiler_params=_cparams("arbitrary"),
        name="rmsnorm_in",
    )(xp, xs, gain.reshape(1, d))


def _mm_body(a_ref, w_ref, o_ref, *, w_is_nk):
    w = w_ref[...].astype(BF16)
    o_ref[...] = _dot_nt(a_ref[...], w) if w_is_nk else _dot(a_ref[...], w)


def _matmul(a, w, tm, tn, n_cols=None, w_is_nk=False, col0=0):
    t, k = a.shape
    n = (w.shape[0] if w_is_nk else w.shape[1]) if n_cols is None else n_cols
    if w_is_nk:
        assert col0 % 8 == 0 and tn % 8 == 0
        w_spec = pl.BlockSpec((pl.Element(tn), pl.Element(k)), lambda i, j: (pl.multiple_of(col0 + j * tn, 8), 0))
    else:
        assert col0 == 0
        w_spec = pl.BlockSpec((k, tn), lambda i, j: (0, j))
    return pl.pallas_call(
        functools.partial(_mm_body, w_is_nk=w_is_nk),
        grid=(t // tm, n // tn),
        in_specs=[pl.BlockSpec((tm, k), lambda i, j: (i, 0)), w_spec],
        out_specs=pl.BlockSpec((tm, tn), lambda i, j: (i, j)),
        out_shape=jax.ShapeDtypeStruct((t, n), F32),
        compiler_params=_cparams("parallel", "arbitrary"),
        name="matmul",
    )(a, w)


def _mm_in_body(a_ref, w_ref, wg_ref, o_ref, og_ref):
    a = a_ref[...]
    o_ref[...] = _dot_nt(a, w_ref[...].astype(BF16))

    @pl.when(pl.program_id(1) == 0)
    def _():
        og_ref[...] = _dot_nt(a, wg_ref[...].astype(BF16))


def _matmul_in(a, w_t, tm, tn, n_cols, col_g):
    t, k = a.shape
    assert col_g % 8 == 0 and tn % 8 == 0
    return pl.pallas_call(
        _mm_in_body,
        grid=(t // tm, n_cols // tn),
        in_specs=[pl.BlockSpec((tm, k), lambda i, j: (i, 0)),
                  pl.BlockSpec((pl.Element(tn), pl.Element(k)), lambda i, j: (pl.multiple_of(j * tn, 8), 0)),
                  pl.BlockSpec((pl.Element(LANES), pl.Element(k)), lambda i, j: (col_g, 0))],
        out_specs=[pl.BlockSpec((tm, tn), lambda i, j: (i, j)), pl.BlockSpec((tm, LANES), lambda i, j: (i, 0))],
        out_shape=[jax.ShapeDtypeStruct((t, n_cols), F32), jax.ShapeDtypeStruct((t, LANES), F32)],
        compiler_params=_cparams("parallel", "arbitrary"),
        name="matmul_in",
    )(a, w_t, w_t)


def _norm_rope(x, gain, cos2, sin2):
    ms = jnp.mean(x * x, axis=-1, keepdims=True)
    y = x * lax.rsqrt(ms + EPS) * gain
    return y * cos2 + pltpu.roll(y, DH_A // 2, axis=1) * sin2


ATTN_GROUP = 8


def _attn_prompt_body(q_ref, k_ref, v_ref, gq_ref, gk_ref, cos_ref, sin_ref, *rest, seq, n_side):
    side_in, o_ref, kr_ref = rest[:n_side], rest[n_side], rest[n_side + 1]
    side_out, scr = rest[n_side + 2:2 * n_side + 2], rest[2 * n_side + 2:]
    for src, dst in zip(side_in, side_out):
        dst[...] = src[...].astype(dst.dtype)
    blk = DH_A
    n_br = len(DILATED_BRANCHES)
    qs_scr, ks_scr, scr = scr[0], scr[1], scr[2:]
    m_scr, l_scr, acc_scr = scr[:n_br], scr[n_br:2 * n_br], scr[2 * n_br:]

    prep_rows = 4 * blk

    def prep(c, carry):
        rows = pl.ds(pl.multiple_of(c * prep_rows, prep_rows), prep_rows)
        cos2, sin2 = cos_ref[rows, :], sin_ref[rows, :]
        qs_scr[rows, :] = _norm_rope(q_ref[rows, :], gq_ref[...], cos2, sin2) * (DH_A ** -0.5)
        kr = _norm_rope(k_ref[rows, :], gk_ref[...], cos2, sin2)
        ks_scr[rows, :] = kr
        kr_ref[rows, :] = kr
        return carry

    lax.fori_loop(0, seq // prep_rows, prep, 0)
    qi = lax.broadcasted_iota(jnp.int32, (blk, 2 * blk), 0)
    kc = lax.broadcasted_iota(jnp.int32, (blk, 2 * blk), 1)
    band = (kc >= qi) & (kc <= qi + blk)
    cur_half = kc >= blk
    qi1 = lax.broadcasted_iota(jnp.int32, (blk, blk), 0)
    causal = lax.broadcasted_iota(jnp.int32, (blk, blk), 1) <= qi1
    floor_tile = 2.0 * NEG_INF - qi1.astype(F32)

    for bi, (window, d) in enumerate(DILATED_BRANCHES):
        assert window // d == blk
        nb = seq // d // blk

        def blocks(it, carry, bi=bi, d=d, nb=nb):
            rows, vvs, ss = [], [], []
            for u in range(ATTN_GROUP):
                idx = it * ATTN_GROUP + u
                if nb == 1:
                    rw = pl.ds(idx, blk, stride=d)
                    kk = ks_scr[rw, :].astype(BF16)
                    vv = v_ref[rw, :].astype(BF16)
                    mask = causal
                else:
                    r = idx // nb
                    j = idx % nb
                    start = r + j * (blk * d)
                    rw = pl.ds(start, blk, stride=d)
                    prows = pl.ds(jnp.maximum(start - blk * d, r), blk, stride=d)
                    kk = jnp.concatenate([ks_scr[prows, :], ks_scr[rw, :]], axis=0).astype(BF16)
                    vv = jnp.concatenate([v_ref[prows, :], v_ref[rw, :]], axis=0).astype(BF16)
                    mask = band & (cur_half | (j > 0))
                q = qs_scr[rw, :].astype(BF16)
                rows.append(rw)
                vvs.append(vv)
                ss.append(jnp.where(mask, _dot_nt(q, kk), NEG_INF))
            ms = [jnp.max(s, axis=-1, keepdims=True) for s in ss]
            ps = [jnp.exp(s - m) for s, m in zip(ss, ms)]
            for rw, m, p, vv in zip(rows, ms, ps, vvs):
                m_scr[bi][rw, :] = jnp.maximum(m, floor_tile)
                l_scr[bi][rw, :] = jnp.maximum(jnp.sum(p, axis=-1, keepdims=True), floor_tile)
                acc_scr[bi][rw, :] = _dot(p.astype(BF16), vv)
            return carry

        assert (d * nb) % ATTN_GROUP == 0
        lax.fori_loop(0, d * nb // ATTN_GROUP, blocks, 0)

    def merge(c, carry):
        rows = pl.ds(pl.multiple_of(c * blk, blk), blk)
        ms = [m[rows, :] for m in m_scr]
        m_max = functools.reduce(jnp.maximum, ms)
        ws = [jnp.exp(m - m_max) for m in ms]
        num = functools.reduce(jnp.add, [w * a[rows, :] for w, a in zip(ws, acc_scr)])
        den = functools.reduce(jnp.add, [w * l[rows, :] for w, l in zip(ws, l_scr)])
        o_ref[rows, :] = (num / den).astype(o_ref.dtype)
        return carry

    lax.fori_loop(0, seq // blk, merge, 0)


def _attn_prompt(z, q_gain, k_gain, cos2, sin2, sides, n_seq, seq, heads):
    steps = n_seq * heads
    sides = [s if isinstance(s, tuple) else (s, 0, s.shape[0]) for s in sides]
    assert all(n % steps == 0 and (n // steps) % 16 == 0 and r0 % 8 == 0 for _, r0, n in sides)
    side_specs = [pl.BlockSpec((n // steps, s.shape[1]), lambda b, h: (b * heads + h, 0)) for s, _, n in sides]
    side_in_specs = [
        pl.BlockSpec((pl.Element(n // steps), pl.Element(s.shape[1])),
                     lambda b, h, r0=r0, slab=n // steps: (pl.multiple_of(r0 + (b * heads + h) * slab, 8), 0))
        for s, r0, n in sides]
    head_spec = pl.BlockSpec((seq, DH_A), lambda b, h: (b, h))
    vec_spec = pl.BlockSpec((1, DH_A), lambda b, h: (0, 0))
    tab_spec = pl.BlockSpec((seq, DH_A), lambda b, h: (0, 0))
    outs = pl.pallas_call(
        functools.partial(_attn_prompt_body, seq=seq, n_side=len(sides)),
        grid=(n_seq, heads),
        in_specs=[head_spec,
                  pl.BlockSpec((seq, DH_A), lambda b, h: (b, heads + h)),
                  pl.BlockSpec((seq, DH_A), lambda b, h: (b, 2 * heads + h)),
                  vec_spec, vec_spec, tab_spec, tab_spec] + side_in_specs,
        out_specs=[head_spec, head_spec] + side_specs,
        out_shape=[jax.ShapeDtypeStruct((n_seq * seq, heads * DH_A), BF16),
                   jax.ShapeDtypeStruct((n_seq * seq, heads * DH_A), F32)]
        + [jax.ShapeDtypeStruct((n, s.shape[1]), BF16) for s, _, n in sides],
        scratch_shapes=[pltpu.VMEM((seq, DH_A), F32)] * (2 + 3 * len(DILATED_BRANCHES)),
        compiler_params=_cparams("parallel", "arbitrary"),
        name="attn_prompt",
    )(z, z, z, q_gain.reshape(1, DH_A), k_gain.reshape(1, DH_A), cos2, sin2, *[s for s, _, _ in sides])
    return outs[0], outs[1], outs[2:]


SAMPLE_CHUNK_ROWS = 4096
SAMPLE_STREAMS = 2


def _sample_plan(win_buf, dec_seq, heads):
    dil = DILATED_BRANCHES[-1][1]
    dense_from = win_buf - DILATED_BRANCHES[-2][0]
    assert win_buf % dil == 0 and dense_from % dil == 0 and dec_seq * 2 == dil and dec_seq * heads == LANES
    pos_sparse = SAMPLE_CHUNK_ROWS // (dec_seq * heads) * dil
    pos_dense = SAMPLE_CHUNK_ROWS // heads
    assert dense_from % pos_sparse == 0 and (win_buf - dense_from) % pos_dense == 0 and dense_from % pos_dense == 0
    return dil, dense_from, dense_from // pos_sparse, (win_buf - dense_from) // pos_dense


def _sample_weights(win_buf, dec_seq, heads):
    dil, dense_from, n_sparse, n_dense = _sample_plan(win_buf, dec_seq, heads)
    n = np.arange(win_buf + dec_seq)[None, :]
    s = np.arange(dec_seq)[:, None]
    dist = win_buf + s - n
    cnt = np.zeros((dec_seq, win_buf + dec_seq), np.float32)
    for window, d in DILATED_BRANCHES:
        cnt += (dist >= 0) & (dist <= window) & (dist % d == 0)
    assert not cnt[:, :dense_from].reshape(dec_seq, -1, dil)[:, :, dec_seq:].any()
    eye = np.eye(heads, dtype=np.float32)
    sparse = cnt[:, :dense_from].reshape(dec_seq, -1, dil)[:, :, :dec_seq].reshape(dec_seq, -1)
    sparse = np.einsum("sn,hg->nhgs", sparse, eye).reshape(n_sparse, SAMPLE_CHUNK_ROWS, LANES)
    dense = np.einsum("sn,hg->nhgs", cnt[:, dense_from:win_buf], eye).reshape(n_dense, SAMPLE_CHUNK_ROWS, LANES)
    new = np.einsum("sn,hg->hngs", cnt[:, win_buf:], eye).reshape(LANES, LANES)
    return np.concatenate([sparse, dense], axis=0), new


def _attn_sample_body(q_ref, ks_ref, kd_ref, vs_ref, vd_ref, kn_ref, vn_ref, w_ref, wn_ref, gq_ref, gk_ref,
                      cos_ref, sin_ref, o_ref, kr_ref, q_scr, m_scr, l_scr, acc_scr,
                      *, heads, dec_seq, n_sparse, n_dense):
    c = pl.program_id(1)
    n_chunks = n_sparse + n_dense

    def by_head(ref, fn=lambda x: x):
        return jnp.concatenate([fn(ref[:, h * DH_A:(h + 1) * DH_A]) for h in range(heads)], axis=0)

    def rotated(gain_ref):
        return lambda x: _norm_rope(x, gain_ref[...], cos_ref[...], sin_ref[...])

    @pl.when(c == 0)
    def _():
        q_scr[...] = (by_head(q_ref, rotated(gq_ref)) * (DH_A ** -0.5)).astype(BF16)
        m_scr[...] = jnp.full(m_scr.shape, NEG_INF, F32)
        l_scr[...] = jnp.zeros(l_scr.shape, F32)
        acc_scr[...] = jnp.zeros(acc_scr.shape, F32)

    def step(kb, vb, w, n_streams):
        rows = kb.shape[0] // n_streams
        part = [slice(t * rows, (t + 1) * rows) for t in range(n_streams)]
        s = [jnp.where(w[r] > 0, _dot_nt(kb[r], q_scr[...]), NEG_INF) for r in part]
        m_old = [m_scr[t] for t in range(n_streams)]
        m_new = [jnp.maximum(m_old[t], jnp.max(s[t], axis=0, keepdims=True)) for t in range(n_streams)]
        alpha = [jnp.exp(m_old[t] - m_new[t]) for t in range(n_streams)]
        p = [w[part[t]] * jnp.exp(s[t] - m_new[t]) for t in range(n_streams)]
        for t in range(n_streams):
            l_scr[t] = alpha[t] * l_scr[t] + jnp.sum(p[t], axis=0, keepdims=True)
            acc_scr[t] = alpha[t] * acc_scr[t] + _dot_tn(vb[part[t]], p[t].astype(BF16))
            m_scr[t] = m_new[t]

    @pl.when(c < n_sparse)
    def _():
        rows = (SAMPLE_CHUNK_ROWS, DH_A)
        step(ks_ref[0].reshape(rows).astype(BF16), vs_ref[0].reshape(rows).astype(BF16), w_ref[c], SAMPLE_STREAMS)

    @pl.when((c >= n_sparse) & (c < n_chunks))
    def _():
        step(kd_ref[0].astype(BF16), vd_ref[0].astype(BF16), w_ref[c], SAMPLE_STREAMS)

    @pl.when(c == n_chunks)
    def _():
        kn = by_head(kn_ref, rotated(gk_ref))
        step(kn.astype(BF16), by_head(vn_ref).astype(BF16), wn_ref[...], 1)
        ms = [m_scr[t] for t in range(SAMPLE_STREAMS)]
        m_all = functools.reduce(jnp.maximum, ms)
        wt = [jnp.exp(m - m_all) for m in ms]
        num = functools.reduce(jnp.add, [wt[t] * acc_scr[t] for t in range(SAMPLE_STREAMS)])
        den = functools.reduce(jnp.add, [wt[t] * l_scr[t] for t in range(SAMPLE_STREAMS)])
        out = (num / den).T
        for h in range(heads):
            o_ref[:, h * DH_A:(h + 1) * DH_A] = out[h * dec_seq:(h + 1) * dec_seq, :]
            kr_ref[:, h * DH_A:(h + 1) * DH_A] = kn[h * dec_seq:(h + 1) * dec_seq, :]


def _attn_sample(z, q_gain, k_gain, cos2, sin2, cache_k, cache_v, row0, dec_seq):
    n_seq, win_buf, heads, _ = cache_k.shape
    width = heads * DH_A
    dil, dense_from, n_sparse, n_dense = _sample_plan(win_buf, dec_seq, heads)
    n_chunks = n_sparse + n_dense
    w_main, w_new = _sample_weights(win_buf, dec_seq, heads)
    rb = row0 // dec_seq
    groups = SAMPLE_CHUNK_ROWS // LANES
    dense0 = dense_from * heads // SAMPLE_CHUNK_ROWS

    def new_spec(col):
        return pl.BlockSpec((dec_seq, width), lambda b, c: (rb + b, col))

    def small(shape):
        return pl.BlockSpec(shape, lambda b, c: (0, 0))

    out_spec = pl.BlockSpec((dec_seq, width), lambda b, c: (b, 0))

    sparse_view = lambda x: x.reshape(n_seq, win_buf // dil, dil * heads, DH_A)
    dense_view = lambda x: x.reshape(n_seq, win_buf * heads, DH_A)
    sparse_spec = pl.BlockSpec((1, groups, LANES, DH_A), lambda b, c: (b, jnp.minimum(c, n_sparse - 1), 0, 0))
    dense_spec = pl.BlockSpec((1, SAMPLE_CHUNK_ROWS, DH_A),
                              lambda b, c: (b, dense0 + jnp.clip(c - n_sparse, 0, n_dense - 1), 0))
    return pl.pallas_call(
        functools.partial(_attn_sample_body, heads=heads, dec_seq=dec_seq, n_sparse=n_sparse, n_dense=n_dense),
        grid=(n_seq, n_chunks + 1),
        in_specs=[new_spec(0), sparse_spec, dense_spec, sparse_spec, dense_spec, new_spec(1), new_spec(2),
                  pl.BlockSpec(w_main.shape, lambda b, c: (0, 0, 0)),
                  pl.BlockSpec(w_new.shape, lambda b, c: (0, 0)),
                  small((1, DH_A)), small((1, DH_A)), small((dec_seq, DH_A)), small((dec_seq, DH_A))],
        out_specs=[out_spec, out_spec],
        out_shape=[jax.ShapeDtypeStruct((n_seq * dec_seq, width), F32)] * 2,
        scratch_shapes=[pltpu.VMEM((LANES, DH_A), BF16), pltpu.VMEM((SAMPLE_STREAMS, 1, LANES), F32),
                        pltpu.VMEM((SAMPLE_STREAMS, 1, LANES), F32), pltpu.VMEM((SAMPLE_STREAMS, DH_A, LANES), F32)],
        compiler_params=_cparams("parallel", "arbitrary"),
        name="attn_sample",
    )(z, sparse_view(cache_k), dense_view(cache_k), sparse_view(cache_v), dense_view(cache_v), z, z,
      jnp.asarray(w_main), jnp.asarray(w_new), q_gain.reshape(1, DH_A), k_gain.reshape(1, DH_A),
      cos2, sin2)


def _gla_body(q_ref, k_ref, v_ref, r_ref, gl_ref, wg_ref, bg_ref, gn_ref, s0_ref, o_ref, sn_ref, st_scr,
              *, chunk, heads, dk, dv):
    c = pl.program_id(1)
    hs = range(heads)

    @pl.when(c == 0)
    def _():
        for h in hs:
            st_scr[h] = s0_ref[0, h]

    cp = max(chunk, LANES // 2)

    def pad(a):
        return a if cp == chunk else jnp.concatenate([a, jnp.zeros((cp - chunk, a.shape[1]), a.dtype)], axis=0)

    def cols(ref, h, w):
        return ref[:, h * w:(h + 1) * w]

    ti = lax.broadcasted_iota(jnp.int32, (cp, cp), 0)
    tj = lax.broadcasted_iota(jnp.int32, (cp, cp), 1)
    tril = ti >= tj
    trilf = tril.astype(F32)
    glow = gl_ref[...].astype(BF16)
    log_a = [pad(jax.nn.log_sigmoid(_dot(glow, cols(wg_ref, h, dk)) + cols(bg_ref, h, dk)) / GATE_TAU) for h in hs]
    b = [jnp.dot(trilf, la, preferred_element_type=F32, precision=lax.Precision.HIGHEST) for la in log_a]
    b_last = [x[cp - 1:cp, :] for x in b]
    k = [pad(cols(k_ref, h, dk)) for h in hs]
    v = [pad(cols(v_ref, h, dv)).astype(BF16) for h in hs]
    q_in = [(pad(cols(q_ref, h, dk)) * (dk ** -0.5) * jnp.exp(b[h])).astype(BF16) for h in hs]
    k_in = [(k[h] * jnp.exp(-b[h])).astype(BF16) for h in hs]
    aug_t = [jnp.concatenate([k[h] * jnp.exp(b_last[h] - b[h]), jnp.broadcast_to(jnp.exp(b_last[h]), (cp, dk))],
                             axis=0).T for h in hs]
    att = [jnp.where(tril, _dot_nt(q_in[h], k_in[h]), 0.0).astype(BF16) for h in hs]
    st = [st_scr[h] for h in hs]
    o = [(_dot(att[h], v[h]) + _dot(q_in[h], st[h].astype(BF16)))[:chunk] for h in hs]
    st_new = [st[h] * aug_t[h][:, cp:cp + 1] + _dot(aug_t[h][:, :cp].astype(BF16), v[h]) for h in hs]
    for h in hs:
        st_scr[h] = st_new[h]
        ms = jnp.mean(o[h] * o[h], axis=-1, keepdims=True)
        gated = (o[h] * lax.rsqrt(ms + EPS) * gn_ref[...]) * jax.nn.silu(cols(r_ref, h, dv))
        o_ref[:, h * dv:(h + 1) * dv] = gated.astype(o_ref.dtype)

    @pl.when(c == pl.num_programs(1) - 1)
    def _():
        for h in hs:
            sn_ref[0, h] = st_new[h]


def _gla(z, zr, zg, w_gate, b_gate, gla_norm, s0, *, row0, chunk, n_chunks, col_q, col_k, col_v, out_dtype):
    n_seq, heads, dk, dv = s0.shape
    rb = row0 // chunk
    wk, wv = heads * dk, heads * dv
    assert col_q % wk == 0 and col_k % wk == 0 and col_v % wv == 0 and zr.shape[1] == wv

    def rows(b, c):
        return rb + b * n_chunks + c

    return pl.pallas_call(
        functools.partial(_gla_body, chunk=chunk, heads=heads, dk=dk, dv=dv),
        grid=(n_seq, n_chunks),
        in_specs=[pl.BlockSpec((chunk, wk), lambda b, c: (rows(b, c), col_q // wk)),
                  pl.BlockSpec((chunk, wk), lambda b, c: (rows(b, c), col_k // wk)),
                  pl.BlockSpec((chunk, wv), lambda b, c: (rows(b, c), col_v // wv)),
                  pl.BlockSpec((chunk, wv), lambda b, c: (rows(b, c), 0)),
                  pl.BlockSpec((chunk, LANES), lambda b, c: (rows(b, c), 0)),
                  pl.BlockSpec((LANES, wk), lambda b, c: (0, 0)),
                  pl.BlockSpec((1, wk), lambda b, c: (0, 0)),
                  pl.BlockSpec((1, dv), lambda b, c: (0, 0)),
                  pl.BlockSpec((1, heads, dk, dv), lambda b, c: (b, 0, 0, 0))],
        out_specs=[pl.BlockSpec((chunk, wv), lambda b, c: (b * n_chunks + c, 0)),
                   pl.BlockSpec((1, heads, dk, dv), lambda b, c: (b, 0, 0, 0))],
        out_shape=[jax.ShapeDtypeStruct((n_seq * n_chunks * chunk, wv), out_dtype),
                   jax.ShapeDtypeStruct((n_seq, heads, dk, dv), F32)],
        scratch_shapes=[pltpu.VMEM((heads, dk, dv), F32)],
        compiler_params=_cparams("parallel", "arbitrary"),
        name="gla",
    )(z, z, z, zr, zg, w_gate, b_gate, gla_norm, s0)


def _mm_out_body(a1p_ref, a1s_ref, a2p_ref, a2s_ref, w1_ref, w2_ref, xp_ref, xs_ref, o_ref, *, n_full, rem):
    i = pl.program_id(0)

    def go(a1, a2, x):
        o_ref[...] = x + _dot(a1, w1_ref[...]) + _dot(a2, w2_ref[...])

    @pl.when(i < n_full)
    def _():
        go(a1p_ref[...], a2p_ref[...], xp_ref[...])

    @pl.when(i == n_full)
    def _():
        def cat(p_ref, s_ref):
            return jnp.concatenate([p_ref[:rem], s_ref[...].astype(p_ref.dtype)], axis=0)

        go(cat(a1p_ref, a1s_ref), cat(a2p_ref, a2s_ref), cat(xp_ref, xs_ref))


def _mm_out(a1p, a1s, a2p, a2s, w, xp, xs, tm, tn):
    t_p, k1 = a1p.shape
    t_s = a1s.shape[0]
    k2 = a2p.shape[1]
    n = w.shape[1]
    n_full, rem = divmod(t_p, tm)
    assert rem > 0 and rem + t_s == tm and k1 == k2 and w.shape[0] == k1 + k2

    def pspec(k):
        return pl.BlockSpec((tm, k), lambda i, j: (i, 0))

    def sspec(k):
        return pl.BlockSpec((t_s, k), lambda i, j: (0, 0))

    return pl.pallas_call(
        functools.partial(_mm_out_body, n_full=n_full, rem=rem),
        grid=(n_full + 1, n // tn),
        in_specs=[pspec(k1), sspec(k1), pspec(k2), sspec(k2),
                  pl.BlockSpec((k1, tn), lambda i, j: (0, j)), pl.BlockSpec((k2, tn), lambda i, j: (1, j)),
                  pl.BlockSpec((tm, tn), lambda i, j: (i, j)), pl.BlockSpec((t_s, tn), lambda i, j: (0, j))],
        out_specs=pl.BlockSpec((tm, tn), lambda i, j: (i, j)),
        out_shape=jax.ShapeDtypeStruct((t_p + t_s, n), F32),
        compiler_params=_cparams("parallel", "arbitrary"),
        name="mm_out",
    )(a1p, a1s, a2p, a2s, w, w, xp, xs)


def _rows_iota(shape):
    return lax.broadcasted_iota(jnp.int32, shape, 0).astype(F32)


def _topk_cols(s, ranks, k):
    big = float(2 ** 23)
    for _ in range(k):
        m = jnp.max(s, axis=0, keepdims=True)
        r = jnp.min(jnp.where(s == m, ranks, big), axis=0, keepdims=True)
        sel = ranks == r
        yield m, r, sel
        s = jnp.where(sel, -jnp.inf, s)


def _stack_rows(rows):
    n = len(rows)
    rid = lax.broadcasted_iota(jnp.int32, (n, rows[0].shape[1]), 0)
    out = jnp.broadcast_to(rows[0], (n, rows[0].shape[1]))
    for i in range(1, n):
        out = jnp.where(rid == i, rows[i], out)
    return out


def _route_body(q_ref, k1_ref, k2_ref, ii_ref, jj_ref, g_ref):
    tt = q_ref.shape[0]
    half = PEER_DKEY // 2
    topk = PEER_TOPK
    key_rank = _rows_iota((N_KEYS, tt))
    r8 = _rows_iota((8, tt))
    r16 = _rows_iota((topk, tt))
    gates, experts = [], []
    for h in range(PEER_HEADS):
        qh = q_ref[:, h * PEER_DKEY:(h + 1) * PEER_DKEY]
        qn = (qh * lax.rsqrt(jnp.mean(qh * qh, axis=-1, keepdims=True) + EPS)).astype(BF16)
        s1 = _dot_nt(k1_ref[h], qn[:, :half])
        s2 = _dot_nt(k2_ref[h], qn[:, half:])
        v1, i1 = zip(*[(m, r) for m, r, _ in _topk_cols(s1, key_rank, topk)])
        v2, i2 = zip(*[(m, r) for m, r, _ in _topk_cols(s2, key_rank, topk)])
        v1t, i1t, v2t, i2t = (_stack_rows(list(x)) for x in (v1, i1, v2, i2))
        cand = [v1[0] + v2t]
        code = [i1[0] * N_KEYS + i2t]
        flat = [r16]
        for a in range(1, 8):
            ok = r8 < float(topk // (a + 1))
            cand.append(jnp.where(ok, v1[a] + v2t[:8], -jnp.inf))
            code.append(i1[a] * N_KEYS + i2t[:8])
            flat.append(r8 + float(a * topk))
        cand.append(v1t[8:] + v2[0])
        code.append(i1t[8:] * N_KEYS + i2[0])
        flat.append((r8 + 8.0) * float(topk))
        cand, code, flat = (jnp.concatenate(x, axis=0) for x in (cand, code, flat))
        top, picked = zip(*[(m, r) for m, r, _ in _topk_cols(cand, flat * float(N_KEYS * N_KEYS) + code, topk)])
        ex = [jnp.exp(m - top[0]) for m in top]
        inv_z = 1.0 / functools.reduce(jnp.add, ex)
        gates.append(_stack_rows([e * inv_z for e in ex]))
        experts.append(_stack_rows(list(picked)))
    g_ref[...] = jnp.concatenate(gates, axis=0).T
    e = jnp.concatenate(experts, axis=0).T.astype(jnp.int32)
    e = lax.bitwise_and(e, N_KEYS * N_KEYS - 1)
    ii_ref[...] = lax.shift_right_logical(e, N_KEYS.bit_length() - 1)
    jj_ref[...] = lax.bitwise_and(e, N_KEYS - 1)


def _route(q, keys1, keys2, tt):
    t, n = q.shape
    no = PEER_HEADS * PEER_TOPK
    kspec = pl.BlockSpec(keys1.shape, lambda i: (0, 0, 0))
    ospec = pl.BlockSpec((tt, no), lambda i: (i, 0))
    return pl.pallas_call(
        _route_body,
        grid=(t // tt,),
        in_specs=[pl.BlockSpec((tt, n), lambda i: (i, 0)), kspec, kspec],
        out_specs=[ospec, ospec, ospec],
        out_shape=[jax.ShapeDtypeStruct((t, no), jnp.int32), jax.ShapeDtypeStruct((t, no), jnp.int32),
                   jax.ShapeDtypeStruct((t, no), F32)],
        compiler_params=_cparams("parallel"),
        name="peer_route",
    )(q, keys1, keys2)


EXPAND_GROUP = 16
EXPAND_PITCH = N_KEYS + 8
EXPAND_BUFS = 4


def _expand_body(ii_ref, jj_ref, g_ref, o_ref, *scrs):
    te = ii_ref.shape[0]
    sub = lax.broadcasted_iota(jnp.int32, (N_KEYS, ii_ref.shape[1]), 0)
    per_trip = len(scrs) * EXPAND_GROUP

    def trip(ti, carry):
        bases = [pl.multiple_of(ti * per_trip + n * EXPAND_GROUP, EXPAND_GROUP) for n in range(len(scrs))]
        for base, scr in zip(bases, scrs):
            for u in range(EXPAND_GROUP):
                ii = ii_ref[pl.ds(base + u, 1), :]
                jj = jj_ref[pl.ds(base + u, 1), :]
                g = g_ref[pl.ds(base + u, 1), :]
                a = jnp.where(sub == ii, 1.0, 0.0).astype(BF16)
                b = jnp.where(sub == jj, g, 0.0).astype(BF16)
                scr[u * EXPAND_PITCH:u * EXPAND_PITCH + N_KEYS, :] = _dot_nt(a, b)
        for base, scr in zip(bases, scrs):
            for i in range(N_KEYS):
                rows = scr[pl.ds(i, EXPAND_GROUP, stride=EXPAND_PITCH), :]
                o_ref[pl.ds(base, EXPAND_GROUP), i * N_KEYS:(i + 1) * N_KEYS] = rows.astype(o_ref.dtype)
        return carry

    assert te % per_trip == 0
    lax.fori_loop(0, te // per_trip, trip, 0)


def _expand(ii, jj, g, te):
    t, no = ii.shape
    spec = pl.BlockSpec((te, no), lambda i: (i, 0))
    return pl.pallas_call(
        _expand_body,
        grid=(t // te,),
        in_specs=[spec, spec, spec],
        out_specs=pl.BlockSpec((te, N_KEYS * N_KEYS), lambda i: (i, 0)),
        out_shape=jax.ShapeDtypeStruct((t, N_KEYS * N_KEYS), BF16),
        scratch_shapes=[pltpu.VMEM((EXPAND_GROUP * EXPAND_PITCH, N_KEYS), F32)] * EXPAND_BUFS,
        compiler_params=_cparams("parallel"),
        name="peer_expand",
    )(ii, jj, g)


PEER_SUB = 512


def _peer_a_body(h_ref, u_ref, g_ref, o_ref):
    h = h_ref[...]
    for s in range(o_ref.shape[1] // PEER_SUB):
        cols = slice(s * PEER_SUB, (s + 1) * PEER_SUB)
        a = _dot_nt(h, u_ref[cols, :])
        o_ref[:, cols] = (g_ref[:, cols].astype(F32) * jax.nn.gelu(a)).astype(o_ref.dtype)


def _peer_a(h, u, gates, tm, tn):
    t, k = h.shape
    e = u.shape[0]
    return pl.pallas_call(
        _peer_a_body,
        grid=(t // tm, e // tn),
        in_specs=[pl.BlockSpec((tm, k), lambda i, j: (i, 0)), pl.BlockSpec((tn, k), lambda i, j: (j, 0)),
                  pl.BlockSpec((tm, tn), lambda i, j: (i, j))],
        out_specs=pl.BlockSpec((tm, tn), lambda i, j: (i, j)),
        out_shape=jax.ShapeDtypeStruct((t, e), BF16),
        compiler_params=_cparams("parallel", "arbitrary"),
        name="peer_act",
    )(h, u, gates)


def _peer_v_body(c_ref, v_ref, x_ref, yp_ref, ys_ref, acc_scr, *, n_full, rem):
    i = pl.program_id(0)
    kk = pl.program_id(2)
    last = pl.num_programs(2) - 1

    @pl.when(kk == 0)
    def _():
        acc_scr[...] = x_ref[...]

    acc_scr[...] += _dot(c_ref[...], v_ref[...].astype(BF16))

    @pl.when(kk == last)
    def _():
        yp_ref[...] = acc_scr[...]

    @pl.when((kk == last) & (i == n_full))
    def _():
        ys_ref[...] = acc_scr[rem:, :]


def _peer_v(c, v, x, t_p, tm, tn, tk):
    t, e = c.shape
    n = v.shape[1]
    t_s = t - t_p
    n_full, rem = divmod(t_p, tm)
    assert rem > 0 and rem + t_s == tm
    return pl.pallas_call(
        functools.partial(_peer_v_body, n_full=n_full, rem=rem),
        grid=(t // tm, n // tn, e // tk),
        in_specs=[pl.BlockSpec((tm, tk), lambda i, j, k: (i, k)), pl.BlockSpec((tk, tn), lambda i, j, k: (k, j)),
                  pl.BlockSpec((tm, tn), lambda i, j, k: (i, j))],
        out_specs=[pl.BlockSpec((tm, tn), lambda i, j, k: (i, j)),
                   pl.BlockSpec((t_s, tn), lambda i, j, k: (0, jnp.where(i == n_full, j, 0)))],
        out_shape=[jax.ShapeDtypeStruct((t_p, n), F32), jax.ShapeDtypeStruct((t_s, n), F32)],
        scratch_shapes=[pltpu.VMEM((tm, tn), F32)],
        compiler_params=_cparams("arbitrary", "arbitrary", "arbitrary"),
        name="peer_mix",
    )(c, v, x)


def _token_tile(t, cap):
    best = 16
    for c in range(16, cap + 1, 16):
        if t % c == 0:
            best = c
    return best


def _rope_tables(pos):
    half = DH_A // 2
    inv = ROPE_THETA ** (-jnp.arange(half, dtype=F32) / half)
    ang = pos.astype(F32)[:, None] * inv[None, :]
    cos, sin = jnp.cos(ang), jnp.sin(ang)
    return jnp.concatenate([cos, cos], axis=1), jnp.concatenate([-sin, sin], axis=1)


def kernel(x_prompt, x_sample, cache_attn_k, cache_attn_v, state_gla, norm_mix, w_in, q_norm, k_norm, w_gate, b_gate,
           gla_norm, w_out, norm_ffn, w_peer_q, peer_keys1, peer_keys2, peer_u, peer_v):
    n_p, seq, d_model = x_prompt.shape
    n_s, dec_seq, _ = x_sample.shape
    depth = w_in.shape[0]
    heads_a = cache_attn_k.shape[3]
    w_a = heads_a * DH_A
    _, _, heads_b, dk, dv = state_gla.shape
    w_b = heads_b * dv
    assert seq == DILATED_BRANCHES[-1][0] and heads_b == H_B and cache_attn_k.shape[2] == seq
    t_p = n_p * seq
    t_s = n_s * dec_seq
    t_all = t_p + t_s
    tm = _token_tile(t_all, 1056)
    tt = _token_tile(t_s, 256)

    xp = x_prompt.reshape(t_p, d_model)
    xs = x_sample.reshape(t_s, d_model)
    rope_p = _rope_tables(jnp.arange(seq, dtype=jnp.int32))
    rope_s = _rope_tables(PAST_LEN + jnp.arange(dec_seq, dtype=jnp.int32))

    c_qb = 3 * w_a
    c_kb = c_qb + heads_b * dk
    c_vb = c_kb + heads_b * dk
    c_g = c_vb + w_b
    c_r = c_g + GATE_RANK

    outs = dict(kp=[], vp=[], ks=[], vs=[], sp=[], ss=[])
    for l in range(depth):
        w_in_t = jnp.swapaxes(w_in[l], 0, 1)
        w_gate_p = jnp.pad(w_gate[l].astype(BF16), ((0, LANES - GATE_RANK), (0, 0)))

        if l == 0:
            h = _rmsnorm_pair(xp, xs, norm_mix[l], tt)
        else:
            h = _rmsnorm(x, norm_mix[l], tt)
            xp, xs = x[:t_p], x[t_p:]
        z, zg = _matmul_in(h, w_in_t, tm, 512, c_g, c_g)

        o_a_p, k_p, (peer_u_bf16, peer_v_bf16, w_out_bf16, w_peer_q_bf16, w_r_bf16) = _attn_prompt(
            z, q_norm[l], k_norm[l], *rope_p,
            (peer_u[l], peer_v[l], w_out[l], w_peer_q[l], (w_in_t, c_r, w_b)), n_p, seq, heads_a)
        zr = _matmul(h, w_r_bf16, tm, 1024, w_is_nk=True)
        o_a_s, k_s = _attn_sample(z, q_norm[l], k_norm[l], *rope_s, cache_attn_k[l], cache_attn_v[l], t_p, dec_seq)
        gla_args = (z, zr, zg, w_gate_p, b_gate[l].reshape(1, -1), gla_norm[l].reshape(1, -1))
        gla_cols = dict(col_q=c_qb, col_k=c_kb, col_v=c_vb)
        o_b_p, s_p = _gla(*gla_args, jnp.zeros((n_p, heads_b, dk, dv), F32), row0=0, chunk=GLA_CHUNK,
                          n_chunks=seq // GLA_CHUNK, out_dtype=BF16, **gla_cols)
        o_b_s, s_s = _gla(*gla_args, state_gla[l], row0=t_p, chunk=dec_seq, n_chunks=1, out_dtype=F32, **gla_cols)

        x1 = _mm_out(o_a_p, o_a_s, o_b_p, o_b_s, w_out_bf16, xp, xs, tm, 512)

        h2 = _rmsnorm(x1, norm_ffn[l], tt)
        q = _matmul(h2, w_peer_q_bf16, tm, 1024)
        ii, jj, g = _route(q, peer_keys1[l].astype(BF16), peer_keys2[l].astype(BF16), LANES)
        gates = _expand(ii, jj, g, _token_tile(t_all, 128))
        c_act = _peer_a(h2, peer_u_bf16, gates, tm, 2 * PEER_SUB)
        yp, ys = _peer_v(c_act, peer_v_bf16, x1, t_p, tm, 1024, 2048)
        if l + 1 < depth:
            x = jnp.concatenate([yp, ys], axis=0)

        outs["kp"].append(k_p.reshape(n_p, seq, heads_a, DH_A))
        outs["vp"].append(z[:t_p, 2 * w_a:3 * w_a].reshape(n_p, seq, heads_a, DH_A))
        outs["ks"].append(k_s.reshape(n_s, dec_seq, heads_a, DH_A))
        outs["vs"].append(z[t_p:, 2 * w_a:3 * w_a].reshape(n_s, dec_seq, heads_a, DH_A))
        outs["sp"].append(s_p)
        outs["ss"].append(s_s)

    y_prompt = yp.reshape(n_p, seq, d_model)
    y_sample = ys.reshape(n_s, dec_seq, d_model)
    return (y_prompt, y_sample, jnp.stack(outs["kp"]), jnp.stack(outs["vp"]), jnp.stack(outs["ks"]),
            jnp.stack(outs["vs"]), jnp.stack(outs["sp"]), jnp.stack(outs["ss"]))
```

```python
import functools

import numpy as np
import jax
import jax.numpy as jnp
from jax import lax
from jax.experimental import pallas as pl
from jax.experimental.pallas import tpu as pltpu

F32 = jnp.float32
BF16 = jnp.bfloat16

DH_A = 128
DILATED_BRANCHES = ((128, 1), (512, 4), (2048, 16))
ROPE_THETA = 10000.0
PAST_LEN = 8192
H_B = 4
GATE_RANK = 16
GATE_TAU = 16.0
GLA_CHUNK = 64
N_KEYS = 128
PEER_HEADS = 8
PEER_DKEY = 256
PEER_TOPK = 16
EPS = 1e-6
NEG_INF = -1e30

LANES = 128
VMEM_LIMIT = 56 * 1024 * 1024


def _cparams(*sem):
    return pltpu.CompilerParams(dimension_semantics=sem, vmem_limit_bytes=VMEM_LIMIT)


def _dot(a, b):
    return jnp.dot(a, b, preferred_element_type=F32)


def _dot_nt(a, b):
    return lax.dot_general(a, b, (((1,), (1,)), ((), ())), preferred_element_type=F32)


def _dot_tn(a, b):
    return lax.dot_general(a, b, (((0,), (0,)), ((), ())), preferred_element_type=F32)


def _rmsnorm_rows(x, gain):
    ms = jnp.mean(x * x, axis=-1, keepdims=True)
    return (x * lax.rsqrt(ms + EPS) * gain).astype(BF16)


def _rmsnorm_body(x_ref, g_ref, o_ref):
    o_ref[...] = _rmsnorm_rows(x_ref[...], g_ref[...])


def _rmsnorm(x, gain, tt):
    t, d = x.shape
    return pl.pallas_call(
        _rmsnorm_body,
        grid=(t // tt,),
        in_specs=[pl.BlockSpec((tt, d), lambda i: (i, 0)), pl.BlockSpec((1, d), lambda i: (0, 0))],
        out_specs=pl.BlockSpec((tt, d), lambda i: (i, 0)),
        out_shape=jax.ShapeDtypeStruct((t, d), BF16),
        compiler_params=_cparams("parallel"),
        name="rmsnorm",
    )(x, gain.reshape(1, d))


def _rmsnorm_pair_body(xp_ref, xs_ref, g_ref, o_ref, *, n_p):
    i = pl.program_id(0)

    @pl.when(i < n_p)
    def _():
        o_ref[...] = _rmsnorm_rows(xp_ref[...], g_ref[...])

    @pl.when(i >= n_p)
    def _():
        o_ref[...] = _rmsnorm_rows(xs_ref[...], g_ref[...])


def _rmsnorm_pair(xp, xs, gain, tt):
    t_p, d = xp.shape
    t_s = xs.shape[0]
    assert t_p % tt == 0 and t_s % tt == 0
    n_p = t_p // tt
    return pl.pallas_call(
        functools.partial(_rmsnorm_pair_body, n_p=n_p),
        grid=((t_p + t_s) // tt,),
        in_specs=[pl.BlockSpec((tt, d), lambda i: (jnp.minimum(i, n_p - 1), 0)),
                  pl.BlockSpec((tt, d), lambda i: (jnp.maximum(i - n_p, 0), 0)),
                  pl.BlockSpec((1, d), lambda i: (0, 0))],
        out_specs=pl.BlockSpec((tt, d), lambda i: (i, 0)),
        out_shape=jax.ShapeDtypeStruct((t_p + t_s, d), BF16),
        compiler_params=_cparams("arbitrary"),
        name="rmsnorm_in",
    )(xp, xs, gain.reshape(1, d))


def _mm_body(a_ref, w_ref, o_ref, *, w_is_nk):
    w = w_ref[...].astype(BF16)
    o_ref[...] = _dot_nt(a_ref[...], w) if w_is_nk else _dot(a_ref[...], w)


def _matmul(a, w, tm, tn, n_cols=None, w_is_nk=False, col0=0):
    t, k = a.shape
    n = (w.shape[0] if w_is_nk else w.shape[1]) if n_cols is None else n_cols
    if w_is_nk:
        assert col0 % 8 == 0 and tn % 8 == 0
        w_spec = pl.BlockSpec((pl.Element(tn), pl.Element(k)), lambda i, j: (pl.multiple_of(col0 + j * tn, 8), 0))
    else:
        assert col0 == 0
        w_spec = pl.BlockSpec((k, tn), lambda i, j: (0, j))
    return pl.pallas_call(
        functools.partial(_mm_body, w_is_nk=w_is_nk),
        grid=(t // tm, n // tn),
        in_specs=[pl.BlockSpec((tm, k), lambda i, j: (i, 0)), w_spec],
        out_specs=pl.BlockSpec((tm, tn), lambda i, j: (i, j)),
        out_shape=jax.ShapeDtypeStruct((t, n), F32),
        compiler_params=_cparams("parallel", "arbitrary"),
        name="matmul",
    )(a, w)


def _mm_in_body(a_ref, w_ref, wg_ref, o_ref, og_ref):
    a = a_ref[...]
    o_ref[...] = _dot_nt(a, w_ref[...].astype(BF16))

    @pl.when(pl.program_id(1) == 0)
    def _():
        og_ref[...] = _dot_nt(a, wg_ref[...].astype(BF16))


def _matmul_in(a, w_t, tm, tn, n_cols, col_g):
    t, k = a.shape
    assert col_g % 8 == 0 and tn % 8 == 0
    return pl.pallas_call(
        _mm_in_body,
        grid=(t // tm, n_cols // tn),
        in_specs=[pl.BlockSpec((tm, k), lambda i, j: (i, 0)),
                  pl.BlockSpec((pl.Element(tn), pl.Element(k)), lambda i, j: (pl.multiple_of(j * tn, 8), 0)),
                  pl.BlockSpec((pl.Element(LANES), pl.Element(k)), lambda i, j: (col_g, 0))],
        out_specs=[pl.BlockSpec((tm, tn), lambda i, j: (i, j)), pl.BlockSpec((tm, LANES), lambda i, j: (i, 0))],
        out_shape=[jax.ShapeDtypeStruct((t, n_cols), F32), jax.ShapeDtypeStruct((t, LANES), F32)],
        compiler_params=_cparams("parallel", "arbitrary"),
        name="matmul_in",
    )(a, w_t, w_t)


def _norm_rope(x, gain, cos2, sin2):
    ms = jnp.mean(x * x, axis=-1, keepdims=True)
    y = x * lax.rsqrt(ms + EPS) * gain
    return y * cos2 + pltpu.roll(y, DH_A // 2, axis=1) * sin2


ATTN_GROUP = 8


def _attn_prompt_body(q_ref, k_ref, v_ref, gq_ref, gk_ref, cos_ref, sin_ref, *rest, seq, n_side):
    side_in, o_ref, kr_ref = rest[:n_side], rest[n_side], rest[n_side + 1]
    side_out, scr = rest[n_side + 2:2 * n_side + 2], rest[2 * n_side + 2:]
    for src, dst in zip(side_in, side_out):
        dst[...] = src[...].astype(dst.dtype)
    blk = DH_A
    n_br = len(DILATED_BRANCHES)
    qs_scr, ks_scr, scr = scr[0], scr[1], scr[2:]
    m_scr, l_scr, acc_scr = scr[:n_br], scr[n_br:2 * n_br], scr[2 * n_br:]

    prep_rows = 4 * blk

    def prep(c, carry):
        rows = pl.ds(pl.multiple_of(c * prep_rows, prep_rows), prep_rows)
        cos2, sin2 = cos_ref[rows, :], sin_ref[rows, :]
        qs_scr[rows, :] = _norm_rope(q_ref[rows, :], gq_ref[...], cos2, sin2) * (DH_A ** -0.5)
        kr = _norm_rope(k_ref[rows, :], gk_ref[...], cos2, sin2)
        ks_scr[rows, :] = kr
        kr_ref[rows, :] = kr
        return carry

    lax.fori_loop(0, seq // prep_rows, prep, 0)
    qi = lax.broadcasted_iota(jnp.int32, (blk, 2 * blk), 0)
    kc = lax.broadcasted_iota(jnp.int32, (blk, 2 * blk), 1)
    band = (kc >= qi) & (kc <= qi + blk)
    cur_half = kc >= blk
    qi1 = lax.broadcasted_iota(jnp.int32, (blk, blk), 0)
    causal = lax.broadcasted_iota(jnp.int32, (blk, blk), 1) <= qi1
    floor_tile = 2.0 * NEG_INF - qi1.astype(F32)

    for bi, (window, d) in enumerate(DILATED_BRANCHES):
        assert window // d == blk
        nb = seq // d // blk

        def blocks(it, carry, bi=bi, d=d, nb=nb):
            rows, vvs, ss = [], [], []
            for u in range(ATTN_GROUP):
                idx = it * ATTN_GROUP + u
                if nb == 1:
                    rw = pl.ds(idx, blk, stride=d)
                    kk = ks_scr[rw, :].astype(BF16)
                    vv = v_ref[rw, :].astype(BF16)
                    mask = causal
                else:
                    r = idx // nb
                    j = idx % nb
                    start = r + j * (blk * d)
                    rw = pl.ds(start, blk, stride=d)
                    prows = pl.ds(jnp.maximum(start - blk * d, r), blk, stride=d)
                    kk = jnp.concatenate([ks_scr[prows, :], ks_scr[rw, :]], axis=0).astype(BF16)
                    vv = jnp.concatenate([v_ref[prows, :], v_ref[rw, :]], axis=0).astype(BF16)
                    mask = band & (cur_half | (j > 0))
                q = qs_scr[rw, :].astype(BF16)
                rows.append(rw)
                vvs.append(vv)
                ss.append(jnp.where(mask, _dot_nt(q, kk), NEG_INF))
            ms = [jnp.max(s, axis=-1, keepdims=True) for s in ss]
            ps = [jnp.exp(s - m) for s, m in zip(ss, ms)]
            for rw, m, p, vv in zip(rows, ms, ps, vvs):
                m_scr[bi][rw, :] = jnp.maximum(m, floor_tile)
                l_scr[bi][rw, :] = jnp.maximum(jnp.sum(p, axis=-1, keepdims=True), floor_tile)
                acc_scr[bi][rw, :] = _dot(p.astype(BF16), vv)
            return carry

        assert (d * nb) % ATTN_GROUP == 0
        lax.fori_loop(0, d * nb // ATTN_GROUP, blocks, 0)

    def merge(c, carry):
        rows = pl.ds(pl.multiple_of(c * blk, blk), blk)
        ms = [m[rows, :] for m in m_scr]
        m_max = functools.reduce(jnp.maximum, ms)
        ws = [jnp.exp(m - m_max) for m in ms]
        num = functools.reduce(jnp.add, [w * a[rows, :] for w, a in zip(ws, acc_scr)])
        den = functools.reduce(jnp.add, [w * l[rows, :] for w, l in zip(ws, l_scr)])
        o_ref[rows, :] = (num / den).astype(o_ref.dtype)
        return carry

    lax.fori_loop(0, seq // blk, merge, 0)


def _attn_prompt(z, q_gain, k_gain, cos2, sin2, sides, n_seq, seq, heads):
    steps = n_seq * heads
    sides = [s if isinstance(s, tuple) else (s, 0, s.shape[0]) for s in sides]
    assert all(n % steps == 0 and (n // steps) % 16 == 0 and r0 % 8 == 0 for _, r0, n in sides)
    side_specs = [pl.BlockSpec((n // steps, s.shape[1]), lambda b, h: (b * heads + h, 0)) for s, _, n in sides]
    side_in_specs = [
        pl.BlockSpec((pl.Element(n // steps), pl.Element(s.shape[1])),
                     lambda b, h, r0=r0, slab=n // steps: (pl.multiple_of(r0 + (b * heads + h) * slab, 8), 0))
        for s, r0, n in sides]
    head_spec = pl.BlockSpec((seq, DH_A), lambda b, h: (b, h))
    vec_spec = pl.BlockSpec((1, DH_A), lambda b, h: (0, 0))
    tab_spec = pl.BlockSpec((seq, DH_A), lambda b, h: (0, 0))
    outs = pl.pallas_call(
        functools.partial(_attn_prompt_body, seq=seq, n_side=len(sides)),
        grid=(n_seq, heads),
        in_specs=[head_spec,
                  pl.BlockSpec((seq, DH_A), lambda b, h: (b, heads + h)),
                  pl.BlockSpec((seq, DH_A), lambda b, h: (b, 2 * heads + h)),
                  vec_spec, vec_spec, tab_spec, tab_spec] + side_in_specs,
        out_specs=[head_spec, head_spec] + side_specs,
        out_shape=[jax.ShapeDtypeStruct((n_seq * seq, heads * DH_A), BF16),
                   jax.ShapeDtypeStruct((n_seq * seq, heads * DH_A), F32)]
        + [jax.ShapeDtypeStruct((n, s.shape[1]), BF16) for s, _, n in sides],
        scratch_shapes=[pltpu.VMEM((seq, DH_A), F32)] * (2 + 3 * len(DILATED_BRANCHES)),
        compiler_params=_cparams("parallel", "arbitrary"),
        name="attn_prompt",
    )(z, z, z, q_gain.reshape(1, DH_A), k_gain.reshape(1, DH_A), cos2, sin2, *[s for s, _, _ in sides])
    return outs[0], outs[1], outs[2:]


SAMPLE_CHUNK_ROWS = 4096
SAMPLE_STREAMS = 2


def _sample_plan(win_buf, dec_seq, heads):
    dil = DILATED_BRANCHES[-1][1]
    dense_from = win_buf - DILATED_BRANCHES[-2][0]
    assert win_buf % dil == 0 and dense_from % dil == 0 and dec_seq * 2 == dil and dec_seq * heads == LANES
    pos_sparse = SAMPLE_CHUNK_ROWS // (dec_seq * heads) * dil
    pos_dense = SAMPLE_CHUNK_ROWS // heads
    assert dense_from % pos_sparse == 0 and (win_buf - dense_from) % pos_dense == 0 and dense_from % pos_dense == 0
    return dil, dense_from, dense_from // pos_sparse, (win_buf - dense_from) // pos_dense


def _sample_weights(win_buf, dec_seq, heads):
    dil, dense_from, n_sparse, n_dense = _sample_plan(win_buf, dec_seq, heads)
    n = np.arange(win_buf + dec_seq)[None, :]
    s = np.arange(dec_seq)[:, None]
    dist = win_buf + s - n
    cnt = np.zeros((dec_seq, win_buf + dec_seq), np.float32)
    for window, d in DILATED_BRANCHES:
        cnt += (dist >= 0) & (dist <= window) & (dist % d == 0)
    assert not cnt[:, :dense_from].reshape(dec_seq, -1, dil)[:, :, dec_seq:].any()
    eye = np.eye(heads, dtype=np.float32)
    sparse = cnt[:, :dense_from].reshape(dec_seq, -1, dil)[:, :, :dec_seq].reshape(dec_seq, -1)
    sparse = np.einsum("sn,hg->nhgs", sparse, eye).reshape(n_sparse, SAMPLE_CHUNK_ROWS, LANES)
    dense = np.einsum("sn,hg->nhgs", cnt[:, dense_from:win_buf], eye).reshape(n_dense, SAMPLE_CHUNK_ROWS, LANES)
    new = np.einsum("sn,hg->hngs", cnt[:, win_buf:], eye).reshape(LANES, LANES)
    return np.concatenate([sparse, dense], axis=0), new


def _attn_sample_body(q_ref, ks_ref, kd_ref, vs_ref, vd_ref, kn_ref, vn_ref, w_ref, wn_ref, gq_ref, gk_ref,
                      cos_ref, sin_ref, o_ref, kr_ref, q_scr, m_scr, l_scr, acc_scr,
                      *, heads, dec_seq, n_sparse, n_dense):
    c = pl.program_id(1)
    n_chunks = n_sparse + n_dense

    def by_head(ref, fn=lambda x: x):
        return jnp.concatenate([fn(ref[:, h * DH_A:(h + 1) * DH_A]) for h in range(heads)], axis=0)

    def rotated(gain_ref):
        return lambda x: _norm_rope(x, gain_ref[...], cos_ref[...], sin_ref[...])

    @pl.when(c == 0)
    def _():
        q_scr[...] = (by_head(q_ref, rotated(gq_ref)) * (DH_A ** -0.5)).astype(BF16)
        m_scr[...] = jnp.full(m_scr.shape, NEG_INF, F32)
        l_scr[...] = jnp.zeros(l_scr.shape, F32)
        acc_scr[...] = jnp.zeros(acc_scr.shape, F32)

    def step(kb, vb, w, n_streams):
        rows = kb.shape[0] // n_streams
        part = [slice(t * rows, (t + 1) * rows) for t in range(n_streams)]
        s = [jnp.where(w[r] > 0, _dot_nt(kb[r], q_scr[...]), NEG_INF) for r in part]
        m_old = [m_scr[t] for t in range(n_streams)]
        m_new = [jnp.maximum(m_old[t], jnp.max(s[t], axis=0, keepdims=True)) for t in range(n_streams)]
        alpha = [jnp.exp(m_old[t] - m_new[t]) for t in range(n_streams)]
        p = [w[part[t]] * jnp.exp(s[t] - m_new[t]) for t in range(n_streams)]
        for t in range(n_streams):
            l_scr[t] = alpha[t] * l_scr[t] + jnp.sum(p[t], axis=0, keepdims=True)
            acc_scr[t] = alpha[t] * acc_scr[t] + _dot_tn(vb[part[t]], p[t].astype(BF16))
            m_scr[t] = m_new[t]

    @pl.when(c < n_sparse)
    def _():
        rows = (SAMPLE_CHUNK_ROWS, DH_A)
        step(ks_ref[0].reshape(rows).astype(BF16), vs_ref[0].reshape(rows).astype(BF16), w_ref[c], SAMPLE_STREAMS)

    @pl.when((c >= n_sparse) & (c < n_chunks))
    def _():
        step(kd_ref[0].astype(BF16), vd_ref[0].astype(BF16), w_ref[c], SAMPLE_STREAMS)

    @pl.when(c == n_chunks)
    def _():
        kn = by_head(kn_ref, rotated(gk_ref))
        step(kn.astype(BF16), by_head(vn_ref).astype(BF16), wn_ref[...], 1)
        ms = [m_scr[t] for t in range(SAMPLE_STREAMS)]
        m_all = functools.reduce(jnp.maximum, ms)
        wt = [jnp.exp(m - m_all) for m in ms]
        num = functools.reduce(jnp.add, [wt[t] * acc_scr[t] for t in range(SAMPLE_STREAMS)])
        den = functools.reduce(jnp.add, [wt[t] * l_scr[t] for t in range(SAMPLE_STREAMS)])
        out = (num / den).T
        for h in range(heads):
            o_ref[:, h * DH_A:(h + 1) * DH_A] = out[h * dec_seq:(h + 1) * dec_seq, :]
            kr_ref[:, h * DH_A:(h + 1) * DH_A] = kn[h * dec_seq:(h + 1) * dec_seq, :]


def _attn_sample(z, q_gain, k_gain, cos2, sin2, cache_k, cache_v, row0, dec_seq):
    n_seq, win_buf, heads, _ = cache_k.shape
    width = heads * DH_A
    dil, dense_from, n_sparse, n_dense = _sample_plan(win_buf, dec_seq, heads)
    n_chunks = n_sparse + n_dense
    w_main, w_new = _sample_weights(win_buf, dec_seq, heads)
    rb = row0 // dec_seq
    groups = SAMPLE_CHUNK_ROWS // LANES
    dense0 = dense_from * heads // SAMPLE_CHUNK_ROWS

    def new_spec(col):
        return pl.BlockSpec((dec_seq, width), lambda b, c: (rb + b, col))

    def small(shape):
        return pl.BlockSpec(shape, lambda b, c: (0, 0))

    out_spec = pl.BlockSpec((dec_seq, width), lambda b, c: (b, 0))

    sparse_view = lambda x: x.reshape(n_seq, win_buf // dil, dil * heads, DH_A)
    dense_view = lambda x: x.reshape(n_seq, win_buf * heads, DH_A)
    sparse_spec = pl.BlockSpec((1, groups, LANES, DH_A), lambda b, c: (b, jnp.minimum(c, n_sparse - 1), 0, 0))
    dense_spec = pl.BlockSpec((1, SAMPLE_CHUNK_ROWS, DH_A),
                              lambda b, c: (b, dense0 + jnp.clip(c - n_sparse, 0, n_dense - 1), 0))
    return pl.pallas_call(
        functools.partial(_attn_sample_body, heads=heads, dec_seq=dec_seq, n_sparse=n_sparse, n_dense=n_dense),
        grid=(n_seq, n_chunks + 1),
        in_specs=[new_spec(0), sparse_spec, dense_spec, sparse_spec, dense_spec, new_spec(1), new_spec(2),
                  pl.BlockSpec(w_main.shape, lambda b, c: (0, 0, 0)),
                  pl.BlockSpec(w_new.shape, lambda b, c: (0, 0)),
                  small((1, DH_A)), small((1, DH_A)), small((dec_seq, DH_A)), small((dec_seq, DH_A))],
        out_specs=[out_spec, out_spec],
        out_shape=[jax.ShapeDtypeStruct((n_seq * dec_seq, width), F32)] * 2,
        scratch_shapes=[pltpu.VMEM((LANES, DH_A), BF16), pltpu.VMEM((SAMPLE_STREAMS, 1, LANES), F32),
                        pltpu.VMEM((SAMPLE_STREAMS, 1, LANES), F32), pltpu.VMEM((SAMPLE_STREAMS, DH_A, LANES), F32)],
        compiler_params=_cparams("parallel", "arbitrary"),
        name="attn_sample",
    )(z, sparse_view(cache_k), dense_view(cache_k), sparse_view(cache_v), dense_view(cache_v), z, z,
      jnp.asarray(w_main), jnp.asarray(w_new), q_gain.reshape(1, DH_A), k_gain.reshape(1, DH_A),
      cos2, sin2)


def _gla_body(q_ref, k_ref, v_ref, r_ref, gl_ref, wg_ref, bg_ref, gn_ref, s0_ref, o_ref, sn_ref, st_scr,
              *, chunk, heads, dk, dv):
    c = pl.program_id(1)
    hs = range(heads)

    @pl.when(c == 0)
    def _():
        for h in hs:
            st_scr[h] = s0_ref[0, h]

    cp = max(chunk, LANES // 2)

    def pad(a):
        return a if cp == chunk else jnp.concatenate([a, jnp.zeros((cp - chunk, a.shape[1]), a.dtype)], axis=0)

    def cols(ref, h, w):
        return ref[:, h * w:(h + 1) * w]

    ti = lax.broadcasted_iota(jnp.int32, (cp, cp), 0)
    tj = lax.broadcasted_iota(jnp.int32, (cp, cp), 1)
    tril = ti >= tj
    trilf = tril.astype(F32)
    glow = gl_ref[...].astype(BF16)
    log_a = [pad(jax.nn.log_sigmoid(_dot(glow, cols(wg_ref, h, dk)) + cols(bg_ref, h, dk)) / GATE_TAU) for h in hs]
    b = [jnp.dot(trilf, la, preferred_element_type=F32, precision=lax.Precision.HIGHEST) for la in log_a]
    b_last = [x[cp - 1:cp, :] for x in b]
    k = [pad(cols(k_ref, h, dk)) for h in hs]
    v = [pad(cols(v_ref, h, dv)).astype(BF16) for h in hs]
    q_in = [(pad(cols(q_ref, h, dk)) * (dk ** -0.5) * jnp.exp(b[h])).astype(BF16) for h in hs]
    k_in = [(k[h] * jnp.exp(-b[h])).astype(BF16) for h in hs]
    aug_t = [jnp.concatenate([k[h] * jnp.exp(b_last[h] - b[h]), jnp.broadcast_to(jnp.exp(b_last[h]), (cp, dk))],
                             axis=0).T for h in hs]
    att = [jnp.where(tril, _dot_nt(q_in[h], k_in[h]), 0.0).astype(BF16) for h in hs]
    st = [st_scr[h] for h in hs]
    o = [(_dot(att[h], v[h]) + _dot(q_in[h], st[h].astype(BF16)))[:chunk] for h in hs]
    st_new = [st[h] * aug_t[h][:, cp:cp + 1] + _dot(aug_t[h][:, :cp].astype(BF16), v[h]) for h in hs]
    for h in hs:
        st_scr[h] = st_new[h]
        ms = jnp.mean(o[h] * o[h], axis=-1, keepdims=True)
        gated = (o[h] * lax.rsqrt(ms + EPS) * gn_ref[...]) * jax.nn.silu(cols(r_ref, h, dv))
        o_ref[:, h * dv:(h + 1) * dv] = gated.astype(o_ref.dtype)

    @pl.when(c == pl.num_programs(1) - 1)
    def _():
        for h in hs:
            sn_ref[0, h] = st_new[h]


def _gla(z, zr, zg, w_gate, b_gate, gla_norm, s0, *, row0, chunk, n_chunks, col_q, col_k, col_v, out_dtype):
    n_seq, heads, dk, dv = s0.shape
    rb = row0 // chunk
    wk, wv = heads * dk, heads * dv
    assert col_q % wk == 0 and col_k % wk == 0 and col_v % wv == 0 and zr.shape[1] == wv

    def rows(b, c):
        return rb + b * n_chunks + c

    return pl.pallas_call(
        functools.partial(_gla_body, chunk=chunk, heads=heads, dk=dk, dv=dv),
        grid=(n_seq, n_chunks),
        in_specs=[pl.BlockSpec((chunk, wk), lambda b, c: (rows(b, c), col_q // wk)),
                  pl.BlockSpec((chunk, wk), lambda b, c: (rows(b, c), col_k // wk)),
                  pl.BlockSpec((chunk, wv), lambda b, c: (rows(b, c), col_v // wv)),
                  pl.BlockSpec((chunk, wv), lambda b, c: (rows(b, c), 0)),
                  pl.BlockSpec((chunk, LANES), lambda b, c: (rows(b, c), 0)),
                  pl.BlockSpec((LANES, wk), lambda b, c: (0, 0)),
                  pl.BlockSpec((1, wk), lambda b, c: (0, 0)),
                  pl.BlockSpec((1, dv), lambda b, c: (0, 0)),
                  pl.BlockSpec((1, heads, dk, dv), lambda b, c: (b, 0, 0, 0))],
        out_specs=[pl.BlockSpec((chunk, wv), lambda b, c: (b * n_chunks + c, 0)),
                   pl.BlockSpec((1, heads, dk, dv), lambda b, c: (b, 0, 0, 0))],
        out_shape=[jax.ShapeDtypeStruct((n_seq * n_chunks * chunk, wv), out_dtype),
                   jax.ShapeDtypeStruct((n_seq, heads, dk, dv), F32)],
        scratch_shapes=[pltpu.VMEM((heads, dk, dv), F32)],
        compiler_params=_cparams("parallel", "arbitrary"),
        name="gla",
    )(z, z, z, zr, zg, w_gate, b_gate, gla_norm, s0)


def _mm_out_body(a1p_ref, a1s_ref, a2p_ref, a2s_ref, w1_ref, w2_ref, xp_ref, xs_ref, o_ref, *, n_full, rem):
    i = pl.program_id(0)

    def go(a1, a2, x):
        o_ref[...] = x + _dot(a1, w1_ref[...]) + _dot(a2, w2_ref[...])

    @pl.when(i < n_full)
    def _():
        go(a1p_ref[...], a2p_ref[...], xp_ref[...])

    @pl.when(i == n_full)
    def _():
        def cat(p_ref, s_ref):
            return jnp.concatenate([p_ref[:rem], s_ref[...].astype(p_ref.dtype)], axis=0)

        go(cat(a1p_ref, a1s_ref), cat(a2p_ref, a2s_ref), cat(xp_ref, xs_ref))


def _mm_out(a1p, a1s, a2p, a2s, w, xp, xs, tm, tn):
    t_p, k1 = a1p.shape
    t_s = a1s.shape[0]
    k2 = a2p.shape[1]
    n = w.shape[1]
    n_full, rem = divmod(t_p, tm)
    assert rem > 0 and rem + t_s == tm and k1 == k2 and w.shape[0] == k1 + k2

    def pspec(k):
        return pl.BlockSpec((tm, k), lambda i, j: (i, 0))

    def sspec(k):
        return pl.BlockSpec((t_s, k), lambda i, j: (0, 0))

    return pl.pallas_call(
        functools.partial(_mm_out_body, n_full=n_full, rem=rem),
        grid=(n_full + 1, n // tn),
        in_specs=[pspec(k1), sspec(k1), pspec(k2), sspec(k2),
                  pl.BlockSpec((k1, tn), lambda i, j: (0, j)), pl.BlockSpec((k2, tn), lambda i, j: (1, j)),
                  pl.BlockSpec((tm, tn), lambda i, j: (i, j)), pl.BlockSpec((t_s, tn), lambda i, j: (0, j))],
        out_specs=pl.BlockSpec((tm, tn), lambda i, j: (i, j)),
        out_shape=jax.ShapeDtypeStruct((t_p + t_s, n), F32),
        compiler_params=_cparams("parallel", "arbitrary"),
        name="mm_out",
    )(a1p, a1s, a2p, a2s, w, w, xp, xs)


def _rows_iota(shape):
    return lax.broadcasted_iota(jnp.int32, shape, 0).astype(F32)


def _topk_cols(s, ranks, k):
    big = float(2 ** 23)
    for _ in range(k):
        m = jnp.max(s, axis=0, keepdims=True)
        r = jnp.min(jnp.where(s == m, ranks, big), axis=0, keepdims=True)
        sel = ranks == r
        yield m, r, sel
        s = jnp.where(sel, -jnp.inf, s)


def _stack_rows(rows):
    n = len(rows)
    rid = lax.broadcasted_iota(jnp.int32, (n, rows[0].shape[1]), 0)
    out = jnp.broadcast_to(rows[0], (n, rows[0].shape[1]))
    for i in range(1, n):
        out = jnp.where(rid == i, rows[i], out)
    return out


def _route_body(q_ref, k1_ref, k2_ref, ii_ref, jj_ref, g_ref):
    tt = q_ref.shape[0]
    half = PEER_DKEY // 2
    topk = PEER_TOPK
    key_rank = _rows_iota((N_KEYS, tt))
    r8 = _rows_iota((8, tt))
    r16 = _rows_iota((topk, tt))
    gates, experts = [], []
    for h in range(PEER_HEADS):
        qh = q_ref[:, h * PEER_DKEY:(h + 1) * PEER_DKEY]
        qn = (qh * lax.rsqrt(jnp.mean(qh * qh, axis=-1, keepdims=True) + EPS)).astype(BF16)
        s1 = _dot_nt(k1_ref[h], qn[:, :half])
        s2 = _dot_nt(k2_ref[h], qn[:, half:])
        v1, i1 = zip(*[(m, r) for m, r, _ in _topk_cols(s1, key_rank, topk)])
        v2, i2 = zip(*[(m, r) for m, r, _ in _topk_cols(s2, key_rank, topk)])
        v1t, i1t, v2t, i2t = (_stack_rows(list(x)) for x in (v1, i1, v2, i2))
        cand = [v1[0] + v2t]
        code = [i1[0] * N_KEYS + i2t]
        flat = [r16]
        for a in range(1, 8):
            ok = r8 < float(topk // (a + 1))
            cand.append(jnp.where(ok, v1[a] + v2t[:8], -jnp.inf))
            code.append(i1[a] * N_KEYS + i2t[:8])
            flat.append(r8 + float(a * topk))
        cand.append(v1t[8:] + v2[0])
        code.append(i1t[8:] * N_KEYS + i2[0])
        flat.append((r8 + 8.0) * float(topk))
        cand, code, flat = (jnp.concatenate(x, axis=0) for x in (cand, code, flat))
        top, picked = zip(*[(m, r) for m, r, _ in _topk_cols(cand, flat * float(N_KEYS * N_KEYS) + code, topk)])
        ex = [jnp.exp(m - top[0]) for m in top]
        inv_z = 1.0 / functools.reduce(jnp.add, ex)
        gates.append(_stack_rows([e * inv_z for e in ex]))
        experts.append(_stack_rows(list(picked)))
    g_ref[...] = jnp.concatenate(gates, axis=0).T
    e = jnp.concatenate(experts, axis=0).T.astype(jnp.int32)
    e = lax.bitwise_and(e, N_KEYS * N_KEYS - 1)
    ii_ref[...] = lax.shift_right_logical(e, N_KEYS.bit_length() - 1)
    jj_ref[...] = lax.bitwise_and(e, N_KEYS - 1)


def _route(q, keys1, keys2, tt):
    t, n = q.shape
    no = PEER_HEADS * PEER_TOPK
    kspec = pl.BlockSpec(keys1.shape, lambda i: (0, 0, 0))
    ospec = pl.BlockSpec((tt, no), lambda i: (i, 0))
    return pl.pallas_call(
        _route_body,
        grid=(t // tt,),
        in_specs=[pl.BlockSpec((tt, n), lambda i: (i, 0)), kspec, kspec],
        out_specs=[ospec, ospec, ospec],
        out_shape=[jax.ShapeDtypeStruct((t, no), jnp.int32), jax.ShapeDtypeStruct((t, no), jnp.int32),
                   jax.ShapeDtypeStruct((t, no), F32)],
        compiler_params=_cparams("parallel"),
        name="peer_route",
    )(q, keys1, keys2)


EXPAND_GROUP = 16
EXPAND_PITCH = N_KEYS + 8
EXPAND_BUFS = 4


def _expand_body(ii_ref, jj_ref, g_ref, o_ref, *scrs):
    te = ii_ref.shape[0]
    sub = lax.broadcasted_iota(jnp.int32, (N_KEYS, ii_ref.shape[1]), 0)
    per_trip = len(scrs) * EXPAND_GROUP

    def trip(ti, carry):
        bases = [pl.multiple_of(ti * per_trip + n * EXPAND_GROUP, EXPAND_GROUP) for n in range(len(scrs))]
        for base, scr in zip(bases, scrs):
            for u in range(EXPAND_GROUP):
                ii = ii_ref[pl.ds(base + u, 1), :]
                jj = jj_ref[pl.ds(base + u, 1), :]
                g = g_ref[pl.ds(base + u, 1), :]
                a = jnp.where(sub == ii, 1.0, 0.0).astype(BF16)
                b = jnp.where(sub == jj, g, 0.0).astype(BF16)
                scr[u * EXPAND_PITCH:u * EXPAND_PITCH + N_KEYS, :] = _dot_nt(a, b)
        for base, scr in zip(bases, scrs):
            for i in range(N_KEYS):
                rows = scr[pl.ds(i, EXPAND_GROUP, stride=EXPAND_PITCH), :]
                o_ref[pl.ds(base, EXPAND_GROUP), i * N_KEYS:(i + 1) * N_KEYS] = rows.astype(o_ref.dtype)
        return carry

    assert te % per_trip == 0
    lax.fori_loop(0, te // per_trip, trip, 0)


def _expand(ii, jj, g, te):
    t, no = ii.shape
    spec = pl.BlockSpec((te, no), lambda i: (i, 0))
    return pl.pallas_call(
        _expand_body,
        grid=(t // te,),
        in_specs=[spec, spec, spec],
        out_specs=pl.BlockSpec((te, N_KEYS * N_KEYS), lambda i: (i, 0)),
        out_shape=jax.ShapeDtypeStruct((t, N_KEYS * N_KEYS), BF16),
        scratch_shapes=[pltpu.VMEM((EXPAND_GROUP * EXPAND_PITCH, N_KEYS), F32)] * EXPAND_BUFS,
        compiler_params=_cparams("parallel"),
        name="peer_expand",
    )(ii, jj, g)


PEER_SUB = 512


def _peer_a_body(h_ref, u_ref, g_ref, o_ref):
    h = h_ref[...]
    for s in range(o_ref.shape[1] // PEER_SUB):
        cols = slice(s * PEER_SUB, (s + 1) * PEER_SUB)
        a = _dot_nt(h, u_ref[cols, :])
        o_ref[:, cols] = (g_ref[:, cols].astype(F32) * jax.nn.gelu(a)).astype(o_ref.dtype)


def _peer_a(h, u, gates, tm, tn):
    t, k = h.shape
    e = u.shape[0]
    return pl.pallas_call(
        _peer_a_body,
        grid=(t // tm, e // tn),
        in_specs=[pl.BlockSpec((tm, k), lambda i, j: (i, 0)), pl.BlockSpec((tn, k), lambda i, j: (j, 0)),
                  pl.BlockSpec((tm, tn), lambda i, j: (i, j))],
        out_specs=pl.BlockSpec((tm, tn), lambda i, j: (i, j)),
        out_shape=jax.ShapeDtypeStruct((t, e), BF16),
        compiler_params=_cparams("parallel", "arbitrary"),
        name="peer_act",
    )(h, u, gates)


def _peer_v_body(c_ref, v_ref, x_ref, yp_ref, ys_ref, acc_scr, *, n_full, rem):
    i = pl.program_id(0)
    kk = pl.program_id(2)
    last = pl.num_programs(2) - 1

    @pl.when(kk == 0)
    def _():
        acc_scr[...] = x_ref[...]

    acc_scr[...] += _dot(c_ref[...], v_ref[...].astype(BF16))

    @pl.when(kk == last)
    def _():
        yp_ref[...] = acc_scr[...]

    @pl.when((kk == last) & (i == n_full))
    def _():
        ys_ref[...] = acc_scr[rem:, :]


def _peer_v(c, v, x, t_p, tm, tn, tk):
    t, e = c.shape
    n = v.shape[1]
    t_s = t - t_p
    n_full, rem = divmod(t_p, tm)
    assert rem > 0 and rem + t_s == tm
    return pl.pallas_call(
        functools.partial(_peer_v_body, n_full=n_full, rem=rem),
        grid=(t // tm, n // tn, e // tk),
        in_specs=[pl.BlockSpec((tm, tk), lambda i, j, k: (i, k)), pl.BlockSpec((tk, tn), lambda i, j, k: (k, j)),
                  pl.BlockSpec((tm, tn), lambda i, j, k: (i, j))],
        out_specs=[pl.BlockSpec((tm, tn), lambda i, j, k: (i, j)),
                   pl.BlockSpec((t_s, tn), lambda i, j, k: (0, jnp.where(i == n_full, j, 0)))],
        out_shape=[jax.ShapeDtypeStruct((t_p, n), F32), jax.ShapeDtypeStruct((t_s, n), F32)],
        scratch_shapes=[pltpu.VMEM((tm, tn), F32)],
        compiler_params=_cparams("arbitrary", "arbitrary", "arbitrary"),
        name="peer_mix",
    )(c, v, x)


def _token_tile(t, cap):
    best = 16
    for c in range(16, cap + 1, 16):
        if t % c == 0:
            best = c
    return best


def _rope_tables(pos):
    half = DH_A // 2
    inv = ROPE_THETA ** (-jnp.arange(half, dtype=F32) / half)
    ang = pos.astype(F32)[:, None] * inv[None, :]
    cos, sin = jnp.cos(ang), jnp.sin(ang)
    return jnp.concatenate([cos, cos], axis=1), jnp.concatenate([-sin, sin], axis=1)


def kernel(x_prompt, x_sample, cache_attn_k, cache_attn_v, state_gla, norm_mix, w_in, q_norm, k_norm, w_gate, b_gate,
           gla_norm, w_out, norm_ffn, w_peer_q, peer_keys1, peer_keys2, peer_u, peer_v):
    n_p, seq, d_model = x_prompt.shape
    n_s, dec_seq, _ = x_sample.shape
    depth = w_in.shape[0]
    heads_a = cache_attn_k.shape[3]
    w_a = heads_a * DH_A
    _, _, heads_b, dk, dv = state_gla.shape
    w_b = heads_b * dv
    assert seq == DILATED_BRANCHES[-1][0] and heads_b == H_B and cache_attn_k.shape[2] == seq
    t_p = n_p * seq
    t_s = n_s * dec_seq
    t_all = t_p + t_s
    tm = _token_tile(t_all, 1056)
    tt = _token_tile(t_s, 256)

    xp = x_prompt.reshape(t_p, d_model)
    xs = x_sample.reshape(t_s, d_model)
    rope_p = _rope_tables(jnp.arange(seq, dtype=jnp.int32))
    rope_s = _rope_tables(PAST_LEN + jnp.arange(dec_seq, dtype=jnp.int32))

    c_qb = 3 * w_a
    c_kb = c_qb + heads_b * dk
    c_vb = c_kb + heads_b * dk
    c_g = c_vb + w_b
    c_r = c_g + GATE_RANK

    outs = dict(kp=[], vp=[], ks=[], vs=[], sp=[], ss=[])
    for l in range(depth):
        w_in_t = jnp.swapaxes(w_in[l], 0, 1)
        w_gate_p = jnp.pad(w_gate[l].astype(BF16), ((0, LANES - GATE_RANK), (0, 0)))

        if l == 0:
            h = _rmsnorm_pair(xp, xs, norm_mix[l], tt)
        else:
            h = _rmsnorm(x, norm_mix[l], tt)
            xp, xs = x[:t_p], x[t_p:]
        z, zg = _matmul_in(h, w_in_t, tm, 512, c_g, c_g)

        o_a_p, k_p, (peer_u_bf16, peer_v_bf16, w_out_bf16, w_peer_q_bf16, w_r_bf16) = _attn_prompt(
            z, q_norm[l], k_norm[l], *rope_p,
            (peer_u[l], peer_v[l], w_out[l], w_peer_q[l], (w_in_t, c_r, w_b)), n_p, seq, heads_a)
        zr = _matmul(h, w_r_bf16, tm, 1024, w_is_nk=True)
        o_a_s, k_s = _attn_sample(z, q_norm[l], k_norm[l], *rope_s, cache_attn_k[l], cache_attn_v[l], t_p, dec_seq)
        gla_args = (z, zr, zg, w_gate_p, b_gate[l].reshape(1, -1), gla_norm[l].reshape(1, -1))
        gla_cols = dict(col_q=c_qb, col_k=c_kb, col_v=c_vb)
        o_b_p, s_p = _gla(*gla_args, jnp.zeros((n_p, heads_b, dk, dv), F32), row0=0, chunk=GLA_CHUNK,
                          n_chunks=seq // GLA_CHUNK, out_dtype=BF16, **gla_cols)
        o_b_s, s_s = _gla(*gla_args, state_gla[l], row0=t_p, chunk=dec_seq, n_chunks=1, out_dtype=F32, **gla_cols)

        x1 = _mm_out(o_a_p, o_a_s, o_b_p, o_b_s, w_out_bf16, xp, xs, tm, 512)

        h2 = _rmsnorm(x1, norm_ffn[l], tt)
        q = _matmul(h2, w_peer_q_bf16, tm, 1024)
        ii, jj, g = _route(q, peer_keys1[l].astype(BF16), peer_keys2[l].astype(BF16), 2 * LANES)
        gates = _expand(ii, jj, g, _token_tile(t_all, 128))
        c_act = _peer_a(h2, peer_u_bf16, gates, tm, 2 * PEER_SUB)
        yp, ys = _peer_v(c_act, peer_v_bf16, x1, t_p, tm, 1024, 2048)
        if l + 1 < depth:
            x = jnp.concatenate([yp, ys], axis=0)

        outs["kp"].append(k_p.reshape(n_p, seq, heads_a, DH_A))
        outs["vp"].append(z[:t_p, 2 * w_a:3 * w_a].reshape(n_p, seq, heads_a, DH_A))
        outs["ks"].append(k_s.reshape(n_s, dec_seq, heads_a, DH_A))
        outs["vs"].append(z[t_p:, 2 * w_a:3 * w_a].reshape(n_s, dec_seq, heads_a, DH_A))
        outs["sp"].append(s_p)
        outs["ss"].append(s_s)

    y_prompt = yp.reshape(n_p, seq, d_model)
    y_sample = ys.reshape(n_s, dec_seq, d_model)
    return (y_prompt, y_sample, jnp.stack(outs["kp"]), jnp.stack(outs["vp"]), jnp.stack(outs["ks"]),
            jnp.stack(outs["vs"]), jnp.stack(outs["sp"]), jnp.stack(outs["ss"]))
```
